```python
import math
import jax
import jax.numpy as jnp
from jax import lax
import numpy as np

D_MODEL = 1024
BATCH = 8
SEQ = 2048
DEPTH = 2
DEC_BATCH = 128
DEC_SEQ = 4
PAST_LEN = 16384
PAGE_SIZE = 128

MIX_WIDTH = D_MODEL
GROUP_WIDTH = MIX_WIDTH // 4
HEAD_DIM = 64
A_WIDTH = GROUP_WIDTH
A_HEADS = A_WIDTH // HEAD_DIM
A_CHUNK = 128
B_WIDTH = GROUP_WIDTH
B_HEADS = B_WIDTH // HEAD_DIM
B_DK = HEAD_DIM
B_DV = HEAD_DIM
B_CONV = 4
B_CHUNK = 64
C_WIDTH = GROUP_WIDTH
C_HEADS = C_WIDTH // HEAD_DIM
C_HD = HEAD_DIM
C_DECAY_LORA = 64
C_AAA_LORA = 64
C_GATE_LORA = 128
C_PROJ = 3 * C_WIDTH + C_DECAY_LORA + C_AAA_LORA + C_GATE_LORA
D_WIDTH = GROUP_WIDTH
POOL_WINDOWS = (2, 4, 8, 16)
D_GROUPS = len(POOL_WINDOWS)
D_GC = D_WIDTH // D_GROUPS
POOL_BUF = max(POOL_WINDOWS) - 1
D_FF = 4 * D_MODEL
NORM_EPS = 1e-6
L2_EPS = 1e-6
GN_EPS = 64e-5
A_COLS = 2 * A_WIDTH
B_COLS = 4 * B_WIDTH + 2 * B_HEADS
B_OFF = A_COLS
C_OFF = B_OFF + B_COLS
D_OFF = C_OFF + C_PROJ
IN_COLS = D_OFF + D_WIDTH

kernel_name = 'hybrid_gmlp_gdn_rwkv7_pool_step'


def rmsnorm(x, g):
    xf = x.astype(jnp.float32)
    y = xf * lax.rsqrt(jnp.mean(xf * xf, axis=-1, keepdims=True) + NORM_EPS)
    return (y * g.astype(jnp.float32)).astype(x.dtype)


def l2norm(x):
    xf = x.astype(jnp.float32)
    return xf * lax.rsqrt(jnp.sum(xf * xf, axis=-1, keepdims=True) + L2_EPS)


def chunk_gmlp(ap, ws, bs, vg):
    bn, L, _ = ap.shape
    u = jax.nn.gelu(ap[..., :A_WIDTH])
    v = rmsnorm(jax.nn.gelu(ap[..., A_WIDTH:]), vg)
    pad = (-L) % A_CHUNK
    vc = jnp.pad(v, ((0, 0), (0, pad), (0, 0))).reshape(bn, -1, A_CHUNK, A_HEADS, HEAD_DIM)
    causal = jnp.tril(jnp.ones((A_CHUNK, A_CHUNK), dtype=bool))
    wm = jnp.where(causal, ws, jnp.zeros_like(ws))
    mix = jnp.einsum('hts,bnshd->bnthd', wm, vc) + bs.T[None, None, :, :, None]
    mix = mix.reshape(bn, -1, A_WIDTH)[:, :L]
    return u * mix, v


def causal_dwconv(x, buf, w):
    K = w.shape[0]
    L = x.shape[1]
    xp = jnp.concatenate([buf.astype(x.dtype), x], axis=1)
    out = xp[:, 0:L] * w[0]
    for i in range(1, K):
        out = out + xp[:, i:i + L] * w[i]
    return out, xp[:, -(K - 1):]


def gated_delta_chunked(q, k, v, g, beta, S0):
    bn, L, H, dk = q.shape
    dv = v.shape[-1]
    C = B_CHUNK
    pad = (-L) % C
    if pad:
        pw = ((0, 0), (0, pad), (0, 0), (0, 0))
        q, k, v = jnp.pad(q, pw), jnp.pad(k, pw), jnp.pad(v, pw)
        g, beta = jnp.pad(g, pw[:3]), jnp.pad(beta, pw[:3])
    N = (L + pad) // C

    def blk(t):
        return jnp.moveaxis(t.reshape((bn, N, C) + t.shape[2:]), 3, 2)

    q, k, v, g, beta = blk(q), blk(k), blk(v), blk(g), blk(beta)
    G = jnp.cumsum(g, axis=-1)
    incl = jnp.tril(jnp.ones((C, C), dtype=bool))
    strict = jnp.tril(jnp.ones((C, C), dtype=bool), -1)
    decay = jnp.exp(jnp.where(incl, G[..., :, None] - G[..., None, :], -jnp.inf))
    kk = jnp.einsum('bnhid,bnhjd->bnhij', k, k)
    amat = jnp.eye(C, dtype=jnp.float32) + jnp.where(strict, beta[..., :, None] * kk * decay, 0.0)
    U = lax.linalg.triangular_solve(amat, beta[..., None] * v, left_side=True, lower=True,
                                    unit_diagonal=True)
    Wk = lax.linalg.triangular_solve(amat, (beta * jnp.exp(G))[..., None] * k, left_side=True,
                                     lower=True, unit_diagonal=True)
    QK = jnp.einsum('bnhid,bnhjd->bnhij', q, k) * decay
    Qg = q * jnp.exp(G)[..., None]
    Kt = k * jnp.exp(G[..., -1:] - G)[..., None]
    gl = jnp.exp(G[..., -1])

    def step(S, xs):
        u_n, wk_n, qk_n, qg_n, kt_n, gl_n = xs
        W = u_n - jnp.einsum('bhcd,bhde->bhce', wk_n, S)
        O = jnp.einsum('bhcd,bhde->bhce', qg_n, S) + jnp.einsum('bhij,bhje->bhie', qk_n, W)
        S = gl_n[..., None, None] * S + jnp.einsum('bhcd,bhce->bhde', kt_n, W)
        return S, O

    xs = tuple(jnp.moveaxis(t, 1, 0) for t in (U, Wk, QK, Qg, Kt, gl))
    S, O = lax.scan(step, S0, xs)
    O = jnp.transpose(O, (1, 0, 3, 2, 4)).reshape(bn, N * C, H, dv)[:, :L]
    return O, S


def gated_deltanet(bp, conv_buf, S0, conv_w, a_log, dt_bias, onorm_g):
    bn, L, _ = bp.shape
    qkv, new_buf = causal_dwconv(bp[..., :3 * B_WIDTH], conv_buf, conv_w)
    qkv = jax.nn.silu(qkv)
    gate = bp[..., 3 * B_WIDTH:4 * B_WIDTH].reshape(bn, L, B_HEADS, B_DV).astype(jnp.float32)
    beta_logit = bp[..., 4 * B_WIDTH:4 * B_WIDTH + B_HEADS].astype(jnp.float32)
    a_logit = bp[..., 4 * B_WIDTH + B_HEADS:].astype(jnp.float32)
    q = l2norm(qkv[..., :B_WIDTH].reshape(bn, L, B_HEADS, B_DK)) * (B_DK ** -0.5)
    k = l2norm(qkv[..., B_WIDTH:2 * B_WIDTH].reshape(bn, L, B_HEADS, B_DK))
    v = qkv[..., 2 * B_WIDTH:].reshape(bn, L, B_HEADS, B_DV).astype(jnp.float32)
    beta = jax.nn.sigmoid(beta_logit)
    g = -jnp.exp(a_log.astype(jnp.float32)) * jax.nn.softplus(a_logit + dt_bias.astype(jnp.float32))
    o, S = gated_delta_chunked(q, k, v, g, beta, S0.astype(jnp.float32))
    o = rmsnorm(o, onorm_g) * jax.nn.silu(gate)
    return o.reshape(bn, L, B_WIDTH).astype(bp.dtype), new_buf, S


def rwkv7_scan(r, w, k, v, kk, b, S0):
    def step(S, xs):
        r_t, w_t, k_t, v_t, kk_t, b_t = xs
        sk = jnp.einsum('bhk,bhkv->bhv', kk_t, S)
        S = (w_t[..., :, None] * S - b_t[..., :, None] * sk[..., None, :]
             + k_t[..., :, None] * v_t[..., None, :])
        return S, jnp.einsum('bhk,bhkv->bhv', r_t, S)

    xs = tuple(jnp.moveaxis(t, 1, 0) for t in (r, w, k, v, kk, b))
    S, ys = lax.scan(step, S0, xs)
    return jnp.moveaxis(ys, 0, 1), S


def rwkv7_mix(cp, shift_prev, S0, mu, w0, w2, a0, a2, g2, k_k, k_a, r_k, ln_g, ln_b):
    bn, L, _ = cp.shape
    prev = jnp.concatenate([shift_prev[:, None].astype(cp.dtype), cp[:, :-1]], axis=1)
    xm = (cp + (prev - cp) * mu).astype(jnp.float32)
    o1, o2, o3 = C_WIDTH, 2 * C_WIDTH, 3 * C_WIDTH
    o4, o5 = o3 + C_DECAY_LORA, o3 + C_DECAY_LORA + C_AAA_LORA
    r, k, v = xm[..., :o1], xm[..., o1:o2], xm[..., o2:o3]
    wl, al, gl = xm[..., o3:o4], xm[..., o4:o5], xm[..., o5:]
    w_log = -jax.nn.softplus(-(w0 + jnp.tanh(wl) @ w2)) - 0.5
    decay = jnp.exp(-jnp.exp(w_log))
    a = jax.nn.sigmoid(a0 + al @ a2)
    g = jax.nn.sigmoid(gl) @ g2

    def hs(t):
        return t.reshape(bn, L, C_HEADS, C_HD)

    kk = l2norm(hs(k * k_k))
    k = k * (1.0 + (a - 1.0) * k_a)
    r_h, k_h, v_h, a_h = hs(r), hs(k), hs(v), hs(a)
    y, S = rwkv7_scan(r_h, hs(decay), k_h, v_h, kk, kk * a_h, S0.astype(jnp.float32))
    mean = jnp.mean(y, axis=-1, keepdims=True)
    var = jnp.mean(jnp.square(y - mean), axis=-1, keepdims=True)
    y = ((y - mean) * lax.rsqrt(var + GN_EPS)).reshape(bn, L, C_WIDTH) * ln_g + ln_b
    y = y + (jnp.sum(r_h * k_h * r_k, axis=-1, keepdims=True) * v_h).reshape(bn, L, C_WIDTH)
    return (y * g).astype(cp.dtype), cp[:, -1], S


def multiscale_pool(dp, buf, start, d_w, d_scale):
    bn, L, _ = dp.shape
    xp = jnp.concatenate([buf.astype(dp.dtype), dp], axis=1)
    cs = jnp.pad(jnp.cumsum(xp.astype(jnp.float32), axis=1), ((0, 0), (1, 0), (0, 0)))
    pos = start + jnp.arange(L)
    end = cs[:, POOL_BUF + 1:POOL_BUF + 1 + L]
    means = []
    for gi, w in enumerate(POOL_WINDOWS):
        sl = slice(gi * D_GC, (gi + 1) * D_GC)
        s = end[..., sl] - cs[:, POOL_BUF + 1 - w:POOL_BUF + 1 - w + L, sl]
        cnt = jnp.minimum(pos + 1, w).astype(jnp.float32)[None, :, None]
        means.append(s / cnt)
    mean = jnp.stack(means, axis=2)
    diff = mean - dp.reshape(bn, L, D_GROUPS, D_GC).astype(jnp.float32)
    out = jnp.einsum('blgc,gcd->blgd', diff, d_w.astype(jnp.float32)).reshape(bn, L, D_WIDTH)
    return (out * d_scale).astype(dp.dtype), xp[:, -POOL_BUF:]


def run_trunk(x, start, b_conv, b_ssm, c_shift, c_wkv, d_pool, p):
    n_av, n_bc, n_bs, n_cs, n_cw, n_dp = [], [], [], [], [], []
    for l in range(DEPTH):
        h = rmsnorm(x, p['norm1_g'][l])
        proj = h @ p['w_in'][l]
        a_out, a_v = chunk_gmlp(proj[..., :B_OFF], p['a_ws'][l], p['a_bs'][l], p['a_vnorm_g'][l])
        b_out, nb_conv, nb_ssm = gated_deltanet(proj[..., B_OFF:C_OFF], b_conv[l], b_ssm[l],
                                                p['b_conv_w'][l], p['b_a_log'][l],
                                                p['b_dt_bias'][l], p['b_onorm_g'][l])
        c_out, nc_shift, nc_wkv = rwkv7_mix(proj[..., C_OFF:D_OFF], c_shift[l], c_wkv[l],
                                            p['c_mu'][l], p['c_w0'][l], p['c_w2'][l],
                                            p['c_a0'][l], p['c_a2'][l], p['c_g2'][l],
                                            p['c_k_k'][l], p['c_k_a'][l], p['c_r_k'][l],
                                            p['c_ln_g'][l], p['c_ln_b'][l])
        d_out, nd_pool = multiscale_pool(proj[..., D_OFF:], d_pool[l], start,
                                         p['d_w'][l], p['d_scale'][l])
        mixed = jnp.concatenate([a_out, b_out, c_out, d_out], axis=-1).astype(x.dtype)
        x = x + mixed @ p['w_out'][l]
        hm = rmsnorm(x, p['norm2_g'][l])
        x = x + jnp.square(jax.nn.relu(hm @ p['w_up'][l])) @ p['w_down'][l]
        n_av.append(a_v)
        n_bc.append(nb_conv)
        n_bs.append(nb_ssm)
        n_cs.append(nc_shift)
        n_cw.append(nc_wkv)
        n_dp.append(nd_pool)
    y = rmsnorm(x, p['final_g'])
    return (y, jnp.stack(n_av), jnp.stack(n_bc), jnp.stack(n_bs), jnp.stack(n_cs),
            jnp.stack(n_cw), jnp.stack(n_dp))


def setup_inputs(seed: int = 0) -> dict:
    key = jax.random.key(seed)
    keys = iter(jax.random.split(key, 48))

    def nrm(shape, scale):
        return scale * jax.random.normal(next(keys), shape, jnp.float32)

    def unif(shape, lo, hi):
        return jax.random.uniform(next(keys), shape, jnp.float32, lo, hi)

    dt = jnp.exp(unif((DEPTH, B_HEADS), math.log(1e-3), math.log(1e-1)))
    return {
        'x_prompt': nrm((BATCH, SEQ, D_MODEL), 1.0),
        'x_sample': nrm((DEC_BATCH, DEC_SEQ, D_MODEL), 1.0),
        'state_b_conv': nrm((DEPTH, DEC_BATCH, B_CONV - 1, 3 * B_WIDTH), 1.0),
        'state_b_ssm': nrm((DEPTH, DEC_BATCH, B_HEADS, B_DK, B_DV), 0.1),
        'state_c_shift': nrm((DEPTH, DEC_BATCH, C_PROJ), 1.0),
        'state_c_wkv': nrm((DEPTH, DEC_BATCH, C_HEADS, C_HD, C_HD), 0.1),
        'state_d_pool': nrm((DEPTH, DEC_BATCH, POOL_BUF, D_WIDTH), 1.0),
        'norm1_g': 1.0 + nrm((DEPTH, D_MODEL), 0.02),
        'w_in': nrm((DEPTH, D_MODEL, IN_COLS), D_MODEL ** -0.5),
        'a_ws': nrm((DEPTH, A_HEADS, A_CHUNK, A_CHUNK), A_CHUNK ** -0.5),
        'a_bs': 1.0 + nrm((DEPTH, A_HEADS, A_CHUNK), 0.1),
        'a_vnorm_g': 1.0 + nrm((DEPTH, A_WIDTH), 0.02),
        'b_conv_w': nrm((DEPTH, B_CONV, 3 * B_WIDTH), B_CONV ** -0.5),
        'b_a_log': jnp.log(unif((DEPTH, B_HEADS), 1.0, 16.0)),
        'b_dt_bias': dt + jnp.log(-jnp.expm1(-dt)),
        'b_onorm_g': 1.0 + nrm((DEPTH, B_DV), 0.02),
        'c_mu': unif((DEPTH, C_PROJ), 0.0, 1.0),
        'c_w0': unif((DEPTH, C_WIDTH), -6.0, 1.0),
        'c_w2': nrm((DEPTH, C_DECAY_LORA, C_WIDTH), 0.1 * C_DECAY_LORA ** -0.5),
        'c_a0': nrm((DEPTH, C_WIDTH), 0.1),
        'c_a2': nrm((DEPTH, C_AAA_LORA, C_WIDTH), C_AAA_LORA ** -0.5),
        'c_g2': nrm((DEPTH, C_GATE_LORA, C_WIDTH), C_GATE_LORA ** -0.5),
        'c_k_k': 0.85 + nrm((DEPTH, C_WIDTH), 0.02),
        'c_k_a': 1.0 + nrm((DEPTH, C_WIDTH), 0.02),
        'c_r_k': nrm((DEPTH, C_HEADS, C_HD), 0.1),
        'c_ln_g': 1.0 + nrm((DEPTH, C_WIDTH), 0.02),
        'c_ln_b': nrm((DEPTH, C_WIDTH), 0.02),
        'd_w': nrm((DEPTH, D_GROUPS, D_GC, D_GC), D_GC ** -0.5),
        'd_scale': 1.0 + nrm((DEPTH, D_WIDTH), 0.1),
        'w_out': nrm((DEPTH, MIX_WIDTH, D_MODEL), MIX_WIDTH ** -0.5),
        'norm2_g': 1.0 + nrm((DEPTH, D_MODEL), 0.02),
        'w_up': nrm((DEPTH, D_MODEL, D_FF), D_MODEL ** -0.5),
        'w_down': nrm((DEPTH, D_FF, D_MODEL), D_FF ** -0.5),
        'final_g': 1.0 + nrm((D_MODEL,), 0.02),
    }


def reference(x_prompt, x_sample, state_b_conv, state_b_ssm, state_c_shift, state_c_wkv,
              state_d_pool, norm1_g, w_in, a_ws, a_bs, a_vnorm_g, b_conv_w, b_a_log, b_dt_bias,
              b_onorm_g, c_mu, c_w0, c_w2, c_a0, c_a2, c_g2, c_k_k, c_k_a, c_r_k, c_ln_g, c_ln_b,
              d_w, d_scale, w_out, norm2_g, w_up, w_down, final_g):
    p = dict(norm1_g=norm1_g, w_in=w_in, a_ws=a_ws, a_bs=a_bs, a_vnorm_g=a_vnorm_g,
             b_conv_w=b_conv_w, b_a_log=b_a_log, b_dt_bias=b_dt_bias, b_onorm_g=b_onorm_g,
             c_mu=c_mu, c_w0=c_w0, c_w2=c_w2, c_a0=c_a0, c_a2=c_a2, c_g2=c_g2, c_k_k=c_k_k,
             c_k_a=c_k_a, c_r_k=c_r_k, c_ln_g=c_ln_g, c_ln_b=c_ln_b, d_w=d_w, d_scale=d_scale,
             w_out=w_out, norm2_g=norm2_g, w_up=w_up, w_down=w_down, final_g=final_g)
    z_conv = jnp.zeros((DEPTH, BATCH, B_CONV - 1, 3 * B_WIDTH), x_prompt.dtype)
    z_ssm = jnp.zeros((DEPTH, BATCH, B_HEADS, B_DK, B_DV), jnp.float32)
    z_shift = jnp.zeros((DEPTH, BATCH, C_PROJ), x_prompt.dtype)
    z_wkv = jnp.zeros((DEPTH, BATCH, C_HEADS, C_HD, C_HD), jnp.float32)
    z_pool = jnp.zeros((DEPTH, BATCH, POOL_BUF, D_WIDTH), x_prompt.dtype)
    y_prompt, _, p_b_conv, p_b_ssm, p_c_shift, p_c_wkv, p_d_pool = run_trunk(
        x_prompt, 0, z_conv, z_ssm, z_shift, z_wkv, z_pool, p)
    y_sample, s_a_v, s_b_conv, s_b_ssm, s_c_shift, s_c_wkv, s_d_pool = run_trunk(
        x_sample, PAST_LEN, state_b_conv, state_b_ssm, state_c_shift, state_c_wkv,
        state_d_pool, p)
    return (y_prompt, y_sample, p_b_conv, p_b_ssm, p_c_shift, p_c_wkv, p_d_pool,
            s_a_v, s_b_conv, s_b_ssm, s_c_shift, s_c_wkv, s_d_pool)
```

```python
import functools

import jax
import jax.numpy as jnp
from jax import lax
from jax.experimental import pallas as pl
from jax.experimental.pallas import tpu as pltpu

F32 = jnp.float32
BF16 = jnp.bfloat16
HI = lax.Precision.HIGHEST

D_MODEL = 1024
DEPTH = 2
HEADS = 4
HEAD_DIM = 64
WIDTH = HEADS * HEAD_DIM
B_QKV = 3 * WIDTH
B_CONV = 4
C_PROJ = 4 * WIDTH
POOL_WINDOWS = (2, 4, 8, 16)
POOL_BUF = 15
D_FF = 4 * D_MODEL
PAST_LEN = 16384
NORM_EPS = 1e-6
L2_EPS = 1e-6
GN_EPS = 64e-5
GATE_LANES = 128
SUBLANES = 8
VMEM_LIMIT = 48 * 1024 * 1024


def _rms(x, g):
    return x * lax.rsqrt(jnp.mean(x * x, axis=-1, keepdims=True) + NORM_EPS) * g


def _softplus(x):
    return jnp.maximum(x, 0.0) + jnp.log1p(jnp.exp(-jnp.abs(x)))


def _dot(a, b, precision=None):
    return jnp.dot(a, b, precision=precision, preferred_element_type=F32)


def _dot_nt(a, b, precision=None):
    return lax.dot_general(a, b, (((1,), (1,)), ((), ())), precision=precision,
                           preferred_element_type=F32)


def _dot_tn(a, b, precision=None):
    return lax.dot_general(a, b, (((0,), (0,)), ((), ())), precision=precision,
                           preferred_element_type=F32)


def _resident(shape):
    return pl.BlockSpec(shape, lambda *_: (0,) * len(shape), pipeline_mode=pl.Buffered(1))


def _unit_lower_inverse(lm, row, col, n):
    def sub_diag_block(shift):
        return (((row >> (shift + 1)) == (col >> (shift + 1)))
                & (((row >> shift) & 1) == 1) & (((col >> shift) & 1) == 0))

    m = (row == col).astype(F32) - jnp.where(sub_diag_block(0), lm, 0.0)
    shift = 1
    while (1 << shift) < n:
        cs = jnp.where(sub_diag_block(shift), lm, 0.0)
        m = m - _dot(_dot(m, cs, HI), m, HI)
        shift += 1
    return m


def _inproj_kernel(x_ref, g_ref, wa_ref, wb_ref, wc_ref, wd_ref, wg_ref,
                   pa_ref, pb_ref, pc_ref, pd_ref, pg_ref):
    h = _rms(x_ref[...], g_ref[...]).astype(BF16)
    for w_ref, o_ref in ((wa_ref, pa_ref), (wb_ref, pb_ref), (wc_ref, pc_ref), (wd_ref, pd_ref),
                         (wg_ref, pg_ref)):
        o_ref[...] = _dot(h, w_ref[...])


def _inproj(x, g, ws, tm):
    t = x.shape[0]
    widths = [w.shape[1] for w in ws]
    const = lambda i: (0, 0)
    return pl.pallas_call(
        _inproj_kernel,
        grid=(t // tm,),
        in_specs=[pl.BlockSpec((tm, D_MODEL), lambda i: (i, 0)), pl.BlockSpec((1, D_MODEL), const)]
        + [_resident((D_MODEL, n)) for n in widths],
        out_specs=[pl.BlockSpec((tm, n), lambda i: (i, 0)) for n in widths],
        out_shape=[jax.ShapeDtypeStruct((t, n), F32) for n in widths],
        compiler_params=pltpu.CompilerParams(dimension_semantics=("parallel",),
                                             vmem_limit_bytes=VMEM_LIMIT),
        name="inproj",
    )(x, g, *ws)


def _gmlp_kernel(pa_ref, mm_ref, bias_ref, vg_ref, out_ref, v_ref):
    pa = pa_ref[...]
    u = jax.nn.gelu(pa[:, :WIDTH])
    v = _rms(jax.nn.gelu(pa[:, WIDTH:]), vg_ref[...])
    vb = v.astype(BF16)
    lane_head = lax.broadcasted_iota(jnp.int32, v.shape, 1) >> 6
    mix = bias_ref[...]
    for h in range(HEADS):
        mix = mix + jnp.where(lane_head == h, _dot(mm_ref[h], vb), 0.0)
    out_ref[...] = u * mix
    v_ref[...] = v


def _gmlp(pa, mm, bias, vg):
    t = pa.shape[0]
    r = mm.shape[1]
    return pl.pallas_call(
        _gmlp_kernel,
        grid=(t // r,),
        in_specs=[pl.BlockSpec((r, 2 * WIDTH), lambda i: (i, 0)),
                  pl.BlockSpec((HEADS, r, r), lambda i: (0, 0, 0)),
                  pl.BlockSpec((r, WIDTH), lambda i: (0, 0)),
                  pl.BlockSpec((1, WIDTH), lambda i: (0, 0))],
        out_specs=[pl.BlockSpec((r, WIDTH), lambda i: (i, 0))] * 2,
        out_shape=[jax.ShapeDtypeStruct((t, WIDTH), F32)] * 2,
        compiler_params=pltpu.CompilerParams(dimension_semantics=("parallel",)),
        name="gmlp",
    )(pa, mm, bias, vg)


def _gdn_kernel(pb_ref, pg_ref, s0_ref, cbuf_ref, cw_ref, alog_ref, dtb_ref, og_ref,
                o_ref, sout_ref, s_sc, xbuf, *, chunk, n_valid):
    c = pl.program_id(1)

    @pl.when(c == 0)
    def _():
        s_sc[...] = s0_ref[0]
        xbuf[0:SUBLANES, :] = cbuf_ref[0]

    raw = pb_ref[:, 0:B_QKV]
    xbuf[SUBLANES:SUBLANES + chunk, :] = raw
    cw = cw_ref[...]
    conv = xbuf[SUBLANES - 3:SUBLANES - 3 + chunk, :] * cw[0:1]
    conv = conv + xbuf[SUBLANES - 2:SUBLANES - 2 + chunk, :] * cw[1:2]
    conv = conv + xbuf[SUBLANES - 1:SUBLANES - 1 + chunk, :] * cw[2:3]
    conv = conv + raw * cw[3:4]
    xbuf[0:SUBLANES, :] = xbuf[chunk:chunk + SUBLANES, :]
    qkv = jax.nn.silu(conv)
    gate = pb_ref[:, B_QKV:B_QKV + WIDTH]

    row = lax.broadcasted_iota(jnp.int32, (chunk, chunk), 0)
    col = lax.broadcasted_iota(jnp.int32, (chunk, chunk), 1)
    tril = row >= col
    strict = row > col
    trilf = tril.astype(F32)
    strictf = strict.astype(F32)

    pg = pg_ref[...]
    beta_all = jax.nn.sigmoid(pg)
    g_all = -jnp.exp(alog_ref[...]) * _softplus(pg + dtb_ref[...])
    if n_valid < chunk:
        valid = lax.broadcasted_iota(jnp.int32, pg.shape, 0) < n_valid
        beta_all = jnp.where(valid, beta_all, 0.0)
        g_all = jnp.where(valid, g_all, 0.0)
    gc = _dot(trilf, g_all, HI)
    glast = gc[chunk - 1:chunk, :]
    e_g = jnp.exp(gc)
    e_gl = jnp.exp(glast - gc)
    e_last = jnp.exp(glast)
    og = og_ref[...]

    outs = []
    for h in range(HEADS):
        hs = slice(h * HEAD_DIM, (h + 1) * HEAD_DIM)
        qh = qkv[:, h * HEAD_DIM:(h + 1) * HEAD_DIM]
        kh = qkv[:, WIDTH + h * HEAD_DIM:WIDTH + (h + 1) * HEAD_DIM]
        vh = qkv[:, 2 * WIDTH + h * HEAD_DIM:2 * WIDTH + (h + 1) * HEAD_DIM]
        qh = qh * lax.rsqrt(jnp.sum(qh * qh, axis=-1, keepdims=True) + L2_EPS) * (HEAD_DIM ** -0.5)
        kh = kh * lax.rsqrt(jnp.sum(kh * kh, axis=-1, keepdims=True) + L2_EPS)
        bcol = beta_all[:, h:h + 1]
        gcol = g_all[:, HEADS + h:HEADS + h + 1]
        dm = _dot(trilf, gcol * strictf, HI)
        decay = jnp.where(tril, jnp.exp(dm), 0.0)
        kk = _dot_nt(kh, kh, HI)
        tinv = _unit_lower_inverse(jnp.where(strict, bcol * kk * decay, 0.0), row, col, chunk)
        egc = e_g[:, HEADS + h:HEADS + h + 1]
        rhs = jnp.concatenate([bcol * vh, (bcol * egc) * kh], axis=1)
        uw = _dot(tinv, rhs, HI)
        u = uw[:, :HEAD_DIM]
        wk = uw[:, HEAD_DIM:]
        qk = _dot_nt(qh, kh, HI) * decay
        qg = qh * egc
        kt = kh * e_gl[:, HEADS + h:HEADS + h + 1]
        s = s_sc[h]
        wn = u - _dot(wk, s, HI)
        o = _dot(qg, s, HI) + _dot(qk, wn, HI)
        s_sc[h] = e_last[:, HEADS + h:HEADS + h + 1] * s + _dot_tn(kt, wn, HI)
        outs.append(_rms(o, og) * jax.nn.silu(gate[:, hs]))
    o_ref[...] = jnp.concatenate(outs, axis=1)

    @pl.when(c == pl.num_programs(1) - 1)
    def _():
        sout_ref[0] = s_sc[...]


def _gdn(pb, pg, s0, cbuf, cw, alog, dtb, og, *, nseq, nchunk, chunk, n_valid):
    t = pb.shape[0]
    rows = lambda b, c: (b * nchunk + c, 0)
    const = lambda b, c: (0, 0)
    return pl.pallas_call(
        functools.partial(_gdn_kernel, chunk=chunk, n_valid=n_valid),
        grid=(nseq, nchunk),
        in_specs=[pl.BlockSpec((chunk, 4 * WIDTH), rows),
                  pl.BlockSpec((chunk, GATE_LANES), rows),
                  pl.BlockSpec((1, HEADS, HEAD_DIM, HEAD_DIM), lambda b, c: (b, 0, 0, 0)),
                  pl.BlockSpec((1, SUBLANES, B_QKV), lambda b, c: (b, 0, 0)),
                  pl.BlockSpec((B_CONV, B_QKV), const),
                  pl.BlockSpec((1, GATE_LANES), const),
                  pl.BlockSpec((1, GATE_LANES), const),
                  pl.BlockSpec((1, HEAD_DIM), const)],
        out_specs=[pl.BlockSpec((chunk, WIDTH), rows),
                   pl.BlockSpec((1, HEADS, HEAD_DIM, HEAD_DIM), lambda b, c: (b, 0, 0, 0))],
        out_shape=[jax.ShapeDtypeStruct((t, WIDTH), F32),
                   jax.ShapeDtypeStruct((nseq, HEADS, HEAD_DIM, HEAD_DIM), F32)],
        scratch_shapes=[pltpu.VMEM((HEADS, HEAD_DIM, HEAD_DIM), F32),
                        pltpu.VMEM((chunk + SUBLANES, B_QKV), F32)],
        compiler_params=pltpu.CompilerParams(dimension_semantics=("parallel", "arbitrary")),
        name="gdn",
    )(pb, pg, s0, cbuf, cw, alog, dtb, og)


def _rwkv_kernel(pc_ref, s0_ref, sh_ref, mu_ref, w0_ref, w2_ref, a0_ref, a2_ref, g2_ref,
                 kk_ref, ka_ref, rk_ref, lng_ref, lnb_ref,
                 o_ref, sout_ref, st_sc, xs, *, chunk, n_valid):
    c = pl.program_id(1)

    @pl.when(c == 0)
    def _():
        for h in range(HEADS):
            st_sc[h] = s0_ref[0, h].T
        xs[0:SUBLANES, :] = sh_ref[0]

    cp = pc_ref[...]
    xs[SUBLANES:SUBLANES + chunk, :] = cp
    prev = xs[SUBLANES - 1:SUBLANES - 1 + chunk, :]
    xs[0:SUBLANES, :] = xs[chunk:chunk + SUBLANES, :]
    xm = cp + (prev - cp) * mu_ref[...]
    r = xm[:, 0:WIDTH]
    k = xm[:, WIDTH:2 * WIDTH]
    v = xm[:, 2 * WIDTH:3 * WIDTH]
    wl = xm[:, 3 * WIDTH:3 * WIDTH + 64]
    al = xm[:, 3 * WIDTH + 64:3 * WIDTH + 128]
    gl = xm[:, 3 * WIDTH + 128:4 * WIDTH]
    w_log = -_softplus(-(w0_ref[...] + _dot(jnp.tanh(wl), w2_ref[...], HI))) - 0.5
    logw = -jnp.exp(w_log)
    a = jax.nn.sigmoid(a0_ref[...] + _dot(al, a2_ref[...], HI))
    gg = _dot(jax.nn.sigmoid(gl), g2_ref[...], HI)
    kkraw = k * kk_ref[...]
    k2 = k * (1.0 + (a - 1.0) * ka_ref[...])
    if n_valid < chunk:
        valid = lax.broadcasted_iota(jnp.int32, (chunk, WIDTH), 0) < n_valid
        logw = jnp.where(valid, logw, 0.0)
        kkraw = jnp.where(valid, kkraw, 0.0)
        k2m = jnp.where(valid, k2, 0.0)
    else:
        k2m = k2

    row = lax.broadcasted_iota(jnp.int32, (chunk, chunk), 0)
    col = lax.broadcasted_iota(jnp.int32, (chunk, chunk), 1)
    tril = row >= col
    strict = row > col
    glog = _dot(tril.astype(F32), logw, HI)
    glast = glog[chunk - 1:chunk, :]
    e_p = jnp.exp(glog)
    e_pm = jnp.exp(glog - logw)
    e_n = jnp.exp(-glog)
    e_l = jnp.exp(glast - glog)
    e_last = jnp.exp(glast)
    rk = rk_ref[...]
    lng = lng_ref[...]
    lnb = lnb_ref[...]

    outs = []
    for h in range(HEADS):
        hs = slice(h * HEAD_DIM, (h + 1) * HEAD_DIM)
        kkh = kkraw[:, hs]
        kkh = kkh * lax.rsqrt(jnp.sum(kkh * kkh, axis=-1, keepdims=True) + L2_EPS)
        bh = kkh * a[:, hs]
        rh = r[:, hs]
        vh = v[:, hs]
        rhat = rh * e_p[:, hs]
        kkhat = kkh * e_pm[:, hs]
        kt = k2m[:, hs] * e_n[:, hs]
        bt = bh * e_n[:, hs]
        ktl = k2m[:, hs] * e_l[:, hs]
        btl = bh * e_l[:, hs]
        st = st_sc[h]
        lb = jnp.where(strict, _dot_nt(kkhat, bt, HI), 0.0)
        lk = jnp.where(strict, _dot_nt(kkhat, kt, HI), 0.0)
        tinv = _unit_lower_inverse(lb, row, col, chunk)
        u = _dot(tinv, _dot_nt(kkhat, st, HI) + _dot(lk, vh, HI), HI)
        ark = jnp.where(tril, _dot_nt(rhat, kt, HI), 0.0)
        arb = jnp.where(tril, _dot_nt(rhat, bt, HI), 0.0)
        y = _dot_nt(rhat, st, HI) + _dot(ark, vh, HI) - _dot(arb, u, HI)
        st_sc[h] = e_last[:, hs] * st + _dot_tn(vh, ktl, HI) - _dot_tn(u, btl, HI)
        mean = jnp.mean(y, axis=-1, keepdims=True)
        var = jnp.mean(jnp.square(y - mean), axis=-1, keepdims=True)
        yn = (y - mean) * lax.rsqrt(var + GN_EPS) * lng[:, hs] + lnb[:, hs]
        bonus = jnp.sum(rh * k2[:, hs] * rk[:, hs], axis=-1, keepdims=True) * vh
        outs.append((yn + bonus) * gg[:, hs])
    o_ref[...] = jnp.concatenate(outs, axis=1)

    @pl.when(c == pl.num_programs(1) - 1)
    def _():
        for h in range(HEADS):
            sout_ref[0, h] = st_sc[h].T


def _rwkv(pc, s0, sh, params, *, nseq, nchunk, chunk, n_valid):
    t = pc.shape[0]
    rows = lambda b, c: (b * nchunk + c, 0)
    const = lambda b, c: (0, 0)
    pspecs = [pl.BlockSpec(p.shape, const) for p in params]
    return pl.pallas_call(
        functools.partial(_rwkv_kernel, chunk=chunk, n_valid=n_valid),
        grid=(nseq, nchunk),
        in_specs=[pl.BlockSpec((chunk, C_PROJ), rows),
                  pl.BlockSpec((1, HEADS, HEAD_DIM, HEAD_DIM), lambda b, c: (b, 0, 0, 0)),
                  pl.BlockSpec((1, SUBLANES, C_PROJ), lambda b, c: (b, 0, 0))] + pspecs,
        out_specs=[pl.BlockSpec((chunk, WIDTH), rows),
                   pl.BlockSpec((1, HEADS, HEAD_DIM, HEAD_DIM), lambda b, c: (b, 0, 0, 0))],
        out_shape=[jax.ShapeDtypeStruct((t, WIDTH), F32),
                   jax.ShapeDtypeStruct((nseq, HEADS, HEAD_DIM, HEAD_DIM), F32)],
        scratch_shapes=[pltpu.VMEM((HEADS, HEAD_DIM, HEAD_DIM), F32),
                        pltpu.VMEM((chunk + SUBLANES, C_PROJ), F32)],
        compiler_params=pltpu.CompilerParams(dimension_semantics=("parallel", "arbitrary")),
        name="rwkv",
    )(pc, s0, sh, *params)


def _round_up(x, m):
    return (x + m - 1) // m * m


def _pool_offsets(stride):
    offs = []
    a = 0
    for k in range(len(POOL_WINDOWS)):
        a = _round_up(a + (1 << k) * stride, SUBLANES)
        offs.append(a)
    return offs


def _pool_kernel(hist_ref, dp_ref, wbd_ref, scale_ref, out_ref, s1, s2, *, rows, stride, start):
    offs = _pool_offsets(stride)
    d0 = offs[-1]
    n = d0 + rows
    dp = dp_ref[...]
    s1[0:d0, :] = hist_ref[0]
    s1[d0:n, :] = dp
    lane = lax.broadcasted_iota(jnp.int32, (1, WIDTH), 1)
    src, dst = s1, s2
    for k, a in enumerate(offs):
        sh = (1 << k) * stride
        shifted = jnp.where(lane >= k * HEAD_DIM, src[a - sh:n - sh, :], 0.0)
        if k < len(offs) - 1:
            dst[a:n, :] = src[a:n, :] + shifted
            src, dst = dst, src
        else:
            sums = src[a:n, :] + shifted
    assert stride & (stride - 1) == 0
    pos = start + (lax.broadcasted_iota(jnp.int32, (rows, WIDTH), 0) >> (stride.bit_length() - 1))
    window = jnp.left_shift(2, lax.broadcasted_iota(jnp.int32, (rows, WIDTH), 1) >> 6)
    cnt = jnp.minimum(pos + 1, window).astype(F32)
    diff = sums / cnt - dp
    out_ref[...] = _dot(diff.astype(BF16), wbd_ref[...]) * scale_ref[...]


def _pool(hist, dp, wbd, scale, *, nseq, rows, stride, start):
    d0 = _pool_offsets(stride)[-1]
    return pl.pallas_call(
        functools.partial(_pool_kernel, rows=rows, stride=stride, start=start),
        grid=(nseq,),
        in_specs=[pl.BlockSpec((1, d0, WIDTH), lambda b: (b, 0, 0)),
                  pl.BlockSpec((rows, WIDTH), lambda b: (b, 0)),
                  pl.BlockSpec((WIDTH, WIDTH), lambda b: (0, 0)),
                  pl.BlockSpec((1, WIDTH), lambda b: (0, 0))],
        out_specs=pl.BlockSpec((rows, WIDTH), lambda b: (b, 0)),
        out_shape=jax.ShapeDtypeStruct((nseq * rows, WIDTH), F32),
        scratch_shapes=[pltpu.VMEM((d0 + rows, WIDTH), F32)] * 2,
        compiler_params=pltpu.CompilerParams(dimension_semantics=("parallel",)),
        name="pool",
    )(hist, dp, wbd, scale)


def _ffn_kernel(x_ref, ma_ref, mb_ref, mc_ref, md_ref, wo_ref, g2_ref, wu_ref, wd_ref, gf_ref,
                o_ref, *, tf, final):
    mixed = None
    for i, m_ref in enumerate((ma_ref, mb_ref, mc_ref, md_ref)):
        part = _dot(m_ref[...].astype(BF16), wo_ref[i * WIDTH:(i + 1) * WIDTH, :])
        mixed = part if mixed is None else mixed + part
    x = x_ref[...] + mixed
    hm = _rms(x, g2_ref[...]).astype(BF16)
    down = None
    for j in range(D_FF // tf):
        up = jnp.maximum(_dot(hm, wu_ref[:, j * tf:(j + 1) * tf]), 0.0)
        part = _dot((up * up).astype(BF16), wd_ref[j * tf:(j + 1) * tf, :])
        down = part if down is None else down + part
    x = x + down
    if final:
        x = _rms(x, gf_ref[...])
    o_ref[...] = x


def _ffn(x, mixed, wo, g2, wu, wd, gf, *, tm, final):
    t = x.shape[0]
    const = lambda i: (0, 0)
    row = lambda i: (i, 0)
    return pl.pallas_call(
        functools.partial(_ffn_kernel, tf=1024, final=final),
        grid=(t // tm,),
        in_specs=[pl.BlockSpec((tm, D_MODEL), row)] + [pl.BlockSpec((tm, WIDTH), row)] * 4
        + [_resident((D_MODEL, D_MODEL)), pl.BlockSpec((1, D_MODEL), const),
           _resident((D_MODEL, D_FF)), _resident((D_FF, D_MODEL)),
           pl.BlockSpec((1, D_MODEL), const)],
        out_specs=pl.BlockSpec((tm, D_MODEL), row),
        out_shape=jax.ShapeDtypeStruct((t, D_MODEL), F32),
        compiler_params=pltpu.CompilerParams(dimension_semantics=("parallel",),
                                             vmem_limit_bytes=VMEM_LIMIT),
        name="ffn",
    )(x, *mixed, wo, g2, wu, wd, gf)


def _layer_weights(l, w_in, a_ws, a_bs, d_w):
    wl = w_in[l]
    a_end = 2 * WIDTH
    b_end = a_end + 4 * WIDTH
    c_off = b_end + 2 * HEADS
    d_off = c_off + C_PROJ
    wg = jnp.pad(wl[:, b_end:c_off], ((0, 0), (0, GATE_LANES - 2 * HEADS)))
    ws = [wl[:, :a_end], wl[:, a_end:b_end], wl[:, c_off:d_off], wl[:, d_off:], wg]
    ws = [w.astype(BF16) for w in ws]
    chunk = a_ws.shape[-1]
    causal = jnp.tril(jnp.ones((chunk, chunk), dtype=bool))
    wm = jnp.where(causal, a_ws[l], 0.0)
    gc = WIDTH // len(POOL_WINDOWS)
    wbd = jnp.zeros((WIDTH, WIDTH), F32)
    for g in range(len(POOL_WINDOWS)):
        wbd = wbd.at[g * gc:(g + 1) * gc, g * gc:(g + 1) * gc].set(d_w[l, g])
    return ws, wm, wbd.astype(BF16)


def _pad_lanes(v, n):
    return jnp.pad(v, (0, n - v.shape[0]))[None, :]


def kernel(x_prompt, x_sample, state_b_conv, state_b_ssm, state_c_shift, state_c_wkv, state_d_pool, norm1_g, w_in, a_ws, a_bs, a_vnorm_g, b_conv_w, b_a_log, b_dt_bias, b_onorm_g, c_mu, c_w0, c_w2, c_a0, c_a2, c_g2, c_k_k, c_k_a, c_r_k, c_ln_g, c_ln_b, d_w, d_scale, w_out, norm2_g, w_up, w_down, final_g):
    nb, seq, _ = x_prompt.shape
    ns, dseq, _ = x_sample.shape
    a_chunk = a_ws.shape[-1]
    b_chunk = 64
    s_pad = SUBLANES

    xp = x_prompt.reshape(nb * seq, D_MODEL)
    xs = jnp.transpose(x_sample, (1, 0, 2)).reshape(dseq * ns, D_MODEL)

    def to_seq_major_padded(a):
        a = jnp.transpose(a.reshape(dseq, ns, a.shape[-1]), (1, 0, 2))
        return jnp.pad(a, ((0, 0), (0, s_pad - dseq), (0, 0))).reshape(ns * s_pad, a.shape[-1])

    def to_time_major(a):
        a = a.reshape(ns, s_pad, a.shape[-1])[:, :dseq]
        return jnp.transpose(a, (1, 0, 2)).reshape(dseq * ns, a.shape[-1])

    outs = {k: [] for k in ("p_bc", "p_bs", "p_cs", "p_cw", "p_dp",
                            "s_av", "s_bc", "s_bs", "s_cs", "s_cw", "s_dp")}
    for l in range(DEPTH):
        ws, wm, wbd = _layer_weights(l, w_in, a_ws, a_bs, d_w)
        g1 = norm1_g[l][None, :]
        vg = a_vnorm_g[l][None, :]
        alog = _pad_lanes(jnp.concatenate([jnp.zeros((HEADS,), F32), b_a_log[l]]), GATE_LANES)
        dtb = _pad_lanes(jnp.concatenate([jnp.zeros((HEADS,), F32), b_dt_bias[l]]), GATE_LANES)
        og = b_onorm_g[l][None, :]
        cparams = [c_mu[l][None, :], c_w0[l][None, :], c_w2[l], c_a0[l][None, :], c_a2[l], c_g2[l],
                   c_k_k[l][None, :], c_k_a[l][None, :], c_r_k[l].reshape(1, WIDTH),
                   c_ln_g[l][None, :], c_ln_b[l][None, :]]
        dscale = d_scale[l][None, :]
        wo = w_out[l].astype(BF16)
        g2 = norm2_g[l][None, :]
        wu = w_up[l].astype(BF16)
        wd = w_down[l].astype(BF16)
        gf = final_g[None, :]
        final = l == DEPTH - 1

        pa, pb, pc, pd, pg = _inproj(xp, g1, ws, 512)
        bias_p = jnp.repeat(a_bs[l].T, HEAD_DIM, axis=1)
        a_out, _ = _gmlp(pa, wm.astype(BF16), bias_p, vg)
        b_out, p_bs = _gdn(pb, pg, jnp.zeros((nb, HEADS, HEAD_DIM, HEAD_DIM), F32),
                           jnp.zeros((nb, SUBLANES, B_QKV), F32), b_conv_w[l], alog, dtb, og,
                           nseq=nb, nchunk=seq // b_chunk, chunk=b_chunk, n_valid=b_chunk)
        c_out, p_cw = _rwkv(pc, jnp.zeros((nb, HEADS, HEAD_DIM, HEAD_DIM), F32),
                            jnp.zeros((nb, SUBLANES, C_PROJ), F32), cparams,
                            nseq=nb, nchunk=seq // b_chunk, chunk=b_chunk, n_valid=b_chunk)
        d0 = _pool_offsets(1)[-1]
        d_out = _pool(jnp.zeros((nb, d0, WIDTH), F32), pd, wbd, dscale,
                      nseq=nb, rows=seq, stride=1, start=0)
        xp = _ffn(xp, (a_out, b_out, c_out, d_out), wo, g2, wu, wd, gf, tm=512, final=final)
        outs["p_bc"].append(pb.reshape(nb, seq, 4 * WIDTH)[:, seq - (B_CONV - 1):, :B_QKV])
        outs["p_bs"].append(p_bs)
        outs["p_cs"].append(pc.reshape(nb, seq, C_PROJ)[:, seq - 1])
        outs["p_cw"].append(p_cw)
        outs["p_dp"].append(pd.reshape(nb, seq, WIDTH)[:, seq - POOL_BUF:])

        pa, pb, pc, pd, pg = _inproj(xs, g1, ws, 256)
        rows_s = dseq * ns
        eye = jnp.eye(ns, dtype=F32)
        mm_s = jnp.stack([jnp.kron(wm[h, :dseq, :dseq], eye) for h in range(HEADS)]).astype(BF16)
        bias_s = jnp.repeat(jnp.repeat(a_bs[l].T[:dseq], HEAD_DIM, axis=1), ns, axis=0)
        a_out, a_v = _gmlp(pa, mm_s, bias_s, vg)
        cbuf = jnp.pad(state_b_conv[l], ((0, 0), (SUBLANES - (B_CONV - 1), 0), (0, 0)))
        b_out, s_bs = _gdn(to_seq_major_padded(pb), to_seq_major_padded(pg), state_b_ssm[l], cbuf,
                           b_conv_w[l], alog, dtb, og, nseq=ns, nchunk=1, chunk=s_pad, n_valid=dseq)
        b_out = to_time_major(b_out)
        sh = jnp.pad(state_c_shift[l][:, None, :], ((0, 0), (SUBLANES - 1, 0), (0, 0)))
        c_out, s_cw = _rwkv(to_seq_major_padded(pc), state_c_wkv[l], sh, cparams,
                            nseq=ns, nchunk=1, chunk=s_pad, n_valid=dseq)
        c_out = to_time_major(c_out)
        hist = jnp.transpose(state_d_pool[l], (1, 0, 2)).reshape(1, POOL_BUF * ns, WIDTH)
        d_out = _pool(hist, pd, wbd, dscale, nseq=1, rows=rows_s, stride=ns, start=PAST_LEN)
        xs = _ffn(xs, (a_out, b_out, c_out, d_out), wo, g2, wu, wd, gf, tm=256, final=final)

        def seq_major(a):
            return jnp.transpose(a.reshape(dseq, ns, a.shape[-1]), (1, 0, 2))

        outs["s_av"].append(seq_major(a_v))
        outs["s_bc"].append(jnp.concatenate([state_b_conv[l], seq_major(pb)[:, :, :B_QKV]],
                                            axis=1)[:, -(B_CONV - 1):])
        outs["s_bs"].append(s_bs)
        outs["s_cs"].append(seq_major(pc)[:, dseq - 1])
        outs["s_cw"].append(s_cw)
        outs["s_dp"].append(jnp.concatenate([state_d_pool[l], seq_major(pd)], axis=1)[:, -POOL_BUF:])

    y_prompt = xp.reshape(nb, seq, D_MODEL)
    y_sample = jnp.transpose(xs.reshape(dseq, ns, D_MODEL), (1, 0, 2))
    st = {k: jnp.stack(v) for k, v in outs.items()}
    return (y_prompt, y_sample, st["p_bc"], st["p_bs"], st["p_cs"], st["p_cw"], st["p_dp"],
            st["s_av"], st["s_bc"], st["s_bs"], st["s_cs"], st["s_cw"], st["s_dp"])
```

```python
import functools

import jax
import jax.numpy as jnp
from jax import lax
from jax.experimental import pallas as pl
from jax.experimental.pallas import tpu as pltpu

F32 = jnp.float32
BF16 = jnp.bfloat16
HI = lax.Precision.HIGHEST

D_MODEL = 1024
DEPTH = 2
HEADS = 4
HEAD_DIM = 64
WIDTH = HEADS * HEAD_DIM
B_QKV = 3 * WIDTH
B_CONV = 4
C_PROJ = 4 * WIDTH
POOL_WINDOWS = (2, 4, 8, 16)
POOL_BUF = 15
D_FF = 4 * D_MODEL
PAST_LEN = 16384
NORM_EPS = 1e-6
L2_EPS = 1e-6
GN_EPS = 64e-5
GATE_LANES = 128
SUBLANES = 8
VMEM_LIMIT = 48 * 1024 * 1024
PROMPT_SEQS_PER_STEP = 4
SAMPLE_SEQS_PER_STEP = 8


def _rms(x, g):
    return x * lax.rsqrt(jnp.mean(x * x, axis=-1, keepdims=True) + NORM_EPS) * g


def _softplus(x):
    return jnp.maximum(x, 0.0) + jnp.log1p(jnp.exp(-jnp.abs(x)))


def _dot(a, b, precision=None):
    return jnp.dot(a, b, precision=precision, preferred_element_type=F32)


def _dot_nt(a, b, precision=None):
    return lax.dot_general(a, b, (((1,), (1,)), ((), ())), precision=precision,
                           preferred_element_type=F32)


def _dot_tn(a, b, precision=None):
    return lax.dot_general(a, b, (((0,), (0,)), ((), ())), precision=precision,
                           preferred_element_type=F32)


def _resident(shape):
    return pl.BlockSpec(shape, lambda *_: (0,) * len(shape), pipeline_mode=pl.Buffered(1))


def _split(a):
    hi = a.astype(BF16)
    return hi, (a - hi.astype(F32)).astype(BF16)


def _bmm(a, b):
    return jnp.einsum('bij,bjk->bik', a, b, preferred_element_type=F32)


def _bmm_nt(a, b):
    return jnp.einsum('bik,bjk->bij', a, b, preferred_element_type=F32)


def _bmm_tn(a, b):
    return jnp.einsum('bki,bkj->bij', a, b, preferred_element_type=F32)


def _bmm_x3(ah, al, bh, bl):
    return _bmm(ah, bh) + (_bmm(ah, bl) + _bmm(al, bh))


def _split_heads(x, lane0):
    return jnp.stack([x[b, :, lane0 + h * HEAD_DIM:lane0 + (h + 1) * HEAD_DIM]
                      for b in range(x.shape[0]) for h in range(HEADS)])


def _unit_lower_inverse(lm, row, col, n_valid):
    def sub_diag_block(shift):
        return (((row >> (shift + 1)) == (col >> (shift + 1)))
                & (((row >> shift) & 1) == 1) & (((col >> shift) & 1) == 0))

    m = (row == col).astype(F32) - jnp.where(sub_diag_block(0), lm, 0.0)
    shift = 1
    while (1 << shift) < n_valid:
        ch, cl = _split(jnp.where(sub_diag_block(shift), lm, 0.0))
        mh, ml = _split(m)
        ph, pl_ = _split(_bmm_x3(mh, ml, ch, cl))
        m = m - _bmm_x3(ph, pl_, mh, ml)
        shift += 1
    return m


def _transpose_rows(a):
    n = a.shape[0]
    if n < GATE_LANES:
        a = jnp.concatenate([a, jnp.zeros((GATE_LANES - n, a.shape[1]), a.dtype)], axis=0)
    return a.T[:, :n]


def _inproj_kernel(x_ref, g_ref, wa_ref, wb_ref, wc_ref, wd_ref, wg_ref,
                   pa_ref, pb_ref, pc_ref, pd_ref, pg_ref):
    h = _rms(x_ref[...], g_ref[...]).astype(BF16)
    for w_ref, o_ref in ((wa_ref, pa_ref), (wb_ref, pb_ref), (wc_ref, pc_ref), (wd_ref, pd_ref),
                         (wg_ref, pg_ref)):
        o_ref[...] = _dot(h, w_ref[...])


def _inproj(x, g, ws, tm):
    t = x.shape[0]
    widths = [w.shape[1] for w in ws]
    const = lambda i: (0, 0)
    return pl.pallas_call(
        _inproj_kernel,
        grid=(t // tm,),
        in_specs=[pl.BlockSpec((tm, D_MODEL), lambda i: (i, 0)), pl.BlockSpec((1, D_MODEL), const)]
        + [_resident((D_MODEL, n)) for n in widths],
        out_specs=[pl.BlockSpec((tm, n), lambda i: (i, 0)) for n in widths],
        out_shape=[jax.ShapeDtypeStruct((t, n), F32) for n in widths],
        compiler_params=pltpu.CompilerParams(dimension_semantics=("parallel",),
                                             vmem_limit_bytes=VMEM_LIMIT),
        name="inproj",
    )(x, g, *ws)


def _gmlp_kernel(pa_ref, mm_ref, bias_ref, vg_ref, out_ref, v_ref):
    pa = pa_ref[...]
    u = jax.nn.gelu(pa[:, :WIDTH])
    v = _rms(jax.nn.gelu(pa[:, WIDTH:]), vg_ref[...])
    vb = v.astype(BF16)
    lane_head = lax.broadcasted_iota(jnp.int32, v.shape, 1) >> 6
    mix = bias_ref[...]
    for h in range(HEADS):
        mix = mix + jnp.where(lane_head == h, _dot(mm_ref[h], vb), 0.0)
    out_ref[...] = u * mix
    v_ref[...] = v


def _gmlp(pa, mm, bias, vg):
    t = pa.shape[0]
    r = mm.shape[1]
    return pl.pallas_call(
        _gmlp_kernel,
        grid=(t // r,),
        in_specs=[pl.BlockSpec((r, 2 * WIDTH), lambda i: (i, 0)),
                  pl.BlockSpec((HEADS, r, r), lambda i: (0, 0, 0)),
                  pl.BlockSpec((r, WIDTH), lambda i: (0, 0)),
                  pl.BlockSpec((1, WIDTH), lambda i: (0, 0))],
        out_specs=[pl.BlockSpec((r, WIDTH), lambda i: (i, 0))] * 2,
        out_shape=[jax.ShapeDtypeStruct((t, WIDTH), F32)] * 2,
        compiler_params=pltpu.CompilerParams(dimension_semantics=("parallel",)),
        name="gmlp",
    )(pa, mm, bias, vg)


def _gdn_kernel(pb_ref, pg_ref, s0_ref, cbuf_ref, cw_ref, alog_ref, dtb_ref, og_ref,
                o_ref, sout_ref, s_sc, xbuf, *, nbs, chunk, n_valid):
    c = pl.program_id(1)
    nb = nbs * HEADS

    @pl.when(c == 0)
    def _():
        s_sc[...] = s0_ref[...].reshape(nb, HEAD_DIM, HEAD_DIM)
        xbuf[:, 0:SUBLANES, :] = cbuf_ref[...]

    raw = pb_ref[:, :, 0:B_QKV]
    xbuf[:, SUBLANES:SUBLANES + chunk, :] = raw
    cw = cw_ref[...]
    conv = xbuf[:, SUBLANES - 3:SUBLANES - 3 + chunk, :] * cw[0:1]
    conv = conv + xbuf[:, SUBLANES - 2:SUBLANES - 2 + chunk, :] * cw[1:2]
    conv = conv + xbuf[:, SUBLANES - 1:SUBLANES - 1 + chunk, :] * cw[2:3]
    conv = conv + raw * cw[3:4]
    xbuf[:, 0:SUBLANES, :] = xbuf[:, chunk:chunk + SUBLANES, :]
    qkv = jax.nn.silu(conv)
    gate = jax.nn.silu(pb_ref[:, :, B_QKV:B_QKV + WIDTH])

    row = lax.broadcasted_iota(jnp.int32, (chunk, chunk), 0)
    col = lax.broadcasted_iota(jnp.int32, (chunk, chunk), 1)
    tril = row >= col
    strict = row > col
    trilf = tril.astype(F32)

    pg = pg_ref[...]
    beta_all = jax.nn.sigmoid(pg)
    g_all = -jnp.exp(alog_ref[...]) * _softplus(pg + dtb_ref[...])
    if n_valid < chunk:
        valid = lax.broadcasted_iota(jnp.int32, pg.shape, 1) < n_valid
        beta_all = jnp.where(valid, beta_all, 0.0)
        g_all = jnp.where(valid, g_all, 0.0)

    q = _split_heads(qkv, 0)
    k = _split_heads(qkv, WIDTH)
    v = _split_heads(qkv, 2 * WIDTH)
    q = q * lax.rsqrt(jnp.sum(q * q, axis=-1, keepdims=True) + L2_EPS) * (HEAD_DIM ** -0.5)
    k = k * lax.rsqrt(jnp.sum(k * k, axis=-1, keepdims=True) + L2_EPS)
    gc = [_dot(trilf, g_all[b], HI) for b in range(nbs)]
    gct = [_transpose_rows(g) for g in gc]
    pairs = [(b, HEADS + h) for b in range(nbs) for h in range(HEADS)]
    gcol = jnp.stack([gc[b][:, j:j + 1] for b, j in pairs])
    grow = jnp.stack([gct[b][j:j + 1, :] for b, j in pairs])
    bcol = jnp.stack([beta_all[b, :, h:h + 1] for b in range(nbs) for h in range(HEADS)])
    glast = gcol[:, chunk - 1:chunk, :]
    egc = jnp.exp(gcol)
    decay = jnp.where(tril, jnp.exp(jnp.minimum(gcol - grow, 0.0)), 0.0)
    kb = k.astype(BF16)
    qkk = _bmm_nt(jnp.concatenate([q, k], axis=1).astype(BF16), kb)
    qk = qkk[:, :chunk] * decay
    lm = jnp.where(strict, bcol * qkk[:, chunk:] * decay, 0.0)
    th, tl = _split(_unit_lower_inverse(lm, row, col, n_valid))
    rh, rl = _split(jnp.concatenate([bcol * v, (bcol * egc) * k], axis=2))
    uw = _bmm_x3(th, tl, rh, rl)
    u = uw[:, :, :HEAD_DIM]
    wk = uw[:, :, HEAD_DIM:]
    s = s_sc[...]
    ws = _bmm(jnp.concatenate([wk, q * egc], axis=1).astype(BF16), s.astype(BF16))
    wnb = (u - ws[:, :chunk]).astype(BF16)
    o = ws[:, chunk:] + _bmm(qk.astype(BF16), wnb)
    kt = k * jnp.exp(glast - gcol)
    s_sc[...] = jnp.exp(glast) * s + _bmm_tn(kt.astype(BF16), wnb)
    o = _rms(o, og_ref[...])
    for b in range(nbs):
        o_ref[b] = jnp.concatenate([o[b * HEADS + h] for h in range(HEADS)], axis=1) * gate[b]

    @pl.when(c == pl.num_programs(1) - 1)
    def _():
        sout_ref[...] = s_sc[...].reshape(nbs, HEADS, HEAD_DIM, HEAD_DIM)


def _gdn(pb, pg, s0, cbuf, cw, alog, dtb, og, *, nbs, chunk, n_valid):
    nseq, length, _ = pb.shape
    blk = lambda b, c: (b, c, 0)
    per_seq4 = lambda b, c: (b, 0, 0, 0)
    const = lambda b, c: (0, 0)
    return pl.pallas_call(
        functools.partial(_gdn_kernel, nbs=nbs, chunk=chunk, n_valid=n_valid),
        grid=(nseq // nbs, length // chunk),
        in_specs=[pl.BlockSpec((nbs, chunk, 4 * WIDTH), blk),
                  pl.BlockSpec((nbs, chunk, GATE_LANES), blk),
                  pl.BlockSpec((nbs, HEADS, HEAD_DIM, HEAD_DIM), per_seq4),
                  pl.BlockSpec((nbs, SUBLANES, B_QKV), lambda b, c: (b, 0, 0)),
                  pl.BlockSpec((B_CONV, B_QKV), const),
                  pl.BlockSpec((1, GATE_LANES), const),
                  pl.BlockSpec((1, GATE_LANES), const),
                  pl.BlockSpec((1, HEAD_DIM), const)],
        out_specs=[pl.BlockSpec((nbs, chunk, WIDTH), blk),
                   pl.BlockSpec((nbs, HEADS, HEAD_DIM, HEAD_DIM), per_seq4)],
        out_shape=[jax.ShapeDtypeStruct((nseq, length, WIDTH), F32),
                   jax.ShapeDtypeStruct((nseq, HEADS, HEAD_DIM, HEAD_DIM), F32)],
        scratch_shapes=[pltpu.VMEM((nbs * HEADS, HEAD_DIM, HEAD_DIM), F32),
                        pltpu.VMEM((nbs, chunk + SUBLANES, B_QKV), F32)],
        compiler_params=pltpu.CompilerParams(dimension_semantics=("parallel", "arbitrary")),
        name="gdn",
    )(pb, pg, s0, cbuf, cw, alog, dtb, og)


def _rwkv_kernel(pc_ref, s0_ref, sh_ref, mu_ref, w0_ref, w2_ref, a0_ref, a2_ref, g2_ref,
                 kk_ref, ka_ref, rk_ref, lng_ref, lnb_ref,
                 o_ref, sout_ref, st_sc, xs, *, nbs, chunk, n_valid):
    c = pl.program_id(1)

    @pl.when(c == 0)
    def _():
        for b in range(nbs):
            for h in range(HEADS):
                st_sc[b * HEADS + h] = s0_ref[b, h].T
        xs[:, 0:SUBLANES, :] = sh_ref[...]

    cp = pc_ref[...]
    xs[:, SUBLANES:SUBLANES + chunk, :] = cp
    prev = xs[:, SUBLANES - 1:SUBLANES - 1 + chunk, :]
    xs[:, 0:SUBLANES, :] = xs[:, chunk:chunk + SUBLANES, :]
    xm = (cp + (prev - cp) * mu_ref[...]).reshape(nbs * chunk, C_PROJ)
    r = xm[:, 0:WIDTH]
    k = xm[:, WIDTH:2 * WIDTH]
    v = xm[:, 2 * WIDTH:3 * WIDTH]
    wl = xm[:, 3 * WIDTH:3 * WIDTH + 64]
    al = xm[:, 3 * WIDTH + 64:3 * WIDTH + 128]
    gl = xm[:, 3 * WIDTH + 128:4 * WIDTH]
    w_log = -_softplus(-(w0_ref[...] + _dot(jnp.tanh(wl).astype(BF16), w2_ref[...]))) - 0.5
    logw = -jnp.exp(w_log)
    a = jax.nn.sigmoid(a0_ref[...] + _dot(al.astype(BF16), a2_ref[...]))
    gg = _dot(jax.nn.sigmoid(gl).astype(BF16), g2_ref[...])
    kkraw = k * kk_ref[...]
    k2 = k * (1.0 + (a - 1.0) * ka_ref[...])
    rkk = r * k2 * rk_ref[...]
    if n_valid < chunk:
        t_in_chunk = lax.broadcasted_iota(jnp.int32, (nbs, chunk, WIDTH), 1).reshape(nbs * chunk, WIDTH)
        valid = t_in_chunk < n_valid
        logw = jnp.where(valid, logw, 0.0)
        kkraw = jnp.where(valid, kkraw, 0.0)
        k2 = jnp.where(valid, k2, 0.0)

    row = lax.broadcasted_iota(jnp.int32, (chunk, chunk), 0)
    col = lax.broadcasted_iota(jnp.int32, (chunk, chunk), 1)
    tril = row >= col
    strict = row > col
    trilf = tril.astype(F32)

    def heads(x):
        return _split_heads(x.reshape(nbs, chunk, WIDTH), 0)

    logw3 = logw.reshape(nbs, chunk, WIDTH)
    glog = jnp.stack([_dot(trilf, logw3[b], HI) for b in range(nbs)])
    glast = glog[:, chunk - 1:chunk, :]
    e_n = _split_heads(jnp.exp(-glog), 0)
    e_l = _split_heads(jnp.exp(glast - glog), 0)
    kk = heads(kkraw)
    kk = kk * lax.rsqrt(jnp.sum(kk * kk, axis=-1, keepdims=True) + L2_EPS)
    bh = kk * heads(a)
    vh = heads(v)
    k2h = heads(k2)
    lhs = jnp.concatenate([kk * _split_heads(jnp.exp(glog - logw3), 0),
                           heads(r) * _split_heads(jnp.exp(glog), 0)], axis=1).astype(BF16)
    rhs = jnp.concatenate([bh * e_n, k2h * e_n], axis=1).astype(BF16)
    pair = _bmm_nt(lhs, rhs)
    st = st_sc[...]
    ls = _bmm_nt(lhs, st.astype(BF16))
    lb = jnp.where(strict, pair[:, :chunk, :chunk], 0.0)
    lk = jnp.where(strict, pair[:, :chunk, chunk:], 0.0)
    arb = jnp.where(tril, pair[:, chunk:, :chunk], 0.0)
    ark = jnp.where(tril, pair[:, chunk:, chunk:], 0.0)
    th, tl = _split(_unit_lower_inverse(lb, row, col, n_valid))
    vhb = vh.astype(BF16)
    rh_, rl_ = _split(ls[:, :chunk] + _bmm(lk.astype(BF16), vhb))
    u = _bmm_x3(th, tl, rh_, rl_)
    ub = u.astype(BF16)
    y = ls[:, chunk:] + _bmm(jnp.concatenate([ark, -arb], axis=2).astype(BF16),
                             jnp.concatenate([vhb, ub], axis=1))
    upd = _bmm_tn(jnp.concatenate([vhb, -ub], axis=1),
                  jnp.concatenate([k2h * e_l, bh * e_l], axis=1).astype(BF16))
    st_sc[...] = _split_heads(jnp.exp(glast), 0) * st + upd
    mean = jnp.mean(y, axis=-1, keepdims=True)
    var = jnp.mean(jnp.square(y - mean), axis=-1, keepdims=True)
    yn = (y - mean) * lax.rsqrt(var + GN_EPS)
    bonus = jnp.sum(heads(rkk), axis=-1, keepdims=True) * vh
    gg3 = gg.reshape(nbs, chunk, WIDTH)
    lng = lng_ref[...]
    lnb = lnb_ref[...]
    for b in range(nbs):
        cat = lambda t: jnp.concatenate([t[b * HEADS + h] for h in range(HEADS)], axis=1)
        o_ref[b] = (cat(yn) * lng + lnb + cat(bonus)) * gg3[b]

    @pl.when(c == pl.num_programs(1) - 1)
    def _():
        for b in range(nbs):
            for h in range(HEADS):
                sout_ref[b, h] = st_sc[b * HEADS + h].T


def _rwkv(pc, s0, sh, params, *, nbs, chunk, n_valid):
    nseq, length, _ = pc.shape
    blk = lambda b, c: (b, c, 0)
    per_seq4 = lambda b, c: (b, 0, 0, 0)
    const = lambda b, c: (0, 0)
    pspecs = [pl.BlockSpec(p.shape, const) for p in params]
    return pl.pallas_call(
        functools.partial(_rwkv_kernel, nbs=nbs, chunk=chunk, n_valid=n_valid),
        grid=(nseq // nbs, length // chunk),
        in_specs=[pl.BlockSpec((nbs, chunk, C_PROJ), blk),
                  pl.BlockSpec((nbs, HEADS, HEAD_DIM, HEAD_DIM), per_seq4),
                  pl.BlockSpec((nbs, SUBLANES, C_PROJ), lambda b, c: (b, 0, 0))] + pspecs,
        out_specs=[pl.BlockSpec((nbs, chunk, WIDTH), blk),
                   pl.BlockSpec((nbs, HEADS, HEAD_DIM, HEAD_DIM), per_seq4)],
        out_shape=[jax.ShapeDtypeStruct((nseq, length, WIDTH), F32),
                   jax.ShapeDtypeStruct((nseq, HEADS, HEAD_DIM, HEAD_DIM), F32)],
        scratch_shapes=[pltpu.VMEM((nbs * HEADS, HEAD_DIM, HEAD_DIM), F32),
                        pltpu.VMEM((nbs, chunk + SUBLANES, C_PROJ), F32)],
        compiler_params=pltpu.CompilerParams(dimension_semantics=("parallel", "arbitrary")),
        name="rwkv",
    )(pc, s0, sh, *params)


def _round_up(x, m):
    return (x + m - 1) // m * m


def _pool_offsets(stride):
    offs = []
    a = 0
    for k in range(len(POOL_WINDOWS)):
        a = _round_up(a + (1 << k) * stride, SUBLANES)
        offs.append(a)
    return offs


def _pool_kernel(hist_ref, dp_ref, wbd_ref, scale_ref, out_ref, s1, s2, *, rows, stride, start):
    offs = _pool_offsets(stride)
    d0 = offs[-1]
    n = d0 + rows
    dp = dp_ref[...]
    s1[0:d0, :] = hist_ref[0]
    s1[d0:n, :] = dp
    lane = lax.broadcasted_iota(jnp.int32, (1, WIDTH), 1)
    src, dst = s1, s2
    for k, a in enumerate(offs):
        sh = (1 << k) * stride
        shifted = jnp.where(lane >= k * HEAD_DIM, src[a - sh:n - sh, :], 0.0)
        if k < len(offs) - 1:
            dst[a:n, :] = src[a:n, :] + shifted
            src, dst = dst, src
        else:
            sums = src[a:n, :] + shifted
    assert stride & (stride - 1) == 0
    pos = start + (lax.broadcasted_iota(jnp.int32, (rows, WIDTH), 0) >> (stride.bit_length() - 1))
    window = jnp.left_shift(2, lax.broadcasted_iota(jnp.int32, (rows, WIDTH), 1) >> 6)
    cnt = jnp.minimum(pos + 1, window).astype(F32)
    diff = sums / cnt - dp
    out_ref[...] = _dot(diff.astype(BF16), wbd_ref[...]) * scale_ref[...]


def _pool(hist, dp, wbd, scale, *, nseq, rows, stride, start):
    d0 = _pool_offsets(stride)[-1]
    return pl.pallas_call(
        functools.partial(_pool_kernel, rows=rows, stride=stride, start=start),
        grid=(nseq,),
        in_specs=[pl.BlockSpec((1, d0, WIDTH), lambda b: (b, 0, 0)),
                  pl.BlockSpec((rows, WIDTH), lambda b: (b, 0)),
                  pl.BlockSpec((WIDTH, WIDTH), lambda b: (0, 0)),
                  pl.BlockSpec((1, WIDTH), lambda b: (0, 0))],
        out_specs=pl.BlockSpec((rows, WIDTH), lambda b: (b, 0)),
        out_shape=jax.ShapeDtypeStruct((nseq * rows, WIDTH), F32),
        scratch_shapes=[pltpu.VMEM((d0 + rows, WIDTH), F32)] * 2,
        compiler_params=pltpu.CompilerParams(dimension_semantics=("parallel",)),
        name="pool",
    )(hist, dp, wbd, scale)


def _ffn_kernel(x_ref, ma_ref, mb_ref, mc_ref, md_ref, wo_ref, g2_ref, wu_ref, wd_ref, gf_ref,
                o_ref, *, tf, final):
    mixed = None
    for i, m_ref in enumerate((ma_ref, mb_ref, mc_ref, md_ref)):
        part = _dot(m_ref[...].astype(BF16), wo_ref[i * WIDTH:(i + 1) * WIDTH, :])
        mixed = part if mixed is None else mixed + part
    x = x_ref[...] + mixed
    hm = _rms(x, g2_ref[...]).astype(BF16)
    down = None
    for j in range(D_FF // tf):
        up = jnp.maximum(_dot(hm, wu_ref[:, j * tf:(j + 1) * tf]), 0.0)
        part = _dot((up * up).astype(BF16), wd_ref[j * tf:(j + 1) * tf, :])
        down = part if down is None else down + part
    x = x + down
    if final:
        x = _rms(x, gf_ref[...])
    o_ref[...] = x


def _ffn(x, mixed, wo, g2, wu, wd, gf, *, tm, final):
    t = x.shape[0]
    const = lambda i: (0, 0)
    row = lambda i: (i, 0)
    return pl.pallas_call(
        functools.partial(_ffn_kernel, tf=1024, final=final),
        grid=(t // tm,),
        in_specs=[pl.BlockSpec((tm, D_MODEL), row)] + [pl.BlockSpec((tm, WIDTH), row)] * 4
        + [_resident((D_MODEL, D_MODEL)), pl.BlockSpec((1, D_MODEL), const),
           _resident((D_MODEL, D_FF)), _resident((D_FF, D_MODEL)),
           pl.BlockSpec((1, D_MODEL), const)],
        out_specs=pl.BlockSpec((tm, D_MODEL), row),
        out_shape=jax.ShapeDtypeStruct((t, D_MODEL), F32),
        compiler_params=pltpu.CompilerParams(dimension_semantics=("parallel",),
                                             vmem_limit_bytes=VMEM_LIMIT),
        name="ffn",
    )(x, *mixed, wo, g2, wu, wd, gf)


def _layer_weights(l, w_in, a_ws, a_bs, d_w):
    wl = w_in[l]
    a_end = 2 * WIDTH
    b_end = a_end + 4 * WIDTH
    c_off = b_end + 2 * HEADS
    d_off = c_off + C_PROJ
    wg = jnp.pad(wl[:, b_end:c_off], ((0, 0), (0, GATE_LANES - 2 * HEADS)))
    ws = [wl[:, :a_end], wl[:, a_end:b_end], wl[:, c_off:d_off], wl[:, d_off:], wg]
    ws = [w.astype(BF16) for w in ws]
    chunk = a_ws.shape[-1]
    causal = jnp.tril(jnp.ones((chunk, chunk), dtype=bool))
    wm = jnp.where(causal, a_ws[l], 0.0)
    gc = WIDTH // len(POOL_WINDOWS)
    wbd = jnp.zeros((WIDTH, WIDTH), F32)
    for g in range(len(POOL_WINDOWS)):
        wbd = wbd.at[g * gc:(g + 1) * gc, g * gc:(g + 1) * gc].set(d_w[l, g])
    return ws, wm, wbd.astype(BF16)


def _pad_lanes(v, n):
    return jnp.pad(v, (0, n - v.shape[0]))[None, :]


def kernel(x_prompt, x_sample, state_b_conv, state_b_ssm, state_c_shift, state_c_wkv, state_d_pool, norm1_g, w_in, a_ws, a_bs, a_vnorm_g, b_conv_w, b_a_log, b_dt_bias, b_onorm_g, c_mu, c_w0, c_w2, c_a0, c_a2, c_g2, c_k_k, c_k_a, c_r_k, c_ln_g, c_ln_b, d_w, d_scale, w_out, norm2_g, w_up, w_down, final_g):
    nb, seq, _ = x_prompt.shape
    ns, dseq, _ = x_sample.shape
    b_chunk = 64
    s_pad = SUBLANES

    xp = x_prompt.reshape(nb * seq, D_MODEL)
    xs = jnp.transpose(x_sample, (1, 0, 2)).reshape(dseq * ns, D_MODEL)

    def to_seq_major_padded(a):
        a = jnp.transpose(a.reshape(dseq, ns, a.shape[-1]), (1, 0, 2))
        return jnp.pad(a, ((0, 0), (0, s_pad - dseq), (0, 0)))

    def to_time_major(a):
        return jnp.transpose(a[:, :dseq], (1, 0, 2)).reshape(dseq * ns, a.shape[-1])

    outs = {k: [] for k in ("p_bc", "p_bs", "p_cs", "p_cw", "p_dp",
                            "s_av", "s_bc", "s_bs", "s_cs", "s_cw", "s_dp")}
    for l in range(DEPTH):
        ws, wm, wbd = _layer_weights(l, w_in, a_ws, a_bs, d_w)
        g1 = norm1_g[l][None, :]
        vg = a_vnorm_g[l][None, :]
        alog = _pad_lanes(jnp.concatenate([jnp.zeros((HEADS,), F32), b_a_log[l]]), GATE_LANES)
        dtb = _pad_lanes(jnp.concatenate([jnp.zeros((HEADS,), F32), b_dt_bias[l]]), GATE_LANES)
        og = b_onorm_g[l][None, :]
        cparams = [c_mu[l][None, :], c_w0[l][None, :], c_w2[l].astype(BF16), c_a0[l][None, :],
                   c_a2[l].astype(BF16), c_g2[l].astype(BF16),
                   c_k_k[l][None, :], c_k_a[l][None, :], c_r_k[l].reshape(1, WIDTH),
                   c_ln_g[l][None, :], c_ln_b[l][None, :]]
        dscale = d_scale[l][None, :]
        wo = w_out[l].astype(BF16)
        g2 = norm2_g[l][None, :]
        wu = w_up[l].astype(BF16)
        wd = w_down[l].astype(BF16)
        gf = final_g[None, :]
        final = l == DEPTH - 1

        pa, pb, pc, pd, pg = _inproj(xp, g1, ws, 512)
        bias_p = jnp.repeat(a_bs[l].T, HEAD_DIM, axis=1)
        a_out, _ = _gmlp(pa, wm.astype(BF16), bias_p, vg)
        b_out, p_bs = _gdn(pb.reshape(nb, seq, 4 * WIDTH), pg.reshape(nb, seq, GATE_LANES),
                           jnp.zeros((nb, HEADS, HEAD_DIM, HEAD_DIM), F32),
                           jnp.zeros((nb, SUBLANES, B_QKV), F32), b_conv_w[l], alog, dtb, og,
                           nbs=PROMPT_SEQS_PER_STEP, chunk=b_chunk, n_valid=b_chunk)
        b_out = b_out.reshape(nb * seq, WIDTH)
        c_out, p_cw = _rwkv(pc.reshape(nb, seq, C_PROJ), jnp.zeros((nb, HEADS, HEAD_DIM, HEAD_DIM), F32),
                            jnp.zeros((nb, SUBLANES, C_PROJ), F32), cparams,
                            nbs=PROMPT_SEQS_PER_STEP, chunk=b_chunk, n_valid=b_chunk)
        c_out = c_out.reshape(nb * seq, WIDTH)
        d0 = _pool_offsets(1)[-1]
        d_out = _pool(jnp.zeros((nb, d0, WIDTH), F32), pd, wbd, dscale,
                      nseq=nb, rows=seq, stride=1, start=0)
        xp = _ffn(xp, (a_out, b_out, c_out, d_out), wo, g2, wu, wd, gf, tm=512, final=final)
        outs["p_bc"].append(pb.reshape(nb, seq, 4 * WIDTH)[:, seq - (B_CONV - 1):, :B_QKV])
        outs["p_bs"].append(p_bs)
        outs["p_cs"].append(pc.reshape(nb, seq, C_PROJ)[:, seq - 1])
        outs["p_cw"].append(p_cw)
        outs["p_dp"].append(pd.reshape(nb, seq, WIDTH)[:, seq - POOL_BUF:])

        pa, pb, pc, pd, pg = _inproj(xs, g1, ws, 256)
        rows_s = dseq * ns
        eye = jnp.eye(ns, dtype=F32)
        mm_s = jnp.stack([jnp.kron(wm[h, :dseq, :dseq], eye) for h in range(HEADS)]).astype(BF16)
        bias_s = jnp.repeat(jnp.repeat(a_bs[l].T[:dseq], HEAD_DIM, axis=1), ns, axis=0)
        a_out, a_v = _gmlp(pa, mm_s, bias_s, vg)
        cbuf = jnp.pad(state_b_conv[l], ((0, 0), (SUBLANES - (B_CONV - 1), 0), (0, 0)))
        b_out, s_bs = _gdn(to_seq_major_padded(pb), to_seq_major_padded(pg), state_b_ssm[l], cbuf,
                           b_conv_w[l], alog, dtb, og, nbs=SAMPLE_SEQS_PER_STEP, chunk=s_pad, n_valid=dseq)
        b_out = to_time_major(b_out)
        sh = jnp.pad(state_c_shift[l][:, None, :], ((0, 0), (SUBLANES - 1, 0), (0, 0)))
        c_out, s_cw = _rwkv(to_seq_major_padded(pc), state_c_wkv[l], sh, cparams,
                            nbs=SAMPLE_SEQS_PER_STEP, chunk=s_pad, n_valid=dseq)
        c_out = to_time_major(c_out)
        hist = jnp.transpose(state_d_pool[l], (1, 0, 2)).reshape(1, POOL_BUF * ns, WIDTH)
        d_out = _pool(hist, pd, wbd, dscale, nseq=1, rows=rows_s, stride=ns, start=PAST_LEN)
        xs = _ffn(xs, (a_out, b_out, c_out, d_out), wo, g2, wu, wd, gf, tm=256, final=final)

        def seq_major(a):
            return jnp.transpose(a.reshape(dseq, ns, a.shape[-1]), (1, 0, 2))

        outs["s_av"].append(seq_major(a_v))
        outs["s_bc"].append(jnp.concatenate([state_b_conv[l], seq_major(pb)[:, :, :B_QKV]],
                                            axis=1)[:, -(B_CONV - 1):])
        outs["s_bs"].append(s_bs)
        outs["s_cs"].append(seq_major(pc)[:, dseq - 1])
        outs["s_cw"].append(s_cw)
        outs["s_dp"].append(jnp.concatenate([state_d_pool[l], seq_major(pd)], axis=1)[:, -POOL_BUF:])

    y_prompt = xp.reshape(nb, seq, D_MODEL)
    y_sample = jnp.transpose(xs.reshape(dseq, ns, D_MODEL), (1, 0, 2))
    st = {k: jnp.stack(v) for k, v in outs.items()}
    return (y_prompt, y_sample, st["p_bc"], st["p_bs"], st["p_cs"], st["p_cw"], st["p_dp"],
            st["s_av"], st["s_bc"], st["s_bs"], st["s_cs"], st["s_cw"], st["s_dp"])
```

```python
import functools

import jax
import jax.numpy as jnp
from jax import lax
from jax.experimental import pallas as pl
from jax.experimental.pallas import tpu as pltpu

F32 = jnp.float32
BF16 = jnp.bfloat16
HI = lax.Precision.HIGHEST

D_MODEL = 1024
DEPTH = 2
HEADS = 4
HEAD_DIM = 64
WIDTH = HEADS * HEAD_DIM
B_QKV = 3 * WIDTH
B_CONV = 4
C_PROJ = 4 * WIDTH
POOL_WINDOWS = (2, 4, 8, 16)
POOL_BUF = 15
D_FF = 4 * D_MODEL
PAST_LEN = 16384
NORM_EPS = 1e-6
L2_EPS = 1e-6
GN_EPS = 64e-5
GATE_LANES = 128
SUBLANES = 8
VMEM_LIMIT = 48 * 1024 * 1024
PROMPT_SEQS_PER_STEP = 4
SAMPLE_SEQS_PER_STEP = 8


def _rms(x, g):
    return x * lax.rsqrt(jnp.mean(x * x, axis=-1, keepdims=True) + NORM_EPS) * g


def _softplus(x):
    return jnp.maximum(x, 0.0) + jnp.log1p(jnp.exp(-jnp.abs(x)))


def _dot(a, b, precision=None):
    return jnp.dot(a, b, precision=precision, preferred_element_type=F32)


def _dot_nt(a, b, precision=None):
    return lax.dot_general(a, b, (((1,), (1,)), ((), ())), precision=precision,
                           preferred_element_type=F32)


def _dot_tn(a, b, precision=None):
    return lax.dot_general(a, b, (((0,), (0,)), ((), ())), precision=precision,
                           preferred_element_type=F32)


def _resident(shape):
    return pl.BlockSpec(shape, lambda *_: (0,) * len(shape), pipeline_mode=pl.Buffered(1))


def _bmm(a, b):
    return jnp.einsum('bij,bjk->bik', a, b, preferred_element_type=F32)


def _bmm_nt(a, b):
    return jnp.einsum('bik,bjk->bij', a, b, preferred_element_type=F32)


def _bmm_tn(a, b):
    return jnp.einsum('bki,bkj->bij', a, b, preferred_element_type=F32)


def _split_heads(x, lane0):
    return jnp.stack([x[b, :, lane0 + h * HEAD_DIM:lane0 + (h + 1) * HEAD_DIM]
                      for b in range(x.shape[0]) for h in range(HEADS)])


def _unit_lower_inverse(lm, row, col, n_valid):
    def sub_diag_block(shift):
        return (((row >> (shift + 1)) == (col >> (shift + 1)))
                & (((row >> shift) & 1) == 1) & (((col >> shift) & 1) == 0))

    m = (row == col).astype(F32) - jnp.where(sub_diag_block(0), lm, 0.0)
    shift = 1
    while (1 << shift) < n_valid:
        cs = jnp.where(sub_diag_block(shift), lm, 0.0).astype(BF16)
        mb = m.astype(BF16)
        m = m - _bmm(_bmm(mb, cs).astype(BF16), mb)
        shift += 1
    return m


def _transpose_rows(a):
    n = a.shape[0]
    if n < GATE_LANES:
        a = jnp.concatenate([a, jnp.zeros((GATE_LANES - n, a.shape[1]), a.dtype)], axis=0)
    return a.T[:, :n]


def _inproj_kernel(x_ref, g_ref, wa_ref, wb_ref, wc_ref, wd_ref, wg_ref,
                   pa_ref, pb_ref, pc_ref, pd_ref, pg_ref):
    h = _rms(x_ref[...], g_ref[...]).astype(BF16)
    for w_ref, o_ref in ((wa_ref, pa_ref), (wb_ref, pb_ref), (wc_ref, pc_ref), (wd_ref, pd_ref),
                         (wg_ref, pg_ref)):
        o_ref[...] = _dot(h, w_ref[...])


def _inproj(x, g, ws, tm):
    t = x.shape[0]
    widths = [w.shape[1] for w in ws]
    const = lambda i: (0, 0)
    return pl.pallas_call(
        _inproj_kernel,
        grid=(t // tm,),
        in_specs=[pl.BlockSpec((tm, D_MODEL), lambda i: (i, 0)), pl.BlockSpec((1, D_MODEL), const)]
        + [_resident((D_MODEL, n)) for n in widths],
        out_specs=[pl.BlockSpec((tm, n), lambda i: (i, 0)) for n in widths],
        out_shape=[jax.ShapeDtypeStruct((t, n), F32) for n in widths],
        compiler_params=pltpu.CompilerParams(dimension_semantics=("parallel",),
                                             vmem_limit_bytes=VMEM_LIMIT),
        name="inproj",
    )(x, g, *ws)


def _gmlp_kernel(pa_ref, mm_ref, bias_ref, vg_ref, out_ref, v_ref, *, r):
    pa = pa_ref[...]
    u = jax.nn.gelu(pa[:, :WIDTH])
    v = _rms(jax.nn.gelu(pa[:, WIDTH:]), vg_ref[...])
    v_ref[...] = v
    lane_head = lax.broadcasted_iota(jnp.int32, (r, WIDTH), 1) >> 6
    mm = mm_ref[...]
    bias = bias_ref[...]
    for i in range(pa.shape[0] // r):
        vc = v[i * r:(i + 1) * r]
        per_head = jnp.concatenate([jnp.where(lane_head == h, vc, 0.0).astype(BF16)
                                    for h in range(HEADS)], axis=0)
        out_ref[i * r:(i + 1) * r, :] = u[i * r:(i + 1) * r] * (_dot(mm, per_head) + bias)


def _gmlp(pa, mm, bias, vg, tile):
    t = pa.shape[0]
    r = mm.shape[0]
    return pl.pallas_call(
        functools.partial(_gmlp_kernel, r=r),
        grid=(t // tile,),
        in_specs=[pl.BlockSpec((tile, 2 * WIDTH), lambda i: (i, 0)),
                  pl.BlockSpec((r, HEADS * r), lambda i: (0, 0)),
                  pl.BlockSpec((r, WIDTH), lambda i: (0, 0)),
                  pl.BlockSpec((1, WIDTH), lambda i: (0, 0))],
        out_specs=[pl.BlockSpec((tile, WIDTH), lambda i: (i, 0))] * 2,
        out_shape=[jax.ShapeDtypeStruct((t, WIDTH), F32)] * 2,
        compiler_params=pltpu.CompilerParams(dimension_semantics=("parallel",)),
        name="gmlp",
    )(pa, mm, bias, vg)


def _gdn_kernel(pb_ref, pg_ref, s0_ref, cbuf_ref, cw_ref, alog_ref, dtb_ref, og_ref,
                o_ref, sout_ref, s_sc, xbuf, *, nbs, chunk, n_valid):
    c = pl.program_id(1)
    nb = nbs * HEADS

    @pl.when(c == 0)
    def _():
        s_sc[...] = s0_ref[...].reshape(nb, HEAD_DIM, HEAD_DIM)
        xbuf[:, 0:SUBLANES, :] = cbuf_ref[...]

    raw = pb_ref[:, :, 0:B_QKV]
    xbuf[:, SUBLANES:SUBLANES + chunk, :] = raw
    cw = cw_ref[...]
    conv = xbuf[:, SUBLANES - 3:SUBLANES - 3 + chunk, :] * cw[0:1]
    conv = conv + xbuf[:, SUBLANES - 2:SUBLANES - 2 + chunk, :] * cw[1:2]
    conv = conv + xbuf[:, SUBLANES - 1:SUBLANES - 1 + chunk, :] * cw[2:3]
    conv = conv + raw * cw[3:4]
    xbuf[:, 0:SUBLANES, :] = xbuf[:, chunk:chunk + SUBLANES, :]
    qkv = jax.nn.silu(conv)
    gate = jax.nn.silu(pb_ref[:, :, B_QKV:B_QKV + WIDTH])

    row = lax.broadcasted_iota(jnp.int32, (chunk, chunk), 0)
    col = lax.broadcasted_iota(jnp.int32, (chunk, chunk), 1)
    tril = row >= col
    strict = row > col
    trilf = tril.astype(F32)

    pg = pg_ref[...]
    beta_all = jax.nn.sigmoid(pg)
    g_all = -jnp.exp(alog_ref[...]) * _softplus(pg + dtb_ref[...])
    if n_valid < chunk:
        valid = lax.broadcasted_iota(jnp.int32, pg.shape, 1) < n_valid
        beta_all = jnp.where(valid, beta_all, 0.0)
        g_all = jnp.where(valid, g_all, 0.0)

    q = _split_heads(qkv, 0)
    k = _split_heads(qkv, WIDTH)
    v = _split_heads(qkv, 2 * WIDTH)
    q = q * lax.rsqrt(jnp.sum(q * q, axis=-1, keepdims=True) + L2_EPS) * (HEAD_DIM ** -0.5)
    k = k * lax.rsqrt(jnp.sum(k * k, axis=-1, keepdims=True) + L2_EPS)
    gc = [_dot(trilf, g_all[b], HI) for b in range(nbs)]
    gct = [_transpose_rows(g) for g in gc]
    pairs = [(b, HEADS + h) for b in range(nbs) for h in range(HEADS)]
    gcol = jnp.stack([gc[b][:, j:j + 1] for b, j in pairs])
    grow = jnp.stack([gct[b][j:j + 1, :] for b, j in pairs])
    bcol = jnp.stack([beta_all[b, :, h:h + 1] for b in range(nbs) for h in range(HEADS)])
    glast = gcol[:, chunk - 1:chunk, :]
    egc = jnp.exp(gcol)
    decay = jnp.where(tril, jnp.exp(jnp.minimum(gcol - grow, 0.0)), 0.0)
    kb = k.astype(BF16)
    qkk = _bmm_nt(jnp.concatenate([q, k], axis=1).astype(BF16), kb)
    qk = qkk[:, :chunk] * decay
    lm = jnp.where(strict, bcol * qkk[:, chunk:] * decay, 0.0)
    tinv = _unit_lower_inverse(lm, row, col, n_valid)
    uw = _bmm(tinv.astype(BF16), jnp.concatenate([bcol * v, (bcol * egc) * k], axis=2).astype(BF16))
    u = uw[:, :, :HEAD_DIM]
    wk = uw[:, :, HEAD_DIM:]
    s = s_sc[...]
    ws = _bmm(jnp.concatenate([wk, q * egc], axis=1).astype(BF16), s.astype(BF16))
    wnb = (u - ws[:, :chunk]).astype(BF16)
    o = ws[:, chunk:] + _bmm(qk.astype(BF16), wnb)
    kt = k * jnp.exp(glast - gcol)
    s_sc[...] = jnp.exp(glast) * s + _bmm_tn(kt.astype(BF16), wnb)
    o = _rms(o, og_ref[...])
    for b in range(nbs):
        o_ref[b] = jnp.concatenate([o[b * HEADS + h] for h in range(HEADS)], axis=1) * gate[b]

    @pl.when(c == pl.num_programs(1) - 1)
    def _():
        sout_ref[...] = s_sc[...].reshape(nbs, HEADS, HEAD_DIM, HEAD_DIM)


def _gdn(pb, pg, s0, cbuf, cw, alog, dtb, og, *, nbs, chunk, n_valid):
    nseq, length, _ = pb.shape
    blk = lambda b, c: (b, c, 0)
    per_seq4 = lambda b, c: (b, 0, 0, 0)
    const = lambda b, c: (0, 0)
    return pl.pallas_call(
        functools.partial(_gdn_kernel, nbs=nbs, chunk=chunk, n_valid=n_valid),
        grid=(nseq // nbs, length // chunk),
        in_specs=[pl.BlockSpec((nbs, chunk, 4 * WIDTH), blk),
                  pl.BlockSpec((nbs, chunk, GATE_LANES), blk),
                  pl.BlockSpec((nbs, HEADS, HEAD_DIM, HEAD_DIM), per_seq4),
                  pl.BlockSpec((nbs, SUBLANES, B_QKV), lambda b, c: (b, 0, 0)),
                  pl.BlockSpec((B_CONV, B_QKV), const),
                  pl.BlockSpec((1, GATE_LANES), const),
                  pl.BlockSpec((1, GATE_LANES), const),
                  pl.BlockSpec((1, HEAD_DIM), const)],
        out_specs=[pl.BlockSpec((nbs, chunk, WIDTH), blk),
                   pl.BlockSpec((nbs, HEADS, HEAD_DIM, HEAD_DIM), per_seq4)],
        out_shape=[jax.ShapeDtypeStruct((nseq, length, WIDTH), F32),
                   jax.ShapeDtypeStruct((nseq, HEADS, HEAD_DIM, HEAD_DIM), F32)],
        scratch_shapes=[pltpu.VMEM((nbs * HEADS, HEAD_DIM, HEAD_DIM), F32),
                        pltpu.VMEM((nbs, chunk + SUBLANES, B_QKV), F32)],
        compiler_params=pltpu.CompilerParams(dimension_semantics=("parallel", "arbitrary")),
        name="gdn",
    )(pb, pg, s0, cbuf, cw, alog, dtb, og)


def _rwkv_kernel(pc_ref, s0_ref, sh_ref, mu_ref, w0_ref, w2_ref, a0_ref, a2_ref, g2_ref,
                 kk_ref, ka_ref, rk_ref, lng_ref, lnb_ref,
                 o_ref, sout_ref, st_sc, xs, *, nbs, chunk, n_valid):
    c = pl.program_id(1)

    @pl.when(c == 0)
    def _():
        for b in range(nbs):
            for h in range(HEADS):
                st_sc[b * HEADS + h] = s0_ref[b, h].T
        xs[:, 0:SUBLANES, :] = sh_ref[...]

    cp = pc_ref[...]
    xs[:, SUBLANES:SUBLANES + chunk, :] = cp
    prev = xs[:, SUBLANES - 1:SUBLANES - 1 + chunk, :]
    xs[:, 0:SUBLANES, :] = xs[:, chunk:chunk + SUBLANES, :]
    xm = (cp + (prev - cp) * mu_ref[...]).reshape(nbs * chunk, C_PROJ)
    r = xm[:, 0:WIDTH]
    k = xm[:, WIDTH:2 * WIDTH]
    v = xm[:, 2 * WIDTH:3 * WIDTH]
    wl = xm[:, 3 * WIDTH:3 * WIDTH + 64]
    al = xm[:, 3 * WIDTH + 64:3 * WIDTH + 128]
    gl = xm[:, 3 * WIDTH + 128:4 * WIDTH]
    w_log = -_softplus(-(w0_ref[...] + _dot(jnp.tanh(wl).astype(BF16), w2_ref[...]))) - 0.5
    logw = -jnp.exp(w_log)
    a = jax.nn.sigmoid(a0_ref[...] + _dot(al.astype(BF16), a2_ref[...]))
    gg = _dot(jax.nn.sigmoid(gl).astype(BF16), g2_ref[...])
    kkraw = k * kk_ref[...]
    k2 = k * (1.0 + (a - 1.0) * ka_ref[...])
    rkk = r * k2 * rk_ref[...]
    if n_valid < chunk:
        t_in_chunk = lax.broadcasted_iota(jnp.int32, (nbs, chunk, WIDTH), 1).reshape(nbs * chunk, WIDTH)
        valid = t_in_chunk < n_valid
        logw = jnp.where(valid, logw, 0.0)
        kkraw = jnp.where(valid, kkraw, 0.0)
        k2 = jnp.where(valid, k2, 0.0)

    row = lax.broadcasted_iota(jnp.int32, (chunk, chunk), 0)
    col = lax.broadcasted_iota(jnp.int32, (chunk, chunk), 1)
    tril = row >= col
    strict = row > col
    trilf = tril.astype(F32)

    def heads(x):
        return _split_heads(x.reshape(nbs, chunk, WIDTH), 0)

    logw3 = logw.reshape(nbs, chunk, WIDTH)
    glog = jnp.stack([_dot(trilf, logw3[b], HI) for b in range(nbs)])
    glast = glog[:, chunk - 1:chunk, :]
    e_n = _split_heads(jnp.exp(-glog), 0)
    e_l = _split_heads(jnp.exp(glast - glog), 0)
    kk = heads(kkraw)
    kk = kk * lax.rsqrt(jnp.sum(kk * kk, axis=-1, keepdims=True) + L2_EPS)
    bh = kk * heads(a)
    vh = heads(v)
    k2h = heads(k2)
    lhs = jnp.concatenate([kk * _split_heads(jnp.exp(glog - logw3), 0),
                           heads(r) * _split_heads(jnp.exp(glog), 0)], axis=1).astype(BF16)
    rhs = jnp.concatenate([bh * e_n, k2h * e_n], axis=1).astype(BF16)
    pair = _bmm_nt(lhs, rhs)
    st = st_sc[...]
    ls = _bmm_nt(lhs, st.astype(BF16))
    lb = jnp.where(strict, pair[:, :chunk, :chunk], 0.0)
    lk = jnp.where(strict, pair[:, :chunk, chunk:], 0.0)
    arb = jnp.where(tril, pair[:, chunk:, :chunk], 0.0)
    ark = jnp.where(tril, pair[:, chunk:, chunk:], 0.0)
    tinv = _unit_lower_inverse(lb, row, col, n_valid)
    vhb = vh.astype(BF16)
    u = _bmm(tinv.astype(BF16), (ls[:, :chunk] + _bmm(lk.astype(BF16), vhb)).astype(BF16))
    ub = u.astype(BF16)
    y = ls[:, chunk:] + _bmm(jnp.concatenate([ark, -arb], axis=2).astype(BF16),
                             jnp.concatenate([vhb, ub], axis=1))
    upd = _bmm_tn(jnp.concatenate([vhb, -ub], axis=1),
                  jnp.concatenate([k2h * e_l, bh * e_l], axis=1).astype(BF16))
    st_sc[...] = _split_heads(jnp.exp(glast), 0) * st + upd
    mean = jnp.mean(y, axis=-1, keepdims=True)
    var = jnp.mean(jnp.square(y - mean), axis=-1, keepdims=True)
    yn = (y - mean) * lax.rsqrt(var + GN_EPS)
    bonus = jnp.sum(heads(rkk), axis=-1, keepdims=True) * vh
    gg3 = gg.reshape(nbs, chunk, WIDTH)
    lng = lng_ref[...]
    lnb = lnb_ref[...]
    for b in range(nbs):
        cat = lambda t: jnp.concatenate([t[b * HEADS + h] for h in range(HEADS)], axis=1)
        o_ref[b] = (cat(yn) * lng + lnb + cat(bonus)) * gg3[b]

    @pl.when(c == pl.num_programs(1) - 1)
    def _():
        for b in range(nbs):
            for h in range(HEADS):
                sout_ref[b, h] = st_sc[b * HEADS + h].T


def _rwkv(pc, s0, sh, params, *, nbs, chunk, n_valid):
    nseq, length, _ = pc.shape
    blk = lambda b, c: (b, c, 0)
    per_seq4 = lambda b, c: (b, 0, 0, 0)
    const = lambda b, c: (0, 0)
    pspecs = [pl.BlockSpec(p.shape, const) for p in params]
    return pl.pallas_call(
        functools.partial(_rwkv_kernel, nbs=nbs, chunk=chunk, n_valid=n_valid),
        grid=(nseq // nbs, length // chunk),
        in_specs=[pl.BlockSpec((nbs, chunk, C_PROJ), blk),
                  pl.BlockSpec((nbs, HEADS, HEAD_DIM, HEAD_DIM), per_seq4),
                  pl.BlockSpec((nbs, SUBLANES, C_PROJ), lambda b, c: (b, 0, 0))] + pspecs,
        out_specs=[pl.BlockSpec((nbs, chunk, WIDTH), blk),
                   pl.BlockSpec((nbs, HEADS, HEAD_DIM, HEAD_DIM), per_seq4)],
        out_shape=[jax.ShapeDtypeStruct((nseq, length, WIDTH), F32),
                   jax.ShapeDtypeStruct((nseq, HEADS, HEAD_DIM, HEAD_DIM), F32)],
        scratch_shapes=[pltpu.VMEM((nbs * HEADS, HEAD_DIM, HEAD_DIM), F32),
                        pltpu.VMEM((nbs, chunk + SUBLANES, C_PROJ), F32)],
        compiler_params=pltpu.CompilerParams(dimension_semantics=("parallel", "arbitrary")),
        name="rwkv",
    )(pc, s0, sh, *params)


def _round_up(x, m):
    return (x + m - 1) // m * m


def _pool_offsets(stride):
    offs = []
    a = 0
    for k in range(len(POOL_WINDOWS)):
        a = _round_up(a + (1 << k) * stride, SUBLANES)
        offs.append(a)
    return offs


def _pool_kernel(hist_ref, dp_ref, wbd_ref, scale_ref, out_ref, s1, s2, *, rows, stride, start):
    offs = _pool_offsets(stride)
    d0 = offs[-1]
    n = d0 + rows
    dp = dp_ref[...]
    s1[0:d0, :] = hist_ref[0]
    s1[d0:n, :] = dp
    lane = lax.broadcasted_iota(jnp.int32, (1, WIDTH), 1)
    src, dst = s1, s2
    for k, a in enumerate(offs):
        sh = (1 << k) * stride
        shifted = jnp.where(lane >= k * HEAD_DIM, src[a - sh:n - sh, :], 0.0)
        if k < len(offs) - 1:
            dst[a:n, :] = src[a:n, :] + shifted
            src, dst = dst, src
        else:
            sums = src[a:n, :] + shifted
    assert stride & (stride - 1) == 0
    pos = start + (lax.broadcasted_iota(jnp.int32, (rows, WIDTH), 0) >> (stride.bit_length() - 1))
    window = jnp.left_shift(2, lax.broadcasted_iota(jnp.int32, (rows, WIDTH), 1) >> 6)
    cnt = jnp.minimum(pos + 1, window).astype(F32)
    diff = sums / cnt - dp
    out_ref[...] = _dot(diff.astype(BF16), wbd_ref[...]) * scale_ref[...]


def _pool(hist, dp, wbd, scale, *, nseq, rows, stride, start):
    d0 = _pool_offsets(stride)[-1]
    return pl.pallas_call(
        functools.partial(_pool_kernel, rows=rows, stride=stride, start=start),
        grid=(nseq,),
        in_specs=[pl.BlockSpec((1, d0, WIDTH), lambda b: (b, 0, 0)),
                  pl.BlockSpec((rows, WIDTH), lambda b: (b, 0)),
                  pl.BlockSpec((WIDTH, WIDTH), lambda b: (0, 0)),
                  pl.BlockSpec((1, WIDTH), lambda b: (0, 0))],
        out_specs=pl.BlockSpec((rows, WIDTH), lambda b: (b, 0)),
        out_shape=jax.ShapeDtypeStruct((nseq * rows, WIDTH), F32),
        scratch_shapes=[pltpu.VMEM((d0 + rows, WIDTH), F32)] * 2,
        compiler_params=pltpu.CompilerParams(dimension_semantics=("parallel",)),
        name="pool",
    )(hist, dp, wbd, scale)


def _ffn_kernel(x_ref, ma_ref, mb_ref, mc_ref, md_ref, wo_ref, g2_ref, wu_ref, wd_ref, gf_ref,
                o_ref, *, tf, final):
    mixed = None
    for i, m_ref in enumerate((ma_ref, mb_ref, mc_ref, md_ref)):
        part = _dot(m_ref[...].astype(BF16), wo_ref[i * WIDTH:(i + 1) * WIDTH, :])
        mixed = part if mixed is None else mixed + part
    x = x_ref[...] + mixed
    hm = _rms(x, g2_ref[...]).astype(BF16)
    down = None
    for j in range(D_FF // tf):
        up = jnp.maximum(_dot(hm, wu_ref[:, j * tf:(j + 1) * tf]), 0.0)
        part = _dot((up * up).astype(BF16), wd_ref[j * tf:(j + 1) * tf, :])
        down = part if down is None else down + part
    x = x + down
    if final:
        x = _rms(x, gf_ref[...])
    o_ref[...] = x


def _ffn(x, mixed, wo, g2, wu, wd, gf, *, tm, final):
    t = x.shape[0]
    const = lambda i: (0, 0)
    row = lambda i: (i, 0)
    return pl.pallas_call(
        functools.partial(_ffn_kernel, tf=1024, final=final),
        grid=(t // tm,),
        in_specs=[pl.BlockSpec((tm, D_MODEL), row)] + [pl.BlockSpec((tm, WIDTH), row)] * 4
        + [_resident((D_MODEL, D_MODEL)), pl.BlockSpec((1, D_MODEL), const),
           _resident((D_MODEL, D_FF)), _resident((D_FF, D_MODEL)),
           pl.BlockSpec((1, D_MODEL), const)],
        out_specs=pl.BlockSpec((tm, D_MODEL), row),
        out_shape=jax.ShapeDtypeStruct((t, D_MODEL), F32),
        compiler_params=pltpu.CompilerParams(dimension_semantics=("parallel",),
                                             vmem_limit_bytes=VMEM_LIMIT),
        name="ffn",
    )(x, *mixed, wo, g2, wu, wd, gf)


def _layer_weights(l, w_in, a_ws, a_bs, d_w):
    wl = w_in[l]
    a_end = 2 * WIDTH
    b_end = a_end + 4 * WIDTH
    c_off = b_end + 2 * HEADS
    d_off = c_off + C_PROJ
    wg = jnp.pad(wl[:, b_end:c_off], ((0, 0), (0, GATE_LANES - 2 * HEADS)))
    ws = [wl[:, :a_end], wl[:, a_end:b_end], wl[:, c_off:d_off], wl[:, d_off:], wg]
    ws = [w.astype(BF16) for w in ws]
    chunk = a_ws.shape[-1]
    causal = jnp.tril(jnp.ones((chunk, chunk), dtype=bool))
    wm = jnp.where(causal, a_ws[l], 0.0)
    gc = WIDTH // len(POOL_WINDOWS)
    wbd = jnp.zeros((WIDTH, WIDTH), F32)
    for g in range(len(POOL_WINDOWS)):
        wbd = wbd.at[g * gc:(g + 1) * gc, g * gc:(g + 1) * gc].set(d_w[l, g])
    return ws, wm, wbd.astype(BF16)


def _pad_lanes(v, n):
    return jnp.pad(v, (0, n - v.shape[0]))[None, :]


def kernel(x_prompt, x_sample, state_b_conv, state_b_ssm, state_c_shift, state_c_wkv, state_d_pool, norm1_g, w_in, a_ws, a_bs, a_vnorm_g, b_conv_w, b_a_log, b_dt_bias, b_onorm_g, c_mu, c_w0, c_w2, c_a0, c_a2, c_g2, c_k_k, c_k_a, c_r_k, c_ln_g, c_ln_b, d_w, d_scale, w_out, norm2_g, w_up, w_down, final_g):
    nb, seq, _ = x_prompt.shape
    ns, dseq, _ = x_sample.shape
    b_chunk = 64
    s_pad = SUBLANES

    xp = x_prompt.reshape(nb * seq, D_MODEL)
    xs = jnp.transpose(x_sample, (1, 0, 2)).reshape(dseq * ns, D_MODEL)

    def to_seq_major_padded(a):
        a = jnp.transpose(a.reshape(dseq, ns, a.shape[-1]), (1, 0, 2))
        return jnp.pad(a, ((0, 0), (0, s_pad - dseq), (0, 0)))

    def to_time_major(a):
        return jnp.transpose(a[:, :dseq], (1, 0, 2)).reshape(dseq * ns, a.shape[-1])

    outs = {k: [] for k in ("p_bc", "p_bs", "p_cs", "p_cw", "p_dp",
                            "s_av", "s_bc", "s_bs", "s_cs", "s_cw", "s_dp")}
    for l in range(DEPTH):
        ws, wm, wbd = _layer_weights(l, w_in, a_ws, a_bs, d_w)
        g1 = norm1_g[l][None, :]
        vg = a_vnorm_g[l][None, :]
        alog = _pad_lanes(jnp.concatenate([jnp.zeros((HEADS,), F32), b_a_log[l]]), GATE_LANES)
        dtb = _pad_lanes(jnp.concatenate([jnp.zeros((HEADS,), F32), b_dt_bias[l]]), GATE_LANES)
        og = b_onorm_g[l][None, :]
        cparams = [c_mu[l][None, :], c_w0[l][None, :], c_w2[l].astype(BF16), c_a0[l][None, :],
                   c_a2[l].astype(BF16), c_g2[l].astype(BF16),
                   c_k_k[l][None, :], c_k_a[l][None, :], c_r_k[l].reshape(1, WIDTH),
                   c_ln_g[l][None, :], c_ln_b[l][None, :]]
        dscale = d_scale[l][None, :]
        wo = w_out[l].astype(BF16)
        g2 = norm2_g[l][None, :]
        wu = w_up[l].astype(BF16)
        wd = w_down[l].astype(BF16)
        gf = final_g[None, :]
        final = l == DEPTH - 1

        pa, pb, pc, pd, pg = _inproj(xp, g1, ws, 512)
        bias_p = jnp.repeat(a_bs[l].T, HEAD_DIM, axis=1)
        a_chunk = wm.shape[-1]
        mm_p = jnp.transpose(wm, (1, 0, 2)).reshape(a_chunk, HEADS * a_chunk).astype(BF16)
        a_out, _ = _gmlp(pa, mm_p, bias_p, vg, 4 * a_chunk)
        b_out, p_bs = _gdn(pb.reshape(nb, seq, 4 * WIDTH), pg.reshape(nb, seq, GATE_LANES),
                           jnp.zeros((nb, HEADS, HEAD_DIM, HEAD_DIM), F32),
                           jnp.zeros((nb, SUBLANES, B_QKV), F32), b_conv_w[l], alog, dtb, og,
                           nbs=PROMPT_SEQS_PER_STEP, chunk=b_chunk, n_valid=b_chunk)
        b_out = b_out.reshape(nb * seq, WIDTH)
        c_out, p_cw = _rwkv(pc.reshape(nb, seq, C_PROJ), jnp.zeros((nb, HEADS, HEAD_DIM, HEAD_DIM), F32),
                            jnp.zeros((nb, SUBLANES, C_PROJ), F32), cparams,
                            nbs=PROMPT_SEQS_PER_STEP, chunk=b_chunk, n_valid=b_chunk)
        c_out = c_out.reshape(nb * seq, WIDTH)
        d0 = _pool_offsets(1)[-1]
        d_out = _pool(jnp.zeros((nb, d0, WIDTH), F32), pd, wbd, dscale,
                      nseq=nb, rows=seq, stride=1, start=0)
        xp = _ffn(xp, (a_out, b_out, c_out, d_out), wo, g2, wu, wd, gf, tm=512, final=final)
        outs["p_bc"].append(pb.reshape(nb, seq, 4 * WIDTH)[:, seq - (B_CONV - 1):, :B_QKV])
        outs["p_bs"].append(p_bs)
        outs["p_cs"].append(pc.reshape(nb, seq, C_PROJ)[:, seq - 1])
        outs["p_cw"].append(p_cw)
        outs["p_dp"].append(pd.reshape(nb, seq, WIDTH)[:, seq - POOL_BUF:])

        pa, pb, pc, pd, pg = _inproj(xs, g1, ws, 256)
        rows_s = dseq * ns
        eye = jnp.eye(ns, dtype=F32)
        mm_s = jnp.concatenate([jnp.kron(wm[h, :dseq, :dseq], eye) for h in range(HEADS)],
                               axis=1).astype(BF16)
        bias_s = jnp.repeat(jnp.repeat(a_bs[l].T[:dseq], HEAD_DIM, axis=1), ns, axis=0)
        a_out, a_v = _gmlp(pa, mm_s, bias_s, vg, rows_s)
        cbuf = jnp.pad(state_b_conv[l], ((0, 0), (SUBLANES - (B_CONV - 1), 0), (0, 0)))
        b_out, s_bs = _gdn(to_seq_major_padded(pb), to_seq_major_padded(pg), state_b_ssm[l], cbuf,
                           b_conv_w[l], alog, dtb, og, nbs=SAMPLE_SEQS_PER_STEP, chunk=s_pad, n_valid=dseq)
        b_out = to_time_major(b_out)
        sh = jnp.pad(state_c_shift[l][:, None, :], ((0, 0), (SUBLANES - 1, 0), (0, 0)))
        c_out, s_cw = _rwkv(to_seq_major_padded(pc), state_c_wkv[l], sh, cparams,
                            nbs=SAMPLE_SEQS_PER_STEP, chunk=s_pad, n_valid=dseq)
        c_out = to_time_major(c_out)
        hist = jnp.transpose(state_d_pool[l], (1, 0, 2)).reshape(1, POOL_BUF * ns, WIDTH)
        d_out = _pool(hist, pd, wbd, dscale, nseq=1, rows=rows_s, stride=ns, start=PAST_LEN)
        xs = _ffn(xs, (a_out, b_out, c_out, d_out), wo, g2, wu, wd, gf, tm=256, final=final)

        def seq_major(a):
            return jnp.transpose(a.reshape(dseq, ns, a.shape[-1]), (1, 0, 2))

        outs["s_av"].append(seq_major(a_v))
        outs["s_bc"].append(jnp.concatenate([state_b_conv[l], seq_major(pb)[:, :, :B_QKV]],
                                            axis=1)[:, -(B_CONV - 1):])
        outs["s_bs"].append(s_bs)
        outs["s_cs"].append(seq_major(pc)[:, dseq - 1])
        outs["s_cw"].append(s_cw)
        outs["s_dp"].append(jnp.concatenate([state_d_pool[l], seq_major(pd)], axis=1)[:, -POOL_BUF:])

    y_prompt = xp.reshape(nb, seq, D_MODEL)
    y_sample = jnp.transpose(xs.reshape(dseq, ns, D_MODEL), (1, 0, 2))
    st = {k: jnp.stack(v) for k, v in outs.items()}
    return (y_prompt, y_sample, st["p_bc"], st["p_bs"], st["p_cs"], st["p_cw"], st["p_dp"],
            st["s_av"], st["s_bc"], st["s_bs"], st["s_cs"], st["s_cw"], st["s_dp"])
```

```python
import functools

import jax
import jax.numpy as jnp
from jax import lax
from jax.experimental import pallas as pl
from jax.experimental.pallas import tpu as pltpu

F32 = jnp.float32
BF16 = jnp.bfloat16

D_MODEL = 1024
DEPTH = 2
HEADS = 4
HEAD_DIM = 64
WIDTH = HEADS * HEAD_DIM
B_QKV = 3 * WIDTH
B_CONV = 4
C_PROJ = 4 * WIDTH
POOL_WINDOWS = (2, 4, 8, 16)
POOL_BUF = 15
D_FF = 4 * D_MODEL
PAST_LEN = 16384
NORM_EPS = 1e-6
L2_EPS = 1e-6
GN_EPS = 64e-5
GATE_LANES = 128
SUBLANES = 8
VMEM_LIMIT = 48 * 1024 * 1024
MIX_CHUNK = 64
PROMPT_SEQS_PER_STEP = 4
SAMPLE_SEQS_PER_STEP = 8
ROW_TILE = 512


def _rms(x, g):
    return x * lax.rsqrt(jnp.mean(x * x, axis=-1, keepdims=True) + NORM_EPS) * g


def _softplus(x):
    return jnp.maximum(x, 0.0) + jnp.log1p(jnp.exp(-jnp.abs(x)))


def _dot(a, b):
    return jnp.dot(a, b, preferred_element_type=F32)


def _bmm(a, b):
    return jnp.einsum('bij,bjk->bik', a, b, preferred_element_type=F32)


def _bmm_nt(a, b):
    return jnp.einsum('bik,bjk->bij', a, b, preferred_element_type=F32)


def _bmm_tn(a, b):
    return jnp.einsum('bki,bkj->bij', a, b, preferred_element_type=F32)


def _layer_block(arr, l, single_buffer=False):
    shape = arr.shape[1:]
    index = lambda *_: (l,) + (0,) * len(shape)
    if single_buffer:
        return pl.BlockSpec((None,) + shape, index, pipeline_mode=pl.Buffered(1))
    return pl.BlockSpec((None,) + shape, index)


def _whole(arr):
    return pl.BlockSpec(arr.shape, lambda *_: (0,) * arr.ndim)


def _split_heads(x, lane0):
    return jnp.stack([x[b, :, lane0 + h * HEAD_DIM:lane0 + (h + 1) * HEAD_DIM]
                      for b in range(x.shape[0]) for h in range(HEADS)])


def _merge_heads(x, nbs):
    return jnp.concatenate([jnp.concatenate([x[b * HEADS + h] for h in range(HEADS)], axis=1)
                            for b in range(nbs)], axis=0)


def _group_sum(x, bd):
    hi = x.astype(BF16)
    lo = (x - hi.astype(F32)).astype(BF16)
    return _dot(hi, bd) + _dot(lo, bd)


def _split3(x):
    p1 = x.astype(BF16)
    r1 = x - p1.astype(F32)
    p2 = r1.astype(BF16)
    return p1, p2, (r1 - p2.astype(F32)).astype(BF16)


def _cumsum_rows(tril_b, x):
    w = x.shape[1]
    y = _dot(tril_b, jnp.concatenate(_split3(x), axis=1))
    return y[:, :w] + (y[:, w:2 * w] + y[:, 2 * w:])


def _unit_lower_inverse(lm, row, col, n_valid):
    def sub_diag_block(shift):
        return (((row >> (shift + 1)) == (col >> (shift + 1)))
                & (((row >> shift) & 1) == 1) & (((col >> shift) & 1) == 0))

    m = (row == col).astype(F32) - jnp.where(sub_diag_block(0), lm, 0.0)
    shift = 1
    while (1 << shift) < n_valid:
        cs = jnp.where(sub_diag_block(shift), lm, 0.0).astype(BF16)
        mb = m.astype(BF16)
        m = m - _bmm(_bmm(mb, cs).astype(BF16), mb)
        shift += 1
    return m


def _transpose_rows(a):
    n = a.shape[0]
    if n < GATE_LANES:
        a = jnp.concatenate([a, jnp.zeros((GATE_LANES - n, a.shape[1]), a.dtype)], axis=0)
    return a.T[:, :n]


def _stage_rows(ref, stage, in_rows):
    if stage is None:
        return ref[...]
    stage[...] = jnp.zeros(stage.shape, stage.dtype)
    stage[:, 0:in_rows, :] = ref[...]
    return stage[...]


def _inproj_kernel(x_ref, g_ref, wa_ref, wb_ref, wc_ref, wd_ref, wg_ref,
                   pa_ref, pb_ref, pc_ref, pd_ref, pg_ref):
    h = _rms(x_ref[...], g_ref[...]).astype(BF16)
    for w_ref, o_ref in ((wa_ref, pa_ref), (wb_ref, pb_ref), (wc_ref, pc_ref), (wd_ref, pd_ref),
                         (wg_ref, pg_ref)):
        o_ref[...] = _dot(h, w_ref[...])


def _inproj(x, g, ws, l):
    t = x.shape[0]
    tm = min(ROW_TILE, t)
    widths = [w.shape[2] for w in ws]
    return pl.pallas_call(
        _inproj_kernel,
        grid=(t // tm,),
        in_specs=[pl.BlockSpec((tm, D_MODEL), lambda i: (i, 0)), _layer_block(g, l)]
        + [_layer_block(w, l, single_buffer=True) for w in ws],
        out_specs=[pl.BlockSpec((tm, n), lambda i: (i, 0)) for n in widths],
        out_shape=[jax.ShapeDtypeStruct((t, n), F32) for n in widths],
        compiler_params=pltpu.CompilerParams(dimension_semantics=("parallel",),
                                             vmem_limit_bytes=VMEM_LIMIT),
        name="inproj",
    )(x, g, *ws)


def _gmlp_kernel(pa_ref, mm_ref, bias_ref, vg_ref, out_ref, v_ref, *, r):
    pa = pa_ref[...]
    u = jax.nn.gelu(pa[:, :WIDTH])
    v = _rms(jax.nn.gelu(pa[:, WIDTH:]), vg_ref[...])
    v_ref[...] = v
    lane_head = lax.broadcasted_iota(jnp.int32, (r, WIDTH), 1) >> 6
    mm = mm_ref[...]
    bias = bias_ref[...]
    for i in range(pa.shape[0] // r):
        vc = v[i * r:(i + 1) * r]
        per_head = jnp.concatenate([jnp.where(lane_head == h, vc, 0.0).astype(BF16)
                                    for h in range(HEADS)], axis=0)
        out_ref[i * r:(i + 1) * r, :] = u[i * r:(i + 1) * r] * (_dot(mm, per_head) + bias)


def _gmlp(pa, mm, bias, vg, l):
    t = pa.shape[0]
    r = mm.shape[1]
    tile = max(r, min(ROW_TILE, t))
    return pl.pallas_call(
        functools.partial(_gmlp_kernel, r=r),
        grid=(t // tile,),
        in_specs=[pl.BlockSpec((tile, 2 * WIDTH), lambda i: (i, 0)),
                  _layer_block(mm, l), _layer_block(bias, l), _layer_block(vg, l)],
        out_specs=[pl.BlockSpec((tile, WIDTH), lambda i: (i, 0))] * 2,
        out_shape=[jax.ShapeDtypeStruct((t, WIDTH), F32)] * 2,
        compiler_params=pltpu.CompilerParams(dimension_semantics=("parallel",)),
        name="gmlp",
    )(pa, mm, bias, vg)


def _store_state(sout_ref, prev_ref, new_state):
    if prev_ref is None:
        sout_ref[...] = new_state
    else:
        n_prev = prev_ref.shape[0]
        sout_ref[0:n_prev] = prev_ref[...]
        sout_ref[n_prev] = new_state


def _recurrent_call(kernel_fn, name, x_blocks, s0, s0_layer, carry_in, consts, prev, *,
                    nbs, chunk, carry_width, out_width=WIDTH):
    nseq, length, _ = x_blocks[0].shape
    in_rows = min(chunk, length)
    n_chunks = length // in_rows
    blk = lambda b, c: (b, c, 0)
    state_block = (nbs, HEADS, HEAD_DIM, HEAD_DIM)
    in_specs = [pl.BlockSpec((nbs, in_rows, x.shape[2]), blk) for x in x_blocks]
    in_specs += [pl.BlockSpec((None,) + state_block, lambda b, c: (s0_layer, b, 0, 0, 0)),
                 pl.BlockSpec((None, nbs) + carry_in.shape[2:], lambda b, c: (s0_layer, b, 0, 0))]
    in_specs += [_whole(a) if l is None else _layer_block(a, l) for a, l in consts]
    operands = list(x_blocks) + [s0, carry_in] + [a for a, _ in consts]
    if prev is None:
        state_shape = (nseq, HEADS, HEAD_DIM, HEAD_DIM)
        state_spec = pl.BlockSpec(state_block, lambda b, c: (b, 0, 0, 0))
    else:
        n_prev = prev.shape[0]
        in_specs.append(pl.BlockSpec((n_prev,) + state_block, lambda b, c: (0, b, 0, 0, 0)))
        operands.append(prev)
        state_shape = (n_prev + 1, nseq, HEADS, HEAD_DIM, HEAD_DIM)
        state_spec = pl.BlockSpec((n_prev + 1,) + state_block, lambda b, c: (0, b, 0, 0, 0))
    scratch = [pltpu.VMEM((nbs * HEADS, HEAD_DIM, HEAD_DIM), F32),
               pltpu.VMEM((nbs, chunk + SUBLANES, carry_width), F32)]
    if in_rows < chunk:
        scratch += [pltpu.VMEM((nbs, chunk, x.shape[2]), F32) for x in x_blocks]
    return pl.pallas_call(
        functools.partial(kernel_fn, nbs=nbs, chunk=chunk, in_rows=in_rows, n_x=len(x_blocks),
                          n_consts=len(consts), has_prev=prev is not None, carry=n_chunks > 1),
        grid=(nseq // nbs, n_chunks),
        in_specs=in_specs,
        out_specs=[pl.BlockSpec((nbs, in_rows, out_width), blk), state_spec],
        out_shape=[jax.ShapeDtypeStruct((nseq, length, out_width), F32),
                   jax.ShapeDtypeStruct(state_shape, F32)],
        scratch_shapes=scratch,
        compiler_params=pltpu.CompilerParams(dimension_semantics=("parallel", "arbitrary")),
        name=name,
    )(*operands)


def _unpack_refs(refs, n_x, n_consts, has_prev, staged):
    x_refs = refs[:n_x]
    s0_ref, carry_ref = refs[n_x:n_x + 2]
    consts = refs[n_x + 2:n_x + 2 + n_consts]
    pos = n_x + 2 + n_consts
    prev_ref = refs[pos] if has_prev else None
    pos += int(has_prev)
    o_ref, sout_ref, state_sc, rows_sc = refs[pos:pos + 4]
    stages = refs[pos + 4:] if staged else (None,) * n_x
    return x_refs, s0_ref, carry_ref, consts, prev_ref, o_ref, sout_ref, state_sc, rows_sc, stages


def _gdn_kernel(*refs, nbs, chunk, in_rows, n_x, n_consts, has_prev, carry):
    ((pb_ref, pg_ref), s0_ref, cbuf_ref, (cw_ref, vec_ref, bd_ref, eb_ref, eg_ref), prev_ref,
     o_ref, sout_ref, s_sc, xbuf, (stage_b, stage_g)) = _unpack_refs(refs, n_x, n_consts, has_prev,
                                                                    in_rows < chunk)
    c = pl.program_id(1)
    nb = nbs * HEADS
    rows = nbs * chunk
    n_hist = B_CONV - 1

    @pl.when(c == 0)
    def _():
        s_sc[...] = s0_ref[...].reshape(nb, HEAD_DIM, HEAD_DIM)
        xbuf[:, SUBLANES - n_hist:SUBLANES, :] = cbuf_ref[...]

    pb = _stage_rows(pb_ref, stage_b, in_rows)
    pg = _stage_rows(pg_ref, stage_g, in_rows)
    raw = pb[:, :, 0:B_QKV]
    xbuf[:, SUBLANES:SUBLANES + chunk, :] = raw
    cw = cw_ref[...]
    conv = xbuf[:, SUBLANES - 3:SUBLANES - 3 + chunk, :] * cw[0:1]
    conv = conv + xbuf[:, SUBLANES - 2:SUBLANES - 2 + chunk, :] * cw[1:2]
    conv = conv + xbuf[:, SUBLANES - 1:SUBLANES - 1 + chunk, :] * cw[2:3]
    conv = conv + raw * cw[3:4]
    if carry:
        xbuf[:, 0:SUBLANES, :] = xbuf[:, chunk:chunk + SUBLANES, :]
    qkv = jax.nn.silu(conv).reshape(rows, B_QKV)
    gate = jax.nn.silu(pb[:, :, B_QKV:B_QKV + WIDTH]).reshape(rows, WIDTH)

    row = lax.broadcasted_iota(jnp.int32, (chunk, chunk), 0)
    col = lax.broadcasted_iota(jnp.int32, (chunk, chunk), 1)
    tril = row >= col
    strict = row > col
    trilb = tril.astype(BF16)

    vec = vec_ref[...]
    alog = vec[:, 0:GATE_LANES]
    dtb = vec[:, GATE_LANES:2 * GATE_LANES]
    og = vec[:, 2 * GATE_LANES:2 * GATE_LANES + WIDTH]
    beta_all = jax.nn.sigmoid(pg)
    g_all = -jnp.exp(alog) * _softplus(pg + dtb)
    if in_rows < chunk:
        valid = lax.broadcasted_iota(jnp.int32, pg.shape, 1) < in_rows
        beta_all = jnp.where(valid, beta_all, 0.0)
        g_all = jnp.where(valid, g_all, 0.0)

    bd = bd_ref[...]
    qf = qkv[:, 0:WIDTH]
    kf = qkv[:, WIDTH:2 * WIDTH]
    vf = qkv[:, 2 * WIDTH:]
    qn = qf * lax.rsqrt(_dot((qf * qf).astype(BF16), bd) + L2_EPS) * (HEAD_DIM ** -0.5)
    kn = kf * lax.rsqrt(_dot((kf * kf).astype(BF16), bd) + L2_EPS)
    b1, b2, _ = _split3(beta_all.reshape(rows, GATE_LANES))
    beta_f = _dot(jnp.concatenate([b1, b2], axis=1), eb_ref[...])
    gc_small = [_cumsum_rows(trilb, g_all[b]) for b in range(nbs)]
    gc = _dot(jnp.concatenate(_split3(jnp.concatenate(gc_small, axis=0)), axis=1),
              eg_ref[...]).reshape(nbs, chunk, WIDTH)
    glast = gc[:, chunk - 1:chunk, :]
    e_g = jnp.exp(gc).reshape(rows, WIDTH)
    e_gl = jnp.exp(glast - gc).reshape(rows, WIDTH)
    bk = beta_f * kn

    def heads(x):
        return _split_heads(x.reshape(nbs, chunk, WIDTH).astype(BF16), 0)

    qkk = _bmm_nt(jnp.concatenate([heads(qn), heads(bk)], axis=1), heads(kn))
    grow_all = [_transpose_rows(g) for g in gc_small]
    gcol = jnp.stack([gc[b, :, h * HEAD_DIM:h * HEAD_DIM + 1] for b in range(nbs) for h in range(HEADS)])
    grow = jnp.stack([grow_all[b][HEADS + h:HEADS + h + 1, :] for b in range(nbs) for h in range(HEADS)])
    decay = jnp.where(tril, jnp.exp(jnp.minimum(gcol - grow, 0.0)), 0.0)
    qk = qkk[:, :chunk] * decay
    lm = jnp.where(strict, qkk[:, chunk:] * decay, 0.0)
    tinv = _unit_lower_inverse(lm, row, col, in_rows)
    rhs = jnp.concatenate([heads(beta_f * vf), heads(bk * e_g)], axis=2)
    uw = _bmm(tinv.astype(BF16), rhs)
    u = uw[:, :, :HEAD_DIM]
    wk = uw[:, :, HEAD_DIM:]
    s = s_sc[...]
    ws = _bmm(jnp.concatenate([wk.astype(BF16), heads(qn * e_g)], axis=1),
              s.astype(BF16))
    wnb = (u - ws[:, :chunk]).astype(BF16)
    o = ws[:, chunk:] + _bmm(qk.astype(BF16), wnb)
    s_sc[...] = _split_heads(jnp.exp(glast), 0) * s + _bmm_tn(heads(kn * e_gl), wnb)
    of = _merge_heads(o, nbs)
    of = of * lax.rsqrt(_group_sum(of * of, bd) * (1.0 / HEAD_DIM) + NORM_EPS) * og * gate
    o_ref[...] = of.reshape(nbs, chunk, WIDTH)[:, 0:in_rows, :]

    @pl.when(c == pl.num_programs(1) - 1)
    def _():
        _store_state(sout_ref, prev_ref, s_sc[...].reshape(nbs, HEADS, HEAD_DIM, HEAD_DIM))


def _rwkv_kernel(*refs, nbs, chunk, in_rows, n_x, n_consts, has_prev, carry):
    ((pc_ref,), s0_ref, sh_ref, (vec_ref, w2_ref, a2_ref, g2_ref, bd_ref), prev_ref,
     o_ref, sout_ref, st_sc, xs, (stage_c,)) = _unpack_refs(refs, n_x, n_consts, has_prev, in_rows < chunk)
    c = pl.program_id(1)
    rows = nbs * chunk

    @pl.when(c == 0)
    def _():
        for b in range(nbs):
            for h in range(HEADS):
                st_sc[b * HEADS + h] = s0_ref[b, h].T
        xs[:, SUBLANES - 1:SUBLANES, :] = sh_ref[...]

    cp = _stage_rows(pc_ref, stage_c, in_rows)
    xs[:, SUBLANES:SUBLANES + chunk, :] = cp
    prev = xs[:, SUBLANES - 1:SUBLANES - 1 + chunk, :]
    if carry:
        xs[:, 0:SUBLANES, :] = xs[:, chunk:chunk + SUBLANES, :]
    vec = vec_ref[...]
    mu = vec[:, 0:C_PROJ]
    w0, a0, k_k, k_a, r_k, ln_g, ln_b = (vec[:, C_PROJ + i * WIDTH:C_PROJ + (i + 1) * WIDTH] for i in range(7))
    xm = (cp + (prev - cp) * mu).reshape(rows, C_PROJ)
    r = xm[:, 0:WIDTH]
    k = xm[:, WIDTH:2 * WIDTH]
    v = xm[:, 2 * WIDTH:3 * WIDTH]
    wl = xm[:, 3 * WIDTH:3 * WIDTH + 64]
    al = xm[:, 3 * WIDTH + 64:3 * WIDTH + 128]
    gl = xm[:, 3 * WIDTH + 128:4 * WIDTH]
    w_log = -_softplus(-(w0 + _dot(jnp.tanh(wl).astype(BF16), w2_ref[...]))) - 0.5
    logw = -jnp.exp(w_log)
    a = jax.nn.sigmoid(a0 + _dot(al.astype(BF16), a2_ref[...]))
    gg = _dot(jax.nn.sigmoid(gl).astype(BF16), g2_ref[...])
    kkraw = k * k_k
    k2 = k * (1.0 + (a - 1.0) * k_a)
    rkk = r * k2 * r_k
    if in_rows < chunk:
        t_in_chunk = lax.broadcasted_iota(jnp.int32, (nbs, chunk, WIDTH), 1).reshape(rows, WIDTH)
        valid = t_in_chunk < in_rows
        logw = jnp.where(valid, logw, 0.0)
        kkraw = jnp.where(valid, kkraw, 0.0)
        k2 = jnp.where(valid, k2, 0.0)

    row = lax.broadcasted_iota(jnp.int32, (chunk, chunk), 0)
    col = lax.broadcasted_iota(jnp.int32, (chunk, chunk), 1)
    tril = row >= col
    strict = row > col
    trilb = tril.astype(BF16)

    def heads(x):
        return _split_heads(x.reshape(nbs, chunk, WIDTH).astype(BF16), 0)

    bd = bd_ref[...]
    kkn = kkraw * lax.rsqrt(_dot((kkraw * kkraw).astype(BF16), bd) + L2_EPS)
    bf = kkn * a
    logw3 = logw.reshape(nbs, chunk, WIDTH)
    glog3 = jnp.stack([_cumsum_rows(trilb, logw3[b]) for b in range(nbs)])
    glast = glog3[:, chunk - 1:chunk, :]
    glog = glog3.reshape(rows, WIDTH)
    e_n = jnp.exp(-glog)
    e_l = jnp.exp(glast - glog3).reshape(rows, WIDTH)
    lhs = jnp.concatenate([heads(kkn * jnp.exp(glog - logw)), heads(r * jnp.exp(glog))], axis=1)
    rhs = jnp.concatenate([heads(bf * e_n), heads(k2 * e_n)], axis=1)
    pair = _bmm_nt(lhs, rhs)
    st = st_sc[...]
    ls = _bmm_nt(lhs, st.astype(BF16))
    lb = jnp.where(strict, pair[:, :chunk, :chunk], 0.0)
    lk = jnp.where(strict, pair[:, :chunk, chunk:], 0.0)
    arb = jnp.where(tril, pair[:, chunk:, :chunk], 0.0)
    ark = jnp.where(tril, pair[:, chunk:, chunk:], 0.0)
    tinv = _unit_lower_inverse(lb, row, col, in_rows)
    vhb = heads(v)
    ub = _bmm(tinv.astype(BF16), (ls[:, :chunk] + _bmm(lk.astype(BF16), vhb)).astype(BF16)).astype(BF16)
    y = ls[:, chunk:] + _bmm(jnp.concatenate([ark, -arb], axis=2).astype(BF16),
                             jnp.concatenate([vhb, ub], axis=1))
    upd = _bmm_tn(jnp.concatenate([vhb, -ub], axis=1),
                  jnp.concatenate([heads(k2 * e_l), heads(bf * e_l)], axis=1))
    st_sc[...] = _split_heads(jnp.exp(glast), 0) * st + upd
    yf = _merge_heads(y, nbs)
    dev = yf - _group_sum(yf, bd) * (1.0 / HEAD_DIM)
    var = _group_sum(dev * dev, bd) * (1.0 / HEAD_DIM)
    yn = dev * lax.rsqrt(var + GN_EPS) * ln_g + ln_b
    out = (yn + _group_sum(rkk, bd) * v) * gg
    o_ref[...] = out.reshape(nbs, chunk, WIDTH)[:, 0:in_rows, :]

    @pl.when(c == pl.num_programs(1) - 1)
    def _():
        _store_state(sout_ref, prev_ref,
                     jnp.stack([jnp.stack([st_sc[b * HEADS + h].T for h in range(HEADS)])
                                for b in range(nbs)]))


def _round_up(x, m):
    return (x + m - 1) // m * m


def _pool_offsets(stride):
    offs = []
    a = 0
    for k in range(len(POOL_WINDOWS)):
        a = _round_up(a + (1 << k) * stride, SUBLANES)
        offs.append(a)
    return offs


def _pool_kernel(hist_ref, dp_ref, wbd_ref, scale_ref, out_ref, s1, s2, *, rows, stride, start):
    offs = _pool_offsets(stride)
    d0 = offs[-1]
    n = d0 + rows
    dp = dp_ref[...]
    s1[0:d0, :] = hist_ref[0]
    s1[d0:n, :] = dp
    lane = lax.broadcasted_iota(jnp.int32, (1, WIDTH), 1)
    src, dst = s1, s2
    for k, a in enumerate(offs):
        sh = (1 << k) * stride
        shifted = jnp.where(lane >= k * HEAD_DIM, src[a - sh:n - sh, :], 0.0)
        if k < len(offs) - 1:
            dst[a:n, :] = src[a:n, :] + shifted
            src, dst = dst, src
        else:
            sums = src[a:n, :] + shifted
    assert stride & (stride - 1) == 0
    pos = start + (lax.broadcasted_iota(jnp.int32, (rows, WIDTH), 0) >> (stride.bit_length() - 1))
    window = jnp.left_shift(2, lax.broadcasted_iota(jnp.int32, (rows, WIDTH), 1) >> 6)
    cnt = jnp.minimum(pos + 1, window).astype(F32)
    diff = sums / cnt - dp
    out_ref[...] = _dot(diff.astype(BF16), wbd_ref[...]) * scale_ref[...]


def _pool(hist, dp, wbd, scale, l, *, nseq, rows, stride, start):
    d0 = _pool_offsets(stride)[-1]
    return pl.pallas_call(
        functools.partial(_pool_kernel, rows=rows, stride=stride, start=start),
        grid=(nseq,),
        in_specs=[pl.BlockSpec((1, d0, WIDTH), lambda b: (b, 0, 0)),
                  pl.BlockSpec((rows, WIDTH), lambda b: (b, 0)),
                  _layer_block(wbd, l), _layer_block(scale, l)],
        out_specs=pl.BlockSpec((rows, WIDTH), lambda b: (b, 0)),
        out_shape=jax.ShapeDtypeStruct((nseq * rows, WIDTH), F32),
        scratch_shapes=[pltpu.VMEM((d0 + rows, WIDTH), F32)] * 2,
        compiler_params=pltpu.CompilerParams(dimension_semantics=("parallel",)),
        name="pool",
    )(hist, dp, wbd, scale)


def _ffn_kernel(x_ref, ma_ref, mb_ref, mc_ref, md_ref, wo_ref, g2_ref, wu_ref, wd_ref, gf_ref,
                o_ref, *, tf, final):
    mixed = None
    for i, m_ref in enumerate((ma_ref, mb_ref, mc_ref, md_ref)):
        part = _dot(m_ref[...].astype(BF16), wo_ref[i * WIDTH:(i + 1) * WIDTH, :])
        mixed = part if mixed is None else mixed + part
    x = x_ref[...] + mixed
    hm = _rms(x, g2_ref[...]).astype(BF16)
    down = None
    for j in range(D_FF // tf):
        up = jnp.maximum(_dot(hm, wu_ref[:, j * tf:(j + 1) * tf]), 0.0)
        part = _dot((up * up).astype(BF16), wd_ref[j * tf:(j + 1) * tf, :])
        down = part if down is None else down + part
    x = x + down
    if final:
        x = _rms(x, gf_ref[...])
    o_ref[...] = x


def _ffn(x, mixed, wo, g2, wu, wd, gf, l):
    t = x.shape[0]
    tm = min(ROW_TILE, t)
    row = lambda i: (i, 0)
    return pl.pallas_call(
        functools.partial(_ffn_kernel, tf=1024, final=l == DEPTH - 1),
        grid=(t // tm,),
        in_specs=[pl.BlockSpec((tm, D_MODEL), row)] + [pl.BlockSpec((tm, WIDTH), row)] * 4
        + [_layer_block(wo, l, single_buffer=True), _layer_block(g2, l),
           _layer_block(wu, l, single_buffer=True), _layer_block(wd, l, single_buffer=True), _whole(gf)],
        out_specs=pl.BlockSpec((tm, D_MODEL), row),
        out_shape=jax.ShapeDtypeStruct((t, D_MODEL), F32),
        compiler_params=pltpu.CompilerParams(dimension_semantics=("parallel",),
                                             vmem_limit_bytes=VMEM_LIMIT),
        name="ffn",
    )(x, *mixed, wo, g2, wu, wd, gf)


def _rows(v):
    return v[:, None, :]


def kernel(x_prompt, x_sample, state_b_conv, state_b_ssm, state_c_shift, state_c_wkv, state_d_pool, norm1_g, w_in, a_ws, a_bs, a_vnorm_g, b_conv_w, b_a_log, b_dt_bias, b_onorm_g, c_mu, c_w0, c_w2, c_a0, c_a2, c_g2, c_k_k, c_k_a, c_r_k, c_ln_g, c_ln_b, d_w, d_scale, w_out, norm2_g, w_up, w_down, final_g):
    nb, seq, _ = x_prompt.shape
    ns, dseq, _ = x_sample.shape
    a_chunk = a_ws.shape[-1]

    a_end = 2 * WIDTH
    b_end = a_end + 4 * WIDTH
    c_off = b_end + 2 * HEADS
    d_off = c_off + C_PROJ
    w_in_b = w_in.astype(BF16)
    w_cols = [w_in_b[:, :, :a_end], w_in_b[:, :, a_end:b_end], w_in_b[:, :, c_off:d_off], w_in_b[:, :, d_off:],
              jnp.pad(w_in_b[:, :, b_end:c_off], ((0, 0), (0, 0), (0, GATE_LANES - 2 * HEADS)))]
    wo = w_out.astype(BF16)
    wu = w_up.astype(BF16)
    wd = w_down.astype(BF16)
    g1 = _rows(norm1_g)
    g2 = _rows(norm2_g)
    gf = final_g[None, :]
    vg = _rows(a_vnorm_g)
    causal = jnp.tril(jnp.ones((a_chunk, a_chunk), dtype=bool))
    wm = jnp.where(causal, a_ws, 0.0)
    bias_t = jnp.repeat(jnp.transpose(a_bs, (0, 2, 1)), HEAD_DIM, axis=2)
    mm_p = jnp.transpose(wm, (0, 2, 1, 3)).reshape(DEPTH, a_chunk, HEADS * a_chunk).astype(BF16)
    eye_s = jnp.eye(ns, dtype=F32)
    mm_s = jnp.einsum('lhts,bc->lbthcs', wm[:, :, :dseq, :dseq], eye_s).reshape(
        DEPTH, ns * dseq, HEADS * ns * dseq).astype(BF16)
    bias_s = jnp.tile(bias_t[:, :dseq], (1, ns, 1))
    lead = jnp.zeros((DEPTH, HEADS), F32)
    tail = jnp.zeros((DEPTH, GATE_LANES - 2 * HEADS), F32)
    gvec = jnp.concatenate([lead, b_a_log, tail, lead, b_dt_bias, tail, jnp.tile(b_onorm_g, (1, HEADS))],
                           axis=1)[:, None, :]
    cvec = jnp.concatenate([c_mu, c_w0, c_a0, c_k_k, c_k_a, c_r_k.reshape(DEPTH, WIDTH), c_ln_g, c_ln_b],
                           axis=1)[:, None, :]
    c_w2b, c_a2b, c_g2b = c_w2.astype(BF16), c_a2.astype(BF16), c_g2.astype(BF16)
    n_groups = len(POOL_WINDOWS)
    wbd = jnp.einsum('lgcd,gh->lgchd', d_w, jnp.eye(n_groups, dtype=F32)).reshape(
        DEPTH, WIDTH, WIDTH).astype(BF16)
    dscale = _rows(d_scale)
    lane_head = jnp.arange(WIDTH) // HEAD_DIM
    gate_lane = jnp.arange(GATE_LANES)
    bd = (lane_head[:, None] == lane_head[None, :]).astype(BF16)
    eb = jnp.tile((gate_lane[:, None] == lane_head[None, :]).astype(BF16), (2, 1))
    eg = jnp.tile((gate_lane[:, None] == lane_head[None, :] + HEADS).astype(BF16), (3, 1))

    def run_group(x, l, grp, prev_b, prev_c):
        nseq, length = grp["nseq"], grp["length"]
        pa, pb, pc, pd, pg = _inproj(x, g1, w_cols, l)
        a_out, a_v = _gmlp(pa, grp["mm"], grp["bias"], vg, l)
        pb3 = pb.reshape(nseq, length, 4 * WIDTH)
        pc3 = pc.reshape(nseq, length, C_PROJ)
        b_out, b_state = _recurrent_call(
            _gdn_kernel, "gdn", [pb3, pg.reshape(nseq, length, GATE_LANES)], grp["b_ssm"], grp["state_layer"](l),
            grp["b_conv"], [(b_conv_w, l), (gvec, l), (bd, None), (eb, None), (eg, None)], prev_b,
            nbs=grp["nbs"], chunk=grp["chunk"], carry_width=B_QKV)
        c_out, c_state = _recurrent_call(
            _rwkv_kernel, "rwkv", [pc3], grp["c_wkv"], grp["state_layer"](l), grp["c_shift"],
            [(cvec, l), (c_w2b, l), (c_a2b, l), (c_g2b, l), (bd, None)], prev_c,
            nbs=grp["nbs"], chunk=grp["chunk"], carry_width=C_PROJ)
        d_out = grp["pool"](pd, l)
        x = _ffn(x, (a_out, b_out.reshape(nseq * length, WIDTH), c_out.reshape(nseq * length, WIDTH), d_out),
                 wo, g2, wu, wd, gf, l)
        return x, a_v, pb3, pc3, pd.reshape(nseq, length, WIDTH), b_state, c_state

    d0 = _pool_offsets(1)[-1]
    zero_hist = jnp.zeros((nb, d0, WIDTH), F32)
    prompt = dict(
        nseq=nb, length=seq, nbs=PROMPT_SEQS_PER_STEP, chunk=MIX_CHUNK, mm=mm_p, bias=bias_t,
        b_ssm=jnp.zeros((1, nb, HEADS, HEAD_DIM, HEAD_DIM), F32), c_wkv=jnp.zeros((1, nb, HEADS, HEAD_DIM, HEAD_DIM), F32),
        b_conv=jnp.zeros((1, nb, B_CONV - 1, B_QKV), F32), c_shift=jnp.zeros((1, nb, 1, C_PROJ), F32),
        state_layer=lambda l: 0,
        pool=lambda pd, l: _pool(zero_hist, pd, wbd, dscale, l, nseq=nb, rows=seq, stride=1, start=0))

    def sample_pool(pd, l):
        pd_t = jnp.transpose(pd.reshape(ns, dseq, WIDTH), (1, 0, 2)).reshape(dseq * ns, WIDTH)
        hist = jnp.transpose(state_d_pool[l], (1, 0, 2)).reshape(1, POOL_BUF * ns, WIDTH)
        out_t = _pool(hist, pd_t, wbd, dscale, l, nseq=1, rows=dseq * ns, stride=ns, start=PAST_LEN)
        return jnp.transpose(out_t.reshape(dseq, ns, WIDTH), (1, 0, 2)).reshape(ns * dseq, WIDTH)

    sample = dict(
        nseq=ns, length=dseq, nbs=SAMPLE_SEQS_PER_STEP, chunk=SUBLANES, mm=mm_s, bias=bias_s,
        b_ssm=state_b_ssm, c_wkv=state_c_wkv, b_conv=state_b_conv, c_shift=state_c_shift[:, :, None, :],
        state_layer=lambda l: l, pool=sample_pool)

    xp = x_prompt.reshape(nb * seq, D_MODEL)
    xs = x_sample.reshape(ns * dseq, D_MODEL)
    p_bc, p_cs, p_dp, s_av, s_bc, s_cs, s_dp = [], [], [], [], [], [], []
    p_bs = p_cw = s_bs = s_cw = None
    for l in range(DEPTH):
        xp, _, pb3, pc3, pd3, p_bs, p_cw = run_group(xp, l, prompt, p_bs, p_cw)
        p_bs = p_bs if l else p_bs[None]
        p_cw = p_cw if l else p_cw[None]
        p_bc.append(pb3[:, seq - (B_CONV - 1):, :B_QKV])
        p_cs.append(pc3[:, seq - 1])
        p_dp.append(pd3[:, seq - POOL_BUF:])

        xs, a_v, pb3, pc3, pd3, s_bs, s_cw = run_group(xs, l, sample, s_bs, s_cw)
        s_bs = s_bs if l else s_bs[None]
        s_cw = s_cw if l else s_cw[None]
        s_av.append(a_v.reshape(ns, dseq, WIDTH))
        s_bc.append(jnp.concatenate([state_b_conv[l], pb3[:, :, :B_QKV]], axis=1)[:, -(B_CONV - 1):])
        s_cs.append(pc3[:, dseq - 1])
        s_dp.append(jnp.concatenate([state_d_pool[l], pd3], axis=1)[:, -POOL_BUF:])

    return (xp.reshape(nb, seq, D_MODEL), xs.reshape(ns, dseq, D_MODEL),
            jnp.stack(p_bc), p_bs, jnp.stack(p_cs), p_cw, jnp.stack(p_dp),
            jnp.stack(s_av), jnp.stack(s_bc), s_bs, jnp.stack(s_cs), s_cw, jnp.stack(s_dp))
```

```python
import functools

import jax
import jax.numpy as jnp
from jax import lax
from jax.experimental import pallas as pl
from jax.experimental.pallas import tpu as pltpu

F32 = jnp.float32
BF16 = jnp.bfloat16

D_MODEL = 1024
DEPTH = 2
HEADS = 4
HEAD_DIM = 64
WIDTH = HEADS * HEAD_DIM
B_QKV = 3 * WIDTH
B_CONV = 4
C_PROJ = 4 * WIDTH
POOL_WINDOWS = (2, 4, 8, 16)
POOL_BUF = 15
D_FF = 4 * D_MODEL
PAST_LEN = 16384
NORM_EPS = 1e-6
L2_EPS = 1e-6
GN_EPS = 64e-5
GATE_LANES = 128
SUBLANES = 8
VMEM_LIMIT = 48 * 1024 * 1024
MIX_CHUNK = 64
PROMPT_SEQS_PER_STEP = 8
SAMPLE_SEQS_PER_STEP = 32
ROW_TILE = 512


def _rms(x, g):
    return x * lax.rsqrt(jnp.mean(x * x, axis=-1, keepdims=True) + NORM_EPS) * g


def _softplus(x):
    return jnp.maximum(x, 0.0) + jnp.log1p(jnp.exp(-jnp.abs(x)))


def _dot(a, b):
    return jnp.dot(a, b, preferred_element_type=F32)


def _bmm(a, b):
    return jnp.einsum('bij,bjk->bik', a, b, preferred_element_type=F32)


def _bmm_nt(a, b):
    return jnp.einsum('bik,bjk->bij', a, b, preferred_element_type=F32)


def _bmm_tn(a, b):
    return jnp.einsum('bki,bkj->bij', a, b, preferred_element_type=F32)


def _layer_block(arr, l, single_buffer=False):
    shape = arr.shape[1:]
    index = lambda *_: (l,) + (0,) * len(shape)
    if single_buffer:
        return pl.BlockSpec((None,) + shape, index, pipeline_mode=pl.Buffered(1))
    return pl.BlockSpec((None,) + shape, index)


def _whole(arr):
    return pl.BlockSpec(arr.shape, lambda *_: (0,) * arr.ndim)


def _split_heads(x, lane0):
    return jnp.stack([x[b, :, lane0 + h * HEAD_DIM:lane0 + (h + 1) * HEAD_DIM]
                      for b in range(x.shape[0]) for h in range(HEADS)])


def _merge_heads(x, nbs):
    return jnp.concatenate([jnp.concatenate([x[b * HEADS + h] for h in range(HEADS)], axis=1)
                            for b in range(nbs)], axis=0)


def _group_sum(x, bd):
    hi = x.astype(BF16)
    lo = (x - hi.astype(F32)).astype(BF16)
    return _dot(hi, bd) + _dot(lo, bd)


def _split3(x):
    p1 = x.astype(BF16)
    r1 = x - p1.astype(F32)
    p2 = r1.astype(BF16)
    return p1, p2, (r1 - p2.astype(F32)).astype(BF16)


def _cumsum_rows(tril_b, x):
    w = x.shape[1]
    y = _dot(tril_b, jnp.concatenate(_split3(x), axis=1))
    return y[:, :w] + (y[:, w:2 * w] + y[:, 2 * w:])


def _unit_lower_inverse(lm, row, col, n_valid):
    def sub_diag_block(shift):
        return (((row >> (shift + 1)) == (col >> (shift + 1)))
                & (((row >> shift) & 1) == 1) & (((col >> shift) & 1) == 0))

    m = (row == col).astype(F32) - jnp.where(sub_diag_block(0), lm, 0.0)
    shift = 1
    while (1 << shift) < n_valid:
        cs = jnp.where(sub_diag_block(shift), lm, 0.0).astype(BF16)
        mb = m.astype(BF16)
        m = m - _bmm(_bmm(mb, cs).astype(BF16), mb)
        shift += 1
    return m


def _transpose_rows(a):
    n = a.shape[0]
    if n < GATE_LANES:
        a = jnp.concatenate([a, jnp.zeros((GATE_LANES - n, a.shape[1]), a.dtype)], axis=0)
    return a.T[:, :n]


def _stage_rows(ref, stage, in_rows):
    if stage is None:
        return ref[...]
    stage[...] = jnp.zeros(stage.shape, stage.dtype)
    stage[:, 0:in_rows, :] = ref[...]
    return stage[...]


def _inproj_kernel(x_ref, g_ref, wa_ref, wb_ref, wc_ref, wd_ref, wg_ref,
                   pa_ref, pb_ref, pc_ref, pd_ref, pg_ref):
    h = _rms(x_ref[...], g_ref[...]).astype(BF16)
    for w_ref, o_ref in ((wa_ref, pa_ref), (wb_ref, pb_ref), (wc_ref, pc_ref), (wd_ref, pd_ref),
                         (wg_ref, pg_ref)):
        o_ref[...] = _dot(h, w_ref[...])


def _inproj(x, g, ws, l):
    t = x.shape[0]
    tm = min(ROW_TILE, t)
    widths = [w.shape[2] for w in ws]
    return pl.pallas_call(
        _inproj_kernel,
        grid=(t // tm,),
        in_specs=[pl.BlockSpec((tm, D_MODEL), lambda i: (i, 0)), _layer_block(g, l)]
        + [_layer_block(w, l, single_buffer=True) for w in ws],
        out_specs=[pl.BlockSpec((tm, n), lambda i: (i, 0)) for n in widths],
        out_shape=[jax.ShapeDtypeStruct((t, n), F32) for n in widths],
        compiler_params=pltpu.CompilerParams(dimension_semantics=("parallel",),
                                             vmem_limit_bytes=VMEM_LIMIT),
        name="inproj",
    )(x, g, *ws)


def _gmlp_kernel(pa_ref, mm_ref, bias_ref, vg_ref, out_ref, v_ref, *, r):
    pa = pa_ref[...]
    u = jax.nn.gelu(pa[:, :WIDTH])
    v = _rms(jax.nn.gelu(pa[:, WIDTH:]), vg_ref[...])
    v_ref[...] = v
    lane_head = lax.broadcasted_iota(jnp.int32, (r, WIDTH), 1) >> 6
    mm = mm_ref[...]
    bias = bias_ref[...]
    for i in range(pa.shape[0] // r):
        vc = v[i * r:(i + 1) * r]
        per_head = jnp.concatenate([jnp.where(lane_head == h, vc, 0.0).astype(BF16)
                                    for h in range(HEADS)], axis=0)
        out_ref[i * r:(i + 1) * r, :] = u[i * r:(i + 1) * r] * (_dot(mm, per_head) + bias)


def _gmlp(pa, mm, bias, vg, l):
    t = pa.shape[0]
    r = mm.shape[1]
    tile = max(r, min(ROW_TILE, t))
    return pl.pallas_call(
        functools.partial(_gmlp_kernel, r=r),
        grid=(t // tile,),
        in_specs=[pl.BlockSpec((tile, 2 * WIDTH), lambda i: (i, 0)),
                  _layer_block(mm, l), _layer_block(bias, l), _layer_block(vg, l)],
        out_specs=[pl.BlockSpec((tile, WIDTH), lambda i: (i, 0))] * 2,
        out_shape=[jax.ShapeDtypeStruct((t, WIDTH), F32)] * 2,
        compiler_params=pltpu.CompilerParams(dimension_semantics=("parallel",)),
        name="gmlp",
    )(pa, mm, bias, vg)


def _store_state(sout_ref, prev_ref, new_state):
    if prev_ref is None:
        sout_ref[...] = new_state
    else:
        n_prev = prev_ref.shape[0]
        sout_ref[0:n_prev] = prev_ref[...]
        sout_ref[n_prev] = new_state


def _recurrent_call(kernel_fn, name, x_blocks, s0, s0_layer, carry_in, consts, prev, *,
                    nbs, chunk, carry_width, out_width=WIDTH):
    nseq, length, _ = x_blocks[0].shape
    in_rows = min(chunk, length)
    n_chunks = length // in_rows
    blk = lambda b, c: (b, c, 0)
    state_block = (nbs, HEADS, HEAD_DIM, HEAD_DIM)
    in_specs = [pl.BlockSpec((nbs, in_rows, x.shape[2]), blk) for x in x_blocks]
    in_specs += [pl.BlockSpec((None,) + state_block, lambda b, c: (s0_layer, b, 0, 0, 0)),
                 pl.BlockSpec((None, nbs) + carry_in.shape[2:], lambda b, c: (s0_layer, b, 0, 0))]
    in_specs += [_whole(a) if l is None else _layer_block(a, l) for a, l in consts]
    operands = list(x_blocks) + [s0, carry_in] + [a for a, _ in consts]
    if prev is None:
        state_shape = (nseq, HEADS, HEAD_DIM, HEAD_DIM)
        state_spec = pl.BlockSpec(state_block, lambda b, c: (b, 0, 0, 0))
    else:
        n_prev = prev.shape[0]
        in_specs.append(pl.BlockSpec((n_prev,) + state_block, lambda b, c: (0, b, 0, 0, 0)))
        operands.append(prev)
        state_shape = (n_prev + 1, nseq, HEADS, HEAD_DIM, HEAD_DIM)
        state_spec = pl.BlockSpec((n_prev + 1,) + state_block, lambda b, c: (0, b, 0, 0, 0))
    scratch = [pltpu.VMEM((nbs * HEADS, HEAD_DIM, HEAD_DIM), F32),
               pltpu.VMEM((nbs, chunk + SUBLANES, carry_width), F32)]
    if in_rows < chunk:
        scratch += [pltpu.VMEM((nbs, chunk, x.shape[2]), F32) for x in x_blocks]
    return pl.pallas_call(
        functools.partial(kernel_fn, nbs=nbs, chunk=chunk, in_rows=in_rows, n_x=len(x_blocks),
                          n_consts=len(consts), has_prev=prev is not None, carry=n_chunks > 1),
        grid=(nseq // nbs, n_chunks),
        in_specs=in_specs,
        out_specs=[pl.BlockSpec((nbs, in_rows, out_width), blk), state_spec],
        out_shape=[jax.ShapeDtypeStruct((nseq, length, out_width), F32),
                   jax.ShapeDtypeStruct(state_shape, F32)],
        scratch_shapes=scratch,
        compiler_params=pltpu.CompilerParams(dimension_semantics=("parallel", "arbitrary")),
        name=name,
    )(*operands)


def _unpack_refs(refs, n_x, n_consts, has_prev, staged):
    x_refs = refs[:n_x]
    s0_ref, carry_ref = refs[n_x:n_x + 2]
    consts = refs[n_x + 2:n_x + 2 + n_consts]
    pos = n_x + 2 + n_consts
    prev_ref = refs[pos] if has_prev else None
    pos += int(has_prev)
    o_ref, sout_ref, state_sc, rows_sc = refs[pos:pos + 4]
    stages = refs[pos + 4:] if staged else (None,) * n_x
    return x_refs, s0_ref, carry_ref, consts, prev_ref, o_ref, sout_ref, state_sc, rows_sc, stages


def _gdn_kernel(*refs, nbs, chunk, in_rows, n_x, n_consts, has_prev, carry):
    ((pb_ref, pg_ref), s0_ref, cbuf_ref, (cw_ref, vec_ref, bd_ref, eb_ref, eg_ref), prev_ref,
     o_ref, sout_ref, s_sc, xbuf, (stage_b, stage_g)) = _unpack_refs(refs, n_x, n_consts, has_prev,
                                                                    in_rows < chunk)
    c = pl.program_id(1)
    nb = nbs * HEADS
    rows = nbs * chunk
    n_hist = B_CONV - 1

    @pl.when(c == 0)
    def _():
        s_sc[...] = s0_ref[...].reshape(nb, HEAD_DIM, HEAD_DIM)
        xbuf[:, SUBLANES - n_hist:SUBLANES, :] = cbuf_ref[...]

    pb = _stage_rows(pb_ref, stage_b, in_rows)
    pg = _stage_rows(pg_ref, stage_g, in_rows)
    raw = pb[:, :, 0:B_QKV]
    xbuf[:, SUBLANES:SUBLANES + chunk, :] = raw
    cw = cw_ref[...]
    conv = xbuf[:, SUBLANES - 3:SUBLANES - 3 + chunk, :] * cw[0:1]
    conv = conv + xbuf[:, SUBLANES - 2:SUBLANES - 2 + chunk, :] * cw[1:2]
    conv = conv + xbuf[:, SUBLANES - 1:SUBLANES - 1 + chunk, :] * cw[2:3]
    conv = conv + raw * cw[3:4]
    if carry:
        xbuf[:, 0:SUBLANES, :] = xbuf[:, chunk:chunk + SUBLANES, :]
    qkv = jax.nn.silu(conv).reshape(rows, B_QKV)
    gate = jax.nn.silu(pb[:, :, B_QKV:B_QKV + WIDTH]).reshape(rows, WIDTH)

    row = lax.broadcasted_iota(jnp.int32, (chunk, chunk), 0)
    col = lax.broadcasted_iota(jnp.int32, (chunk, chunk), 1)
    tril = row >= col
    strict = row > col
    trilb = tril.astype(BF16)

    vec = vec_ref[...]
    alog = vec[:, 0:GATE_LANES]
    dtb = vec[:, GATE_LANES:2 * GATE_LANES]
    og = vec[:, 2 * GATE_LANES:2 * GATE_LANES + WIDTH]
    beta_all = jax.nn.sigmoid(pg)
    g_all = -jnp.exp(alog) * _softplus(pg + dtb)
    if in_rows < chunk:
        valid = lax.broadcasted_iota(jnp.int32, pg.shape, 1) < in_rows
        beta_all = jnp.where(valid, beta_all, 0.0)
        g_all = jnp.where(valid, g_all, 0.0)

    bd = bd_ref[...]
    qf = qkv[:, 0:WIDTH]
    kf = qkv[:, WIDTH:2 * WIDTH]
    vf = qkv[:, 2 * WIDTH:]
    qn = qf * lax.rsqrt(_dot((qf * qf).astype(BF16), bd) + L2_EPS) * (HEAD_DIM ** -0.5)
    kn = kf * lax.rsqrt(_dot((kf * kf).astype(BF16), bd) + L2_EPS)
    b1, b2, _ = _split3(beta_all.reshape(rows, GATE_LANES))
    beta_f = _dot(jnp.concatenate([b1, b2], axis=1), eb_ref[...])
    gc_small = [_cumsum_rows(trilb, g_all[b]) for b in range(nbs)]
    gc = _dot(jnp.concatenate(_split3(jnp.concatenate(gc_small, axis=0)), axis=1),
              eg_ref[...]).reshape(nbs, chunk, WIDTH)
    glast = gc[:, chunk - 1:chunk, :]
    e_g = jnp.exp(gc).reshape(rows, WIDTH)
    e_gl = jnp.exp(glast - gc).reshape(rows, WIDTH)
    bk = beta_f * kn

    def heads(x):
        return _split_heads(x.reshape(nbs, chunk, WIDTH).astype(BF16), 0)

    qkk = _bmm_nt(jnp.concatenate([heads(qn), heads(bk)], axis=1), heads(kn))
    grow_all = [_transpose_rows(g) for g in gc_small]
    gcol = jnp.stack([gc[b, :, h * HEAD_DIM:h * HEAD_DIM + 1] for b in range(nbs) for h in range(HEADS)])
    grow = jnp.stack([grow_all[b][HEADS + h:HEADS + h + 1, :] for b in range(nbs) for h in range(HEADS)])
    decay = jnp.where(tril, jnp.exp(jnp.minimum(gcol - grow, 0.0)), 0.0)
    qk = qkk[:, :chunk] * decay
    lm = jnp.where(strict, qkk[:, chunk:] * decay, 0.0)
    tinv = _unit_lower_inverse(lm, row, col, in_rows)
    rhs = jnp.concatenate([heads(beta_f * vf), heads(bk * e_g)], axis=2)
    uw = _bmm(tinv.astype(BF16), rhs)
    u = uw[:, :, :HEAD_DIM]
    wk = uw[:, :, HEAD_DIM:]
    s = s_sc[...]
    ws = _bmm(jnp.concatenate([wk.astype(BF16), heads(qn * e_g)], axis=1),
              s.astype(BF16))
    wnb = (u - ws[:, :chunk]).astype(BF16)
    o = ws[:, chunk:] + _bmm(qk.astype(BF16), wnb)
    s_sc[...] = _split_heads(jnp.exp(glast), 0) * s + _bmm_tn(heads(kn * e_gl), wnb)
    of = _merge_heads(o, nbs)
    of = of * lax.rsqrt(_group_sum(of * of, bd) * (1.0 / HEAD_DIM) + NORM_EPS) * og * gate
    o_ref[...] = of.reshape(nbs, chunk, WIDTH)[:, 0:in_rows, :]

    @pl.when(c == pl.num_programs(1) - 1)
    def _():
        _store_state(sout_ref, prev_ref, s_sc[...].reshape(nbs, HEADS, HEAD_DIM, HEAD_DIM))


def _rwkv_kernel(*refs, nbs, chunk, in_rows, n_x, n_consts, has_prev, carry):
    ((pc_ref,), s0_ref, sh_ref, (vec_ref, w2_ref, a2_ref, g2_ref, bd_ref), prev_ref,
     o_ref, sout_ref, st_sc, xs, (stage_c,)) = _unpack_refs(refs, n_x, n_consts, has_prev, in_rows < chunk)
    c = pl.program_id(1)
    rows = nbs * chunk

    @pl.when(c == 0)
    def _():
        for b in range(nbs):
            for h in range(HEADS):
                st_sc[b * HEADS + h] = s0_ref[b, h].T
        xs[:, SUBLANES - 1:SUBLANES, :] = sh_ref[...]

    cp = _stage_rows(pc_ref, stage_c, in_rows)
    xs[:, SUBLANES:SUBLANES + chunk, :] = cp
    prev = xs[:, SUBLANES - 1:SUBLANES - 1 + chunk, :]
    if carry:
        xs[:, 0:SUBLANES, :] = xs[:, chunk:chunk + SUBLANES, :]
    vec = vec_ref[...]
    mu = vec[:, 0:C_PROJ]
    w0, a0, k_k, k_a, r_k, ln_g, ln_b = (vec[:, C_PROJ + i * WIDTH:C_PROJ + (i + 1) * WIDTH] for i in range(7))
    xm = (cp + (prev - cp) * mu).reshape(rows, C_PROJ)
    r = xm[:, 0:WIDTH]
    k = xm[:, WIDTH:2 * WIDTH]
    v = xm[:, 2 * WIDTH:3 * WIDTH]
    wl = xm[:, 3 * WIDTH:3 * WIDTH + 64]
    al = xm[:, 3 * WIDTH + 64:3 * WIDTH + 128]
    gl = xm[:, 3 * WIDTH + 128:4 * WIDTH]
    w_log = -_softplus(-(w0 + _dot(jnp.tanh(wl).astype(BF16), w2_ref[...]))) - 0.5
    logw = -jnp.exp(w_log)
    a = jax.nn.sigmoid(a0 + _dot(al.astype(BF16), a2_ref[...]))
    gg = _dot(jax.nn.sigmoid(gl).astype(BF16), g2_ref[...])
    kkraw = k * k_k
    k2 = k * (1.0 + (a - 1.0) * k_a)
    rkk = r * k2 * r_k
    if in_rows < chunk:
        t_in_chunk = lax.broadcasted_iota(jnp.int32, (nbs, chunk, WIDTH), 1).reshape(rows, WIDTH)
        valid = t_in_chunk < in_rows
        logw = jnp.where(valid, logw, 0.0)
        kkraw = jnp.where(valid, kkraw, 0.0)
        k2 = jnp.where(valid, k2, 0.0)

    row = lax.broadcasted_iota(jnp.int32, (chunk, chunk), 0)
    col = lax.broadcasted_iota(jnp.int32, (chunk, chunk), 1)
    tril = row >= col
    strict = row > col
    trilb = tril.astype(BF16)

    def heads(x):
        return _split_heads(x.reshape(nbs, chunk, WIDTH).astype(BF16), 0)

    bd = bd_ref[...]
    kkn = kkraw * lax.rsqrt(_dot((kkraw * kkraw).astype(BF16), bd) + L2_EPS)
    bf = kkn * a
    logw3 = logw.reshape(nbs, chunk, WIDTH)
    glog3 = jnp.stack([_cumsum_rows(trilb, logw3[b]) for b in range(nbs)])
    glast = glog3[:, chunk - 1:chunk, :]
    glog = glog3.reshape(rows, WIDTH)
    e_n = jnp.exp(-glog)
    e_l = jnp.exp(glast - glog3).reshape(rows, WIDTH)
    lhs = jnp.concatenate([heads(kkn * jnp.exp(glog - logw)), heads(r * jnp.exp(glog))], axis=1)
    rhs = jnp.concatenate([heads(bf * e_n), heads(k2 * e_n)], axis=1)
    pair = _bmm_nt(lhs, rhs)
    st = st_sc[...]
    ls = _bmm_nt(lhs, st.astype(BF16))
    lb = jnp.where(strict, pair[:, :chunk, :chunk], 0.0)
    lk = jnp.where(strict, pair[:, :chunk, chunk:], 0.0)
    arb = jnp.where(tril, pair[:, chunk:, :chunk], 0.0)
    ark = jnp.where(tril, pair[:, chunk:, chunk:], 0.0)
    tinv = _unit_lower_inverse(lb, row, col, in_rows)
    vhb = heads(v)
    ub = _bmm(tinv.astype(BF16), (ls[:, :chunk] + _bmm(lk.astype(BF16), vhb)).astype(BF16)).astype(BF16)
    y = ls[:, chunk:] + _bmm(jnp.concatenate([ark, -arb], axis=2).astype(BF16),
                             jnp.concatenate([vhb, ub], axis=1))
    upd = _bmm_tn(jnp.concatenate([vhb, -ub], axis=1),
                  jnp.concatenate([heads(k2 * e_l), heads(bf * e_l)], axis=1))
    st_sc[...] = _split_heads(jnp.exp(glast), 0) * st + upd
    yf = _merge_heads(y, nbs)
    dev = yf - _group_sum(yf, bd) * (1.0 / HEAD_DIM)
    var = _group_sum(dev * dev, bd) * (1.0 / HEAD_DIM)
    yn = dev * lax.rsqrt(var + GN_EPS) * ln_g + ln_b
    out = (yn + _group_sum(rkk, bd) * v) * gg
    o_ref[...] = out.reshape(nbs, chunk, WIDTH)[:, 0:in_rows, :]

    @pl.when(c == pl.num_programs(1) - 1)
    def _():
        _store_state(sout_ref, prev_ref,
                     jnp.stack([jnp.stack([st_sc[b * HEADS + h].T for h in range(HEADS)])
                                for b in range(nbs)]))


def _round_up(x, m):
    return (x + m - 1) // m * m


def _pool_offsets(stride):
    offs = []
    a = 0
    for k in range(len(POOL_WINDOWS)):
        a = _round_up(a + (1 << k) * stride, SUBLANES)
        offs.append(a)
    return offs


def _pool_kernel(hist_ref, dp_ref, wbd_ref, scale_ref, out_ref, s1, s2, *, rows, stride, start):
    offs = _pool_offsets(stride)
    d0 = offs[-1]
    n = d0 + rows
    dp = dp_ref[...]
    s1[0:d0, :] = hist_ref[0]
    s1[d0:n, :] = dp
    lane = lax.broadcasted_iota(jnp.int32, (1, WIDTH), 1)
    src, dst = s1, s2
    for k, a in enumerate(offs):
        sh = (1 << k) * stride
        shifted = jnp.where(lane >= k * HEAD_DIM, src[a - sh:n - sh, :], 0.0)
        if k < len(offs) - 1:
            dst[a:n, :] = src[a:n, :] + shifted
            src, dst = dst, src
        else:
            sums = src[a:n, :] + shifted
    assert stride & (stride - 1) == 0
    pos = start + (lax.broadcasted_iota(jnp.int32, (rows, WIDTH), 0) >> (stride.bit_length() - 1))
    window = jnp.left_shift(2, lax.broadcasted_iota(jnp.int32, (rows, WIDTH), 1) >> 6)
    cnt = jnp.minimum(pos + 1, window).astype(F32)
    diff = sums / cnt - dp
    out_ref[...] = _dot(diff.astype(BF16), wbd_ref[...]) * scale_ref[...]


def _pool(hist, dp, wbd, scale, l, *, nseq, rows, stride, start):
    d0 = _pool_offsets(stride)[-1]
    return pl.pallas_call(
        functools.partial(_pool_kernel, rows=rows, stride=stride, start=start),
        grid=(nseq,),
        in_specs=[pl.BlockSpec((1, d0, WIDTH), lambda b: (b, 0, 0)),
                  pl.BlockSpec((rows, WIDTH), lambda b: (b, 0)),
                  _layer_block(wbd, l), _layer_block(scale, l)],
        out_specs=pl.BlockSpec((rows, WIDTH), lambda b: (b, 0)),
        out_shape=jax.ShapeDtypeStruct((nseq * rows, WIDTH), F32),
        scratch_shapes=[pltpu.VMEM((d0 + rows, WIDTH), F32)] * 2,
        compiler_params=pltpu.CompilerParams(dimension_semantics=("parallel",)),
        name="pool",
    )(hist, dp, wbd, scale)


def _ffn_kernel(x_ref, ma_ref, mb_ref, mc_ref, md_ref, wo_ref, g2_ref, wu_ref, wd_ref, gf_ref,
                o_ref, *, tf, final):
    mixed = None
    for i, m_ref in enumerate((ma_ref, mb_ref, mc_ref, md_ref)):
        part = _dot(m_ref[...].astype(BF16), wo_ref[i * WIDTH:(i + 1) * WIDTH, :])
        mixed = part if mixed is None else mixed + part
    x = x_ref[...] + mixed
    hm = _rms(x, g2_ref[...]).astype(BF16)
    down = None
    for j in range(D_FF // tf):
        up = jnp.maximum(_dot(hm, wu_ref[:, j * tf:(j + 1) * tf]), 0.0)
        part = _dot((up * up).astype(BF16), wd_ref[j * tf:(j + 1) * tf, :])
        down = part if down is None else down + part
    x = x + down
    if final:
        x = _rms(x, gf_ref[...])
    o_ref[...] = x


def _ffn(x, mixed, wo, g2, wu, wd, gf, l):
    t = x.shape[0]
    tm = min(ROW_TILE, t)
    row = lambda i: (i, 0)
    return pl.pallas_call(
        functools.partial(_ffn_kernel, tf=1024, final=l == DEPTH - 1),
        grid=(t // tm,),
        in_specs=[pl.BlockSpec((tm, D_MODEL), row)] + [pl.BlockSpec((tm, WIDTH), row)] * 4
        + [_layer_block(wo, l, single_buffer=True), _layer_block(g2, l),
           _layer_block(wu, l, single_buffer=True), _layer_block(wd, l, single_buffer=True), _whole(gf)],
        out_specs=pl.BlockSpec((tm, D_MODEL), row),
        out_shape=jax.ShapeDtypeStruct((t, D_MODEL), F32),
        compiler_params=pltpu.CompilerParams(dimension_semantics=("parallel",),
                                             vmem_limit_bytes=VMEM_LIMIT),
        name="ffn",
    )(x, *mixed, wo, g2, wu, wd, gf)


def _rows(v):
    return v[:, None, :]


def kernel(x_prompt, x_sample, state_b_conv, state_b_ssm, state_c_shift, state_c_wkv, state_d_pool, norm1_g, w_in, a_ws, a_bs, a_vnorm_g, b_conv_w, b_a_log, b_dt_bias, b_onorm_g, c_mu, c_w0, c_w2, c_a0, c_a2, c_g2, c_k_k, c_k_a, c_r_k, c_ln_g, c_ln_b, d_w, d_scale, w_out, norm2_g, w_up, w_down, final_g):
    nb, seq, _ = x_prompt.shape
    ns, dseq, _ = x_sample.shape
    a_chunk = a_ws.shape[-1]

    a_end = 2 * WIDTH
    b_end = a_end + 4 * WIDTH
    c_off = b_end + 2 * HEADS
    d_off = c_off + C_PROJ
    w_in_b = w_in.astype(BF16)
    w_cols = [w_in_b[:, :, :a_end], w_in_b[:, :, a_end:b_end], w_in_b[:, :, c_off:d_off], w_in_b[:, :, d_off:],
              jnp.pad(w_in_b[:, :, b_end:c_off], ((0, 0), (0, 0), (0, GATE_LANES - 2 * HEADS)))]
    wo = w_out.astype(BF16)
    wu = w_up.astype(BF16)
    wd = w_down.astype(BF16)
    g1 = _rows(norm1_g)
    g2 = _rows(norm2_g)
    gf = final_g[None, :]
    vg = _rows(a_vnorm_g)
    causal = jnp.tril(jnp.ones((a_chunk, a_chunk), dtype=bool))
    wm = jnp.where(causal, a_ws, 0.0)
    bias_t = jnp.repeat(jnp.transpose(a_bs, (0, 2, 1)), HEAD_DIM, axis=2)
    mm_p = jnp.transpose(wm, (0, 2, 1, 3)).reshape(DEPTH, a_chunk, HEADS * a_chunk).astype(BF16)
    srow = jnp.arange(ns * dseq)
    step_onehot = (srow[:, None] % dseq == jnp.arange(dseq)[None, :]).astype(F32)
    same_seq = (srow[:, None] // dseq) == (srow[None, :] // dseq)
    mm_s = jnp.einsum('it,lhts,js->lihj', step_onehot, wm[:, :, :dseq, :dseq], step_onehot,
                      precision=lax.Precision.HIGHEST)
    mm_s = jnp.where(same_seq[None, :, None, :], mm_s, 0.0).reshape(
        DEPTH, ns * dseq, HEADS * ns * dseq).astype(BF16)
    bias_s = jnp.tile(bias_t[:, :dseq], (1, ns, 1))
    lead = jnp.zeros((DEPTH, HEADS), F32)
    tail = jnp.zeros((DEPTH, GATE_LANES - 2 * HEADS), F32)
    gvec = jnp.concatenate([lead, b_a_log, tail, lead, b_dt_bias, tail, jnp.tile(b_onorm_g, (1, HEADS))],
                           axis=1)[:, None, :]
    cvec = jnp.concatenate([c_mu, c_w0, c_a0, c_k_k, c_k_a, c_r_k.reshape(DEPTH, WIDTH), c_ln_g, c_ln_b],
                           axis=1)[:, None, :]
    c_w2b, c_a2b, c_g2b = c_w2.astype(BF16), c_a2.astype(BF16), c_g2.astype(BF16)
    n_groups = len(POOL_WINDOWS)
    wbd = jnp.einsum('lgcd,gh->lgchd', d_w, jnp.eye(n_groups, dtype=F32)).reshape(
        DEPTH, WIDTH, WIDTH).astype(BF16)
    dscale = _rows(d_scale)
    lane_head = jnp.arange(WIDTH) // HEAD_DIM
    gate_lane = jnp.arange(GATE_LANES)
    bd = (lane_head[:, None] == lane_head[None, :]).astype(BF16)
    eb = jnp.tile((gate_lane[:, None] == lane_head[None, :]).astype(BF16), (2, 1))
    eg = jnp.tile((gate_lane[:, None] == lane_head[None, :] + HEADS).astype(BF16), (3, 1))

    def run_group(x, l, grp, prev_b, prev_c):
        nseq, length = grp["nseq"], grp["length"]
        pa, pb, pc, pd, pg = _inproj(x, g1, w_cols, l)
        a_out, a_v = _gmlp(pa, grp["mm"], grp["bias"], vg, l)
        pb3 = pb.reshape(nseq, length, 4 * WIDTH)
        pc3 = pc.reshape(nseq, length, C_PROJ)
        b_out, b_state = _recurrent_call(
            _gdn_kernel, "gdn", [pb3, pg.reshape(nseq, length, GATE_LANES)], grp["b_ssm"], grp["state_layer"](l),
            grp["b_conv"], [(b_conv_w, l), (gvec, l), (bd, None), (eb, None), (eg, None)], prev_b,
            nbs=grp["nbs"], chunk=grp["chunk"], carry_width=B_QKV)
        c_out, c_state = _recurrent_call(
            _rwkv_kernel, "rwkv", [pc3], grp["c_wkv"], grp["state_layer"](l), grp["c_shift"],
            [(cvec, l), (c_w2b, l), (c_a2b, l), (c_g2b, l), (bd, None)], prev_c,
            nbs=grp["nbs"], chunk=grp["chunk"], carry_width=C_PROJ)
        d_out = grp["pool"](pd, l)
        x = _ffn(x, (a_out, b_out.reshape(nseq * length, WIDTH), c_out.reshape(nseq * length, WIDTH), d_out),
                 wo, g2, wu, wd, gf, l)
        return x, a_v, pb3, pc3, pd.reshape(nseq, length, WIDTH), b_state, c_state

    d0 = _pool_offsets(1)[-1]
    zero_hist = jnp.zeros((nb, d0, WIDTH), F32)
    prompt = dict(
        nseq=nb, length=seq, nbs=PROMPT_SEQS_PER_STEP, chunk=MIX_CHUNK, mm=mm_p, bias=bias_t,
        b_ssm=jnp.zeros((1, nb, HEADS, HEAD_DIM, HEAD_DIM), F32), c_wkv=jnp.zeros((1, nb, HEADS, HEAD_DIM, HEAD_DIM), F32),
        b_conv=jnp.zeros((1, nb, B_CONV - 1, B_QKV), F32), c_shift=jnp.zeros((1, nb, 1, C_PROJ), F32),
        state_layer=lambda l: 0,
        pool=lambda pd, l: _pool(zero_hist, pd, wbd, dscale, l, nseq=nb, rows=seq, stride=1, start=0))

    def sample_pool(pd, l):
        pd_t = jnp.transpose(pd.reshape(ns, dseq, WIDTH), (1, 0, 2)).reshape(dseq * ns, WIDTH)
        hist = jnp.transpose(state_d_pool[l], (1, 0, 2)).reshape(1, POOL_BUF * ns, WIDTH)
        out_t = _pool(hist, pd_t, wbd, dscale, l, nseq=1, rows=dseq * ns, stride=ns, start=PAST_LEN)
        return jnp.transpose(out_t.reshape(dseq, ns, WIDTH), (1, 0, 2)).reshape(ns * dseq, WIDTH)

    sample = dict(
        nseq=ns, length=dseq, nbs=SAMPLE_SEQS_PER_STEP, chunk=SUBLANES, mm=mm_s, bias=bias_s,
        b_ssm=state_b_ssm, c_wkv=state_c_wkv, b_conv=state_b_conv, c_shift=state_c_shift[:, :, None, :],
        state_layer=lambda l: l, pool=sample_pool)

    xp = x_prompt.reshape(nb * seq, D_MODEL)
    xs = x_sample.reshape(ns * dseq, D_MODEL)
    p_bc, p_cs, p_dp, s_av, s_bc, s_cs, s_dp = [], [], [], [], [], [], []
    p_bs = p_cw = s_bs = s_cw = None
    for l in range(DEPTH):
        xp, _, pb3, pc3, pd3, p_bs, p_cw = run_group(xp, l, prompt, p_bs, p_cw)
        p_bs = p_bs if l else p_bs[None]
        p_cw = p_cw if l else p_cw[None]
        p_bc.append(pb3[:, seq - (B_CONV - 1):, :B_QKV])
        p_cs.append(pc3[:, seq - 1])
        p_dp.append(pd3[:, seq - POOL_BUF:])

        xs, a_v, pb3, pc3, pd3, s_bs, s_cw = run_group(xs, l, sample, s_bs, s_cw)
        s_bs = s_bs if l else s_bs[None]
        s_cw = s_cw if l else s_cw[None]
        s_av.append(a_v.reshape(ns, dseq, WIDTH))
        s_bc.append(jnp.concatenate([state_b_conv[l], pb3[:, :, :B_QKV]], axis=1)[:, -(B_CONV - 1):])
        s_cs.append(pc3[:, dseq - 1])
        s_dp.append(jnp.concatenate([state_d_pool[l], pd3], axis=1)[:, -POOL_BUF:])

    return (xp.reshape(nb, seq, D_MODEL), xs.reshape(ns, dseq, D_MODEL),
            jnp.stack(p_bc), p_bs, jnp.stack(p_cs), p_cw, jnp.stack(p_dp),
            jnp.stack(s_av), jnp.stack(s_bc), s_bs, jnp.stack(s_cs), s_cw, jnp.stack(s_dp))
```

```python
import functools

import jax
import jax.numpy as jnp
from jax import lax
from jax.experimental import pallas as pl
from jax.experimental.pallas import tpu as pltpu

F32 = jnp.float32
BF16 = jnp.bfloat16

D_MODEL = 1024
DEPTH = 2
HEADS = 4
HEAD_DIM = 64
WIDTH = HEADS * HEAD_DIM
B_QKV = 3 * WIDTH
B_CONV = 4
C_PROJ = 4 * WIDTH
POOL_WINDOWS = (2, 4, 8, 16)
POOL_BUF = 15
D_FF = 4 * D_MODEL
PAST_LEN = 16384
NORM_EPS = 1e-6
L2_EPS = 1e-6
GN_EPS = 64e-5
GATE_LANES = 128
SUBLANES = 8
VMEM_LIMIT = 48 * 1024 * 1024
MIX_CHUNK = 64
PROMPT_SEQS_PER_STEP = 8
SAMPLE_SEQS_PER_STEP = 32
ROW_TILE = 512


def _rms(x, g):
    return x * lax.rsqrt(jnp.mean(x * x, axis=-1, keepdims=True) + NORM_EPS) * g


def _softplus(x):
    return jnp.maximum(x, 0.0) + jnp.log1p(jnp.exp(-jnp.abs(x)))


def _dot(a, b):
    return jnp.dot(a, b, preferred_element_type=F32)


def _bmm(a, b):
    return jnp.einsum('bij,bjk->bik', a, b, preferred_element_type=F32)


def _bmm_nt(a, b):
    return jnp.einsum('bik,bjk->bij', a, b, preferred_element_type=F32)


def _bmm_tn(a, b):
    return jnp.einsum('bki,bkj->bij', a, b, preferred_element_type=F32)


def _layer_block(arr, l, single_buffer=False):
    shape = arr.shape[1:]
    index = lambda *_: (l,) + (0,) * len(shape)
    if single_buffer:
        return pl.BlockSpec((None,) + shape, index, pipeline_mode=pl.Buffered(1))
    return pl.BlockSpec((None,) + shape, index)


def _whole(arr):
    return pl.BlockSpec(arr.shape, lambda *_: (0,) * arr.ndim)


def _split_heads(x, lane0):
    return jnp.stack([x[b, :, lane0 + h * HEAD_DIM:lane0 + (h + 1) * HEAD_DIM]
                      for b in range(x.shape[0]) for h in range(HEADS)])


def _merge_heads(x, nbs):
    return jnp.concatenate([jnp.concatenate([x[b * HEADS + h] for h in range(HEADS)], axis=1)
                            for b in range(nbs)], axis=0)


def _group_sum(x, bd):
    hi = x.astype(BF16)
    lo = (x - hi.astype(F32)).astype(BF16)
    return _dot(hi, bd) + _dot(lo, bd)


def _split3(x):
    p1 = x.astype(BF16)
    r1 = x - p1.astype(F32)
    p2 = r1.astype(BF16)
    return p1, p2, (r1 - p2.astype(F32)).astype(BF16)


def _cumsum_rows(tril_b, x):
    w = x.shape[1]
    y = _dot(tril_b, jnp.concatenate(_split3(x), axis=1))
    return y[:, :w] + (y[:, w:2 * w] + y[:, 2 * w:])


def _unit_lower_inverse(lm, row, col, n_valid):
    def sub_diag_block(shift):
        return (((row >> (shift + 1)) == (col >> (shift + 1)))
                & (((row >> shift) & 1) == 1) & (((col >> shift) & 1) == 0))

    m = (row == col).astype(F32) - jnp.where(sub_diag_block(0), lm, 0.0)
    shift = 1
    while (1 << shift) < n_valid:
        cs = jnp.where(sub_diag_block(shift), lm, 0.0).astype(BF16)
        mb = m.astype(BF16)
        m = m - _bmm(_bmm(mb, cs).astype(BF16), mb)
        shift += 1
    return m


def _transpose_rows(a):
    n = a.shape[0]
    if n < GATE_LANES:
        a = jnp.concatenate([a, jnp.zeros((GATE_LANES - n, a.shape[1]), a.dtype)], axis=0)
    return a.T[:, :n]


def _stage_rows(ref, stage, in_rows):
    if stage is None:
        return ref[...]
    stage[...] = jnp.zeros(stage.shape, stage.dtype)
    stage[:, 0:in_rows, :] = ref[...]
    return stage[...]


def _inproj_kernel(x_ref, g_ref, wa_ref, wb_ref, wc_ref, wd_ref, wg_ref, mm_ref, bias_ref, vg_ref,
                   a_ref, v_ref, pb_ref, pc_ref, pd_ref, pg_ref, *, r):
    h = _rms(x_ref[...], g_ref[...]).astype(BF16)
    pa = _dot(h, wa_ref[...])
    pb_ref[...] = _dot(h, wb_ref[...])
    u = jax.nn.gelu(pa[:, :WIDTH])
    v = _rms(jax.nn.gelu(pa[:, WIDTH:]), vg_ref[...])
    v_ref[...] = v
    pc_ref[...] = _dot(h, wc_ref[...])
    pd_ref[...] = _dot(h, wd_ref[...])
    pg_ref[...] = _dot(h, wg_ref[...])
    lane_head = lax.broadcasted_iota(jnp.int32, (r, WIDTH), 1) >> 6
    mm = mm_ref[...]
    bias = bias_ref[...]
    for i in range(pa.shape[0] // r):
        vc = v[i * r:(i + 1) * r]
        per_head = jnp.concatenate([jnp.where(lane_head == h, vc, 0.0).astype(BF16)
                                    for h in range(HEADS)], axis=0)
        a_ref[i * r:(i + 1) * r, :] = u[i * r:(i + 1) * r] * (_dot(mm, per_head) + bias)


def _inproj(x, g, ws, mm, bias, vg, l):
    t = x.shape[0]
    r = mm.shape[1]
    tm = max(r, min(ROW_TILE, t))
    widths = [WIDTH, WIDTH] + [w.shape[2] for w in ws[1:]]
    return pl.pallas_call(
        functools.partial(_inproj_kernel, r=r),
        grid=(t // tm,),
        in_specs=[pl.BlockSpec((tm, D_MODEL), lambda i: (i, 0)), _layer_block(g, l)]
        + [_layer_block(w, l, single_buffer=True) for w in ws]
        + [_layer_block(mm, l), _layer_block(bias, l), _layer_block(vg, l)],
        out_specs=[pl.BlockSpec((tm, n), lambda i: (i, 0)) for n in widths],
        out_shape=[jax.ShapeDtypeStruct((t, n), F32) for n in widths],
        compiler_params=pltpu.CompilerParams(dimension_semantics=("parallel",),
                                             vmem_limit_bytes=VMEM_LIMIT),
        name="inproj",
    )(x, g, *ws, mm, bias, vg)


def _store_state(sout_ref, prev_ref, new_state):
    if prev_ref is None:
        sout_ref[...] = new_state
    else:
        n_prev = prev_ref.shape[0]
        sout_ref[0:n_prev] = prev_ref[...]
        sout_ref[n_prev] = new_state


def _recurrent_call(kernel_fn, name, x_blocks, s0, s0_layer, carry_in, consts, prev, *,
                    nbs, chunk, carry_width, out_width=WIDTH):
    nseq, length, _ = x_blocks[0].shape
    in_rows = min(chunk, length)
    n_chunks = length // in_rows
    blk = lambda b, c: (b, c, 0)
    state_block = (nbs, HEADS, HEAD_DIM, HEAD_DIM)
    in_specs = [pl.BlockSpec((nbs, in_rows, x.shape[2]), blk) for x in x_blocks]
    in_specs += [pl.BlockSpec((None,) + state_block, lambda b, c: (s0_layer, b, 0, 0, 0)),
                 pl.BlockSpec((None, nbs) + carry_in.shape[2:], lambda b, c: (s0_layer, b, 0, 0))]
    in_specs += [_whole(a) if l is None else _layer_block(a, l) for a, l in consts]
    operands = list(x_blocks) + [s0, carry_in] + [a for a, _ in consts]
    if prev is None:
        state_shape = (nseq, HEADS, HEAD_DIM, HEAD_DIM)
        state_spec = pl.BlockSpec(state_block, lambda b, c: (b, 0, 0, 0))
    else:
        n_prev = prev.shape[0]
        in_specs.append(pl.BlockSpec((n_prev,) + state_block, lambda b, c: (0, b, 0, 0, 0)))
        operands.append(prev)
        state_shape = (n_prev + 1, nseq, HEADS, HEAD_DIM, HEAD_DIM)
        state_spec = pl.BlockSpec((n_prev + 1,) + state_block, lambda b, c: (0, b, 0, 0, 0))
    scratch = [pltpu.VMEM((nbs * HEADS, HEAD_DIM, HEAD_DIM), F32),
               pltpu.VMEM((nbs, chunk + SUBLANES, carry_width), F32)]
    if in_rows < chunk:
        scratch += [pltpu.VMEM((nbs, chunk, x.shape[2]), F32) for x in x_blocks]
    return pl.pallas_call(
        functools.partial(kernel_fn, nbs=nbs, chunk=chunk, in_rows=in_rows, n_x=len(x_blocks),
                          n_consts=len(consts), has_prev=prev is not None, carry=n_chunks > 1),
        grid=(nseq // nbs, n_chunks),
        in_specs=in_specs,
        out_specs=[pl.BlockSpec((nbs, in_rows, out_width), blk), state_spec],
        out_shape=[jax.ShapeDtypeStruct((nseq, length, out_width), F32),
                   jax.ShapeDtypeStruct(state_shape, F32)],
        scratch_shapes=scratch,
        compiler_params=pltpu.CompilerParams(dimension_semantics=("parallel", "arbitrary")),
        name=name,
    )(*operands)


def _unpack_refs(refs, n_x, n_consts, has_prev, staged):
    x_refs = refs[:n_x]
    s0_ref, carry_ref = refs[n_x:n_x + 2]
    consts = refs[n_x + 2:n_x + 2 + n_consts]
    pos = n_x + 2 + n_consts
    prev_ref = refs[pos] if has_prev else None
    pos += int(has_prev)
    o_ref, sout_ref, state_sc, rows_sc = refs[pos:pos + 4]
    stages = refs[pos + 4:] if staged else (None,) * n_x
    return x_refs, s0_ref, carry_ref, consts, prev_ref, o_ref, sout_ref, state_sc, rows_sc, stages


def _gdn_kernel(*refs, nbs, chunk, in_rows, n_x, n_consts, has_prev, carry):
    ((pb_ref, pg_ref), s0_ref, cbuf_ref, (cw_ref, vec_ref, bd_ref, eb_ref, eg_ref), prev_ref,
     o_ref, sout_ref, s_sc, xbuf, (stage_b, stage_g)) = _unpack_refs(refs, n_x, n_consts, has_prev,
                                                                    in_rows < chunk)
    c = pl.program_id(1)
    nb = nbs * HEADS
    rows = nbs * chunk
    n_hist = B_CONV - 1

    @pl.when(c == 0)
    def _():
        s_sc[...] = s0_ref[...].reshape(nb, HEAD_DIM, HEAD_DIM)
        xbuf[:, 0:SUBLANES, :] = jnp.zeros((nbs, SUBLANES, B_QKV), F32)
        xbuf[:, SUBLANES - n_hist:SUBLANES, :] = cbuf_ref[...]

    pb = _stage_rows(pb_ref, stage_b, in_rows)
    pg = _stage_rows(pg_ref, stage_g, in_rows)
    raw = pb[:, :, 0:B_QKV]
    xbuf[:, SUBLANES:SUBLANES + chunk, :] = raw
    cw = cw_ref[...]
    xfull = xbuf[...]
    conv = pltpu.roll(xfull, 3, axis=1)[:, SUBLANES:, :] * cw[0:1]
    conv = conv + pltpu.roll(xfull, 2, axis=1)[:, SUBLANES:, :] * cw[1:2]
    conv = conv + pltpu.roll(xfull, 1, axis=1)[:, SUBLANES:, :] * cw[2:3]
    conv = conv + raw * cw[3:4]
    if carry:
        xbuf[:, 0:SUBLANES, :] = xbuf[:, chunk:chunk + SUBLANES, :]
    qkv = jax.nn.silu(conv).reshape(rows, B_QKV)
    gate = jax.nn.silu(pb[:, :, B_QKV:B_QKV + WIDTH]).reshape(rows, WIDTH)

    row = lax.broadcasted_iota(jnp.int32, (chunk, chunk), 0)
    col = lax.broadcasted_iota(jnp.int32, (chunk, chunk), 1)
    tril = row >= col
    strict = row > col
    trilb = tril.astype(BF16)

    vec = vec_ref[...]
    alog = vec[:, 0:GATE_LANES]
    dtb = vec[:, GATE_LANES:2 * GATE_LANES]
    og = vec[:, 2 * GATE_LANES:2 * GATE_LANES + WIDTH]
    beta_all = jax.nn.sigmoid(pg)
    g_all = -jnp.exp(alog) * _softplus(pg + dtb)
    if in_rows < chunk:
        valid = lax.broadcasted_iota(jnp.int32, pg.shape, 1) < in_rows
        beta_all = jnp.where(valid, beta_all, 0.0)
        g_all = jnp.where(valid, g_all, 0.0)

    bd = bd_ref[...]
    qf = qkv[:, 0:WIDTH]
    kf = qkv[:, WIDTH:2 * WIDTH]
    vf = qkv[:, 2 * WIDTH:]
    qn = qf * lax.rsqrt(_dot((qf * qf).astype(BF16), bd) + L2_EPS) * (HEAD_DIM ** -0.5)
    kn = kf * lax.rsqrt(_dot((kf * kf).astype(BF16), bd) + L2_EPS)
    b1, b2, _ = _split3(beta_all.reshape(rows, GATE_LANES))
    beta_f = _dot(jnp.concatenate([b1, b2], axis=1), eb_ref[...])
    gc_small = [_cumsum_rows(trilb, g_all[b]) for b in range(nbs)]
    gc = _dot(jnp.concatenate(_split3(jnp.concatenate(gc_small, axis=0)), axis=1),
              eg_ref[...]).reshape(nbs, chunk, WIDTH)
    glast = gc[:, chunk - 1:chunk, :]
    e_g = jnp.exp(gc).reshape(rows, WIDTH)
    e_gl = jnp.exp(glast - gc).reshape(rows, WIDTH)
    bk = beta_f * kn

    def heads(x):
        return _split_heads(x.reshape(nbs, chunk, WIDTH).astype(BF16), 0)

    qkk = _bmm_nt(jnp.concatenate([heads(qn), heads(bk)], axis=1), heads(kn))
    grow_all = [_transpose_rows(g) for g in gc_small]
    gcol = jnp.stack([gc[b, :, h * HEAD_DIM:h * HEAD_DIM + 1] for b in range(nbs) for h in range(HEADS)])
    grow = jnp.stack([grow_all[b][HEADS + h:HEADS + h + 1, :] for b in range(nbs) for h in range(HEADS)])
    decay = jnp.where(tril, jnp.exp(jnp.minimum(gcol - grow, 0.0)), 0.0)
    qk = qkk[:, :chunk] * decay
    lm = jnp.where(strict, qkk[:, chunk:] * decay, 0.0)
    tinv = _unit_lower_inverse(lm, row, col, in_rows)
    rhs = jnp.concatenate([heads(beta_f * vf), heads(bk * e_g)], axis=2)
    uw = _bmm(tinv.astype(BF16), rhs)
    u = uw[:, :, :HEAD_DIM]
    wk = uw[:, :, HEAD_DIM:]
    s = s_sc[...]
    ws = _bmm(jnp.concatenate([wk.astype(BF16), heads(qn * e_g)], axis=1),
              s.astype(BF16))
    wnb = (u - ws[:, :chunk]).astype(BF16)
    o = ws[:, chunk:] + _bmm(qk.astype(BF16), wnb)
    s_sc[...] = _split_heads(jnp.exp(glast), 0) * s + _bmm_tn(heads(kn * e_gl), wnb)
    of = _merge_heads(o, nbs)
    of = of * lax.rsqrt(_group_sum(of * of, bd) * (1.0 / HEAD_DIM) + NORM_EPS) * og * gate
    o_ref[...] = of.reshape(nbs, chunk, WIDTH)[:, 0:in_rows, :]

    @pl.when(c == pl.num_programs(1) - 1)
    def _():
        _store_state(sout_ref, prev_ref, s_sc[...].reshape(nbs, HEADS, HEAD_DIM, HEAD_DIM))


def _rwkv_kernel(*refs, nbs, chunk, in_rows, n_x, n_consts, has_prev, carry):
    ((pc_ref,), s0_ref, sh_ref, (vec_ref, w2_ref, a2_ref, g2_ref, bd_ref), prev_ref,
     o_ref, sout_ref, st_sc, xs, (stage_c,)) = _unpack_refs(refs, n_x, n_consts, has_prev, in_rows < chunk)
    c = pl.program_id(1)
    rows = nbs * chunk

    @pl.when(c == 0)
    def _():
        for b in range(nbs):
            for h in range(HEADS):
                st_sc[b * HEADS + h] = s0_ref[b, h].T
        xs[:, SUBLANES - 1:SUBLANES, :] = sh_ref[...]

    cp = _stage_rows(pc_ref, stage_c, in_rows)
    xs[:, SUBLANES:SUBLANES + chunk, :] = cp
    prev = xs[:, SUBLANES - 1:SUBLANES - 1 + chunk, :]
    if carry:
        xs[:, 0:SUBLANES, :] = xs[:, chunk:chunk + SUBLANES, :]
    vec = vec_ref[...]
    mu = vec[:, 0:C_PROJ]
    w0, a0, k_k, k_a, r_k, ln_g, ln_b = (vec[:, C_PROJ + i * WIDTH:C_PROJ + (i + 1) * WIDTH] for i in range(7))
    xm = (cp + (prev - cp) * mu).reshape(rows, C_PROJ)
    r = xm[:, 0:WIDTH]
    k = xm[:, WIDTH:2 * WIDTH]
    v = xm[:, 2 * WIDTH:3 * WIDTH]
    wl = xm[:, 3 * WIDTH:3 * WIDTH + 64]
    al = xm[:, 3 * WIDTH + 64:3 * WIDTH + 128]
    gl = xm[:, 3 * WIDTH + 128:4 * WIDTH]
    w_log = -_softplus(-(w0 + _dot(jnp.tanh(wl).astype(BF16), w2_ref[...]))) - 0.5
    logw = -jnp.exp(w_log)
    a = jax.nn.sigmoid(a0 + _dot(al.astype(BF16), a2_ref[...]))
    gg = _dot(jax.nn.sigmoid(gl).astype(BF16), g2_ref[...])
    kkraw = k * k_k
    k2 = k * (1.0 + (a - 1.0) * k_a)
    rkk = r * k2 * r_k
    if in_rows < chunk:
        t_in_chunk = lax.broadcasted_iota(jnp.int32, (nbs, chunk, WIDTH), 1).reshape(rows, WIDTH)
        valid = t_in_chunk < in_rows
        logw = jnp.where(valid, logw, 0.0)
        kkraw = jnp.where(valid, kkraw, 0.0)
        k2 = jnp.where(valid, k2, 0.0)

    row = lax.broadcasted_iota(jnp.int32, (chunk, chunk), 0)
    col = lax.broadcasted_iota(jnp.int32, (chunk, chunk), 1)
    tril = row >= col
    strict = row > col
    trilb = tril.astype(BF16)

    def heads(x):
        return _split_heads(x.reshape(nbs, chunk, WIDTH).astype(BF16), 0)

    bd = bd_ref[...]
    kkn = kkraw * lax.rsqrt(_dot((kkraw * kkraw).astype(BF16), bd) + L2_EPS)
    bf = kkn * a
    logw3 = logw.reshape(nbs, chunk, WIDTH)
    glog3 = jnp.stack([_cumsum_rows(trilb, logw3[b]) for b in range(nbs)])
    glast = glog3[:, chunk - 1:chunk, :]
    glog = glog3.reshape(rows, WIDTH)
    e_n = jnp.exp(-glog)
    e_l = jnp.exp(glast - glog3).reshape(rows, WIDTH)
    lhs = jnp.concatenate([heads(kkn * jnp.exp(glog - logw)), heads(r * jnp.exp(glog))], axis=1)
    rhs = jnp.concatenate([heads(bf * e_n), heads(k2 * e_n)], axis=1)
    pair = _bmm_nt(lhs, rhs)
    st = st_sc[...]
    ls = _bmm_nt(lhs, st.astype(BF16))
    lb = jnp.where(strict, pair[:, :chunk, :chunk], 0.0)
    lk = jnp.where(strict, pair[:, :chunk, chunk:], 0.0)
    arb = jnp.where(tril, pair[:, chunk:, :chunk], 0.0)
    ark = jnp.where(tril, pair[:, chunk:, chunk:], 0.0)
    tinv = _unit_lower_inverse(lb, row, col, in_rows)
    vhb = heads(v)
    ub = _bmm(tinv.astype(BF16), (ls[:, :chunk] + _bmm(lk.astype(BF16), vhb)).astype(BF16)).astype(BF16)
    y = ls[:, chunk:] + _bmm(jnp.concatenate([ark, -arb], axis=2).astype(BF16),
                             jnp.concatenate([vhb, ub], axis=1))
    upd = _bmm_tn(jnp.concatenate([vhb, -ub], axis=1),
                  jnp.concatenate([heads(k2 * e_l), heads(bf * e_l)], axis=1))
    st_sc[...] = _split_heads(jnp.exp(glast), 0) * st + upd
    yf = _merge_heads(y, nbs)
    dev = yf - _group_sum(yf, bd) * (1.0 / HEAD_DIM)
    var = _group_sum(dev * dev, bd) * (1.0 / HEAD_DIM)
    yn = dev * lax.rsqrt(var + GN_EPS) * ln_g + ln_b
    out = (yn + _group_sum(rkk, bd) * v) * gg
    o_ref[...] = out.reshape(nbs, chunk, WIDTH)[:, 0:in_rows, :]

    @pl.when(c == pl.num_programs(1) - 1)
    def _():
        _store_state(sout_ref, prev_ref,
                     jnp.stack([jnp.stack([st_sc[b * HEADS + h].T for h in range(HEADS)])
                                for b in range(nbs)]))


def _round_up(x, m):
    return (x + m - 1) // m * m


def _pool_offsets(stride):
    offs = []
    a = 0
    for k in range(len(POOL_WINDOWS)):
        a = _round_up(a + (1 << k) * stride, SUBLANES)
        offs.append(a)
    return offs


def _pool_kernel(hist_ref, dp_ref, wbd_ref, scale_ref, out_ref, s1, s2, *, rows, stride, start):
    offs = _pool_offsets(stride)
    d0 = offs[-1]
    n = d0 + rows
    dp = dp_ref[...]
    s1[0:d0, :] = hist_ref[0]
    s1[d0:n, :] = dp
    lane = lax.broadcasted_iota(jnp.int32, (1, WIDTH), 1)
    src, dst = s1, s2
    for k, a in enumerate(offs):
        sh = (1 << k) * stride
        shifted = jnp.where(lane >= k * HEAD_DIM, src[a - sh:n - sh, :], 0.0)
        if k < len(offs) - 1:
            dst[a:n, :] = src[a:n, :] + shifted
            src, dst = dst, src
        else:
            sums = src[a:n, :] + shifted
    assert stride & (stride - 1) == 0
    pos = start + (lax.broadcasted_iota(jnp.int32, (rows, WIDTH), 0) >> (stride.bit_length() - 1))
    window = jnp.left_shift(2, lax.broadcasted_iota(jnp.int32, (rows, WIDTH), 1) >> 6)
    cnt = jnp.minimum(pos + 1, window).astype(F32)
    diff = sums / cnt - dp
    out_ref[...] = _dot(diff.astype(BF16), wbd_ref[...]) * scale_ref[...]


def _pool(hist, dp, wbd, scale, l, *, nseq, rows, stride, start):
    d0 = _pool_offsets(stride)[-1]
    return pl.pallas_call(
        functools.partial(_pool_kernel, rows=rows, stride=stride, start=start),
        grid=(nseq,),
        in_specs=[pl.BlockSpec((1, d0, WIDTH), lambda b: (b, 0, 0)),
                  pl.BlockSpec((rows, WIDTH), lambda b: (b, 0)),
                  _layer_block(wbd, l), _layer_block(scale, l)],
        out_specs=pl.BlockSpec((rows, WIDTH), lambda b: (b, 0)),
        out_shape=jax.ShapeDtypeStruct((nseq * rows, WIDTH), F32),
        scratch_shapes=[pltpu.VMEM((d0 + rows, WIDTH), F32)] * 2,
        compiler_params=pltpu.CompilerParams(dimension_semantics=("parallel",)),
        name="pool",
    )(hist, dp, wbd, scale)


def _ffn_kernel(x_ref, ma_ref, mb_ref, mc_ref, md_ref, wo_ref, g2_ref, wu_ref, wd_ref, gf_ref,
                o_ref, *, tf, final):
    mixed = None
    for i, m_ref in enumerate((ma_ref, mb_ref, mc_ref, md_ref)):
        part = _dot(m_ref[...].astype(BF16), wo_ref[i * WIDTH:(i + 1) * WIDTH, :])
        mixed = part if mixed is None else mixed + part
    x = x_ref[...] + mixed
    hm = _rms(x, g2_ref[...]).astype(BF16)
    down = None
    for j in range(D_FF // tf):
        up = jnp.maximum(_dot(hm, wu_ref[:, j * tf:(j + 1) * tf]), 0.0)
        part = _dot((up * up).astype(BF16), wd_ref[j * tf:(j + 1) * tf, :])
        down = part if down is None else down + part
    x = x + down
    if final:
        x = _rms(x, gf_ref[...])
    o_ref[...] = x


def _ffn(x, mixed, wo, g2, wu, wd, gf, l):
    t = x.shape[0]
    tm = min(ROW_TILE, t)
    row = lambda i: (i, 0)
    return pl.pallas_call(
        functools.partial(_ffn_kernel, tf=1024, final=l == DEPTH - 1),
        grid=(t // tm,),
        in_specs=[pl.BlockSpec((tm, D_MODEL), row)] + [pl.BlockSpec((tm, WIDTH), row)] * 4
        + [_layer_block(wo, l, single_buffer=True), _layer_block(g2, l),
           _layer_block(wu, l, single_buffer=True), _layer_block(wd, l, single_buffer=True), _whole(gf)],
        out_specs=pl.BlockSpec((tm, D_MODEL), row),
        out_shape=jax.ShapeDtypeStruct((t, D_MODEL), F32),
        compiler_params=pltpu.CompilerParams(dimension_semantics=("parallel",),
                                             vmem_limit_bytes=VMEM_LIMIT),
        name="ffn",
    )(x, *mixed, wo, g2, wu, wd, gf)


def _rows(v):
    return v[:, None, :]


def kernel(x_prompt, x_sample, state_b_conv, state_b_ssm, state_c_shift, state_c_wkv, state_d_pool, norm1_g, w_in, a_ws, a_bs, a_vnorm_g, b_conv_w, b_a_log, b_dt_bias, b_onorm_g, c_mu, c_w0, c_w2, c_a0, c_a2, c_g2, c_k_k, c_k_a, c_r_k, c_ln_g, c_ln_b, d_w, d_scale, w_out, norm2_g, w_up, w_down, final_g):
    nb, seq, _ = x_prompt.shape
    ns, dseq, _ = x_sample.shape
    a_chunk = a_ws.shape[-1]

    a_end = 2 * WIDTH
    b_end = a_end + 4 * WIDTH
    c_off = b_end + 2 * HEADS
    d_off = c_off + C_PROJ
    w_in_b = w_in.astype(BF16)
    w_cols = [w_in_b[:, :, :a_end], w_in_b[:, :, a_end:b_end], w_in_b[:, :, c_off:d_off], w_in_b[:, :, d_off:],
              jnp.pad(w_in_b[:, :, b_end:c_off], ((0, 0), (0, 0), (0, GATE_LANES - 2 * HEADS)))]
    wo = w_out.astype(BF16)
    wu = w_up.astype(BF16)
    wd = w_down.astype(BF16)
    g1 = _rows(norm1_g)
    g2 = _rows(norm2_g)
    gf = final_g[None, :]
    vg = _rows(a_vnorm_g)
    causal = jnp.tril(jnp.ones((a_chunk, a_chunk), dtype=bool))
    wm = jnp.where(causal, a_ws, 0.0)
    bias_t = jnp.repeat(jnp.transpose(a_bs, (0, 2, 1)), HEAD_DIM, axis=2)
    mm_p = jnp.transpose(wm, (0, 2, 1, 3)).reshape(DEPTH, a_chunk, HEADS * a_chunk).astype(BF16)
    srow = jnp.arange(ns * dseq)
    step_onehot = (srow[:, None] % dseq == jnp.arange(dseq)[None, :]).astype(F32)
    same_seq = (srow[:, None] // dseq) == (srow[None, :] // dseq)
    mm_s = jnp.einsum('it,lhts,js->lihj', step_onehot, wm[:, :, :dseq, :dseq], step_onehot,
                      precision=lax.Precision.HIGHEST)
    mm_s = jnp.where(same_seq[None, :, None, :], mm_s, 0.0).reshape(
        DEPTH, ns * dseq, HEADS * ns * dseq).astype(BF16)
    bias_s = jnp.tile(bias_t[:, :dseq], (1, ns, 1))
    lead = jnp.zeros((DEPTH, HEADS), F32)
    tail = jnp.zeros((DEPTH, GATE_LANES - 2 * HEADS), F32)
    gvec = jnp.concatenate([lead, b_a_log, tail, lead, b_dt_bias, tail, jnp.tile(b_onorm_g, (1, HEADS))],
                           axis=1)[:, None, :]
    cvec = jnp.concatenate([c_mu, c_w0, c_a0, c_k_k, c_k_a, c_r_k.reshape(DEPTH, WIDTH), c_ln_g, c_ln_b],
                           axis=1)[:, None, :]
    c_w2b, c_a2b, c_g2b = c_w2.astype(BF16), c_a2.astype(BF16), c_g2.astype(BF16)
    n_groups = len(POOL_WINDOWS)
    wbd = jnp.einsum('lgcd,gh->lgchd', d_w, jnp.eye(n_groups, dtype=F32)).reshape(
        DEPTH, WIDTH, WIDTH).astype(BF16)
    dscale = _rows(d_scale)
    lane_head = jnp.arange(WIDTH) // HEAD_DIM
    gate_lane = jnp.arange(GATE_LANES)
    bd = (lane_head[:, None] == lane_head[None, :]).astype(BF16)
    eb = jnp.tile((gate_lane[:, None] == lane_head[None, :]).astype(BF16), (2, 1))
    eg = jnp.tile((gate_lane[:, None] == lane_head[None, :] + HEADS).astype(BF16), (3, 1))

    def run_group(x, l, grp, prev_b, prev_c):
        nseq, length = grp["nseq"], grp["length"]
        a_out, a_v, pb, pc, pd, pg = _inproj(x, g1, w_cols, grp["mm"], grp["bias"], vg, l)
        pb3 = pb.reshape(nseq, length, 4 * WIDTH)
        pc3 = pc.reshape(nseq, length, C_PROJ)
        b_out, b_state = _recurrent_call(
            _gdn_kernel, "gdn", [pb3, pg.reshape(nseq, length, GATE_LANES)], grp["b_ssm"], grp["state_layer"](l),
            grp["b_conv"], [(b_conv_w, l), (gvec, l), (bd, None), (eb, None), (eg, None)], prev_b,
            nbs=grp["nbs"], chunk=grp["chunk"], carry_width=B_QKV)
        c_out, c_state = _recurrent_call(
            _rwkv_kernel, "rwkv", [pc3], grp["c_wkv"], grp["state_layer"](l), grp["c_shift"],
            [(cvec, l), (c_w2b, l), (c_a2b, l), (c_g2b, l), (bd, None)], prev_c,
            nbs=grp["nbs"], chunk=grp["chunk"], carry_width=C_PROJ)
        d_out = grp["pool"](pd, l)
        x = _ffn(x, (a_out, b_out.reshape(nseq * length, WIDTH), c_out.reshape(nseq * length, WIDTH), d_out),
                 wo, g2, wu, wd, gf, l)
        return x, a_v, pb3, pc3, pd.reshape(nseq, length, WIDTH), b_state, c_state

    d0 = _pool_offsets(1)[-1]
    zero_hist = jnp.zeros((nb, d0, WIDTH), F32)
    prompt = dict(
        nseq=nb, length=seq, nbs=PROMPT_SEQS_PER_STEP, chunk=MIX_CHUNK, mm=mm_p, bias=bias_t,
        b_ssm=jnp.zeros((1, nb, HEADS, HEAD_DIM, HEAD_DIM), F32), c_wkv=jnp.zeros((1, nb, HEADS, HEAD_DIM, HEAD_DIM), F32),
        b_conv=jnp.zeros((1, nb, B_CONV - 1, B_QKV), F32), c_shift=jnp.zeros((1, nb, 1, C_PROJ), F32),
        state_layer=lambda l: 0,
        pool=lambda pd, l: _pool(zero_hist, pd, wbd, dscale, l, nseq=nb, rows=seq, stride=1, start=0))

    def sample_pool(pd, l):
        pd_t = jnp.transpose(pd.reshape(ns, dseq, WIDTH), (1, 0, 2)).reshape(dseq * ns, WIDTH)
        hist = jnp.transpose(state_d_pool[l], (1, 0, 2)).reshape(1, POOL_BUF * ns, WIDTH)
        out_t = _pool(hist, pd_t, wbd, dscale, l, nseq=1, rows=dseq * ns, stride=ns, start=PAST_LEN)
        return jnp.transpose(out_t.reshape(dseq, ns, WIDTH), (1, 0, 2)).reshape(ns * dseq, WIDTH)

    sample = dict(
        nseq=ns, length=dseq, nbs=SAMPLE_SEQS_PER_STEP, chunk=SUBLANES, mm=mm_s, bias=bias_s,
        b_ssm=state_b_ssm, c_wkv=state_c_wkv, b_conv=state_b_conv, c_shift=state_c_shift[:, :, None, :],
        state_layer=lambda l: l, pool=sample_pool)

    xp = x_prompt.reshape(nb * seq, D_MODEL)
    xs = x_sample.reshape(ns * dseq, D_MODEL)
    p_bc, p_cs, p_dp, s_av, s_bc, s_cs, s_dp = [], [], [], [], [], [], []
    p_bs = p_cw = s_bs = s_cw = None
    for l in range(DEPTH):
        xp, _, pb3, pc3, pd3, p_bs, p_cw = run_group(xp, l, prompt, p_bs, p_cw)
        p_bs = p_bs if l else p_bs[None]
        p_cw = p_cw if l else p_cw[None]
        p_bc.append(pb3[:, seq - (B_CONV - 1):, :B_QKV])
        p_cs.append(pc3[:, seq - 1])
        p_dp.append(pd3[:, seq - POOL_BUF:])

        xs, a_v, pb3, pc3, pd3, s_bs, s_cw = run_group(xs, l, sample, s_bs, s_cw)
        s_bs = s_bs if l else s_bs[None]
        s_cw = s_cw if l else s_cw[None]
        s_av.append(a_v.reshape(ns, dseq, WIDTH))
        s_bc.append(jnp.concatenate([state_b_conv[l], pb3[:, :, :B_QKV]], axis=1)[:, -(B_CONV - 1):])
        s_cs.append(pc3[:, dseq - 1])
        s_dp.append(jnp.concatenate([state_d_pool[l], pd3], axis=1)[:, -POOL_BUF:])

    return (xp.reshape(nb, seq, D_MODEL), xs.reshape(ns, dseq, D_MODEL),
            jnp.stack(p_bc), p_bs, jnp.stack(p_cs), p_cw, jnp.stack(p_dp),
            jnp.stack(s_av), jnp.stack(s_bc), s_bs, jnp.stack(s_cs), s_cw, jnp.stack(s_dp))
```

```python
import functools

import jax
import jax.numpy as jnp
from jax import lax
from jax.experimental import pallas as pl
from jax.experimental.pallas import tpu as pltpu

F32 = jnp.float32
BF16 = jnp.bfloat16

D_MODEL = 1024
DEPTH = 2
HEADS = 4
HEAD_DIM = 64
WIDTH = HEADS * HEAD_DIM
B_QKV = 3 * WIDTH
B_CONV = 4
C_PROJ = 4 * WIDTH
POOL_WINDOWS = (2, 4, 8, 16)
POOL_BUF = 15
D_FF = 4 * D_MODEL
PAST_LEN = 16384
NORM_EPS = 1e-6
L2_EPS = 1e-6
GN_EPS = 64e-5
GATE_LANES = 128
SUBLANES = 8
VMEM_LIMIT = 48 * 1024 * 1024
MIX_CHUNK = 64
PROMPT_SEQS_PER_STEP = 8
SAMPLE_SEQS_PER_STEP = 32
ROW_TILE = 512


def _rms(x, g):
    return x * lax.rsqrt(jnp.mean(x * x, axis=-1, keepdims=True) + NORM_EPS) * g


def _softplus(x):
    return jnp.maximum(x, 0.0) + jnp.log1p(jnp.exp(-jnp.abs(x)))


def _dot(a, b):
    return jnp.dot(a, b, preferred_element_type=F32)


def _bmm(a, b):
    return jnp.einsum('bij,bjk->bik', a, b, preferred_element_type=F32)


def _bmm_nt(a, b):
    return jnp.einsum('bik,bjk->bij', a, b, preferred_element_type=F32)


def _bmm_tn(a, b):
    return jnp.einsum('bki,bkj->bij', a, b, preferred_element_type=F32)


def _layer_block(arr, l, single_buffer=False):
    shape = arr.shape[1:]
    index = lambda *_: (l,) + (0,) * len(shape)
    if single_buffer:
        return pl.BlockSpec((None,) + shape, index, pipeline_mode=pl.Buffered(1))
    return pl.BlockSpec((None,) + shape, index)


def _whole(arr):
    return pl.BlockSpec(arr.shape, lambda *_: (0,) * arr.ndim)


def _split_heads(x, lane0):
    return jnp.stack([x[b, :, lane0 + h * HEAD_DIM:lane0 + (h + 1) * HEAD_DIM]
                      for b in range(x.shape[0]) for h in range(HEADS)])


def _merge_heads(x, nbs):
    return jnp.concatenate([jnp.concatenate([x[b * HEADS + h] for h in range(HEADS)], axis=1)
                            for b in range(nbs)], axis=0)


def _group_sum(x, bd):
    hi = x.astype(BF16)
    lo = (x - hi.astype(F32)).astype(BF16)
    return _dot(hi, bd) + _dot(lo, bd)


def _split3(x):
    p1 = x.astype(BF16)
    r1 = x - p1.astype(F32)
    p2 = r1.astype(BF16)
    return p1, p2, (r1 - p2.astype(F32)).astype(BF16)


def _cumsum_rows(tril_b, x):
    w = x.shape[1]
    y = _dot(tril_b, jnp.concatenate(_split3(x), axis=1))
    return y[:, :w] + (y[:, w:2 * w] + y[:, 2 * w:])


def _unit_lower_inverse(lm, row, col, n_valid):
    def sub_diag_block(shift):
        return (((row >> (shift + 1)) == (col >> (shift + 1)))
                & (((row >> shift) & 1) == 1) & (((col >> shift) & 1) == 0))

    m = (row == col).astype(F32) - jnp.where(sub_diag_block(0), lm, 0.0)
    shift = 1
    while (1 << shift) < n_valid:
        cs = jnp.where(sub_diag_block(shift), lm, 0.0).astype(BF16)
        mb = m.astype(BF16)
        m = m - _bmm(_bmm(mb, cs).astype(BF16), mb)
        shift += 1
    return m


def _transpose_rows(a):
    n = a.shape[0]
    if n < GATE_LANES:
        a = jnp.concatenate([a, jnp.zeros((GATE_LANES - n, a.shape[1]), a.dtype)], axis=0)
    return a.T[:, :n]


def _stage_rows(ref, stage, in_rows):
    if stage is None:
        return ref[...]
    stage[...] = jnp.zeros(stage.shape, stage.dtype)
    stage[:, 0:in_rows, :] = ref[...]
    return stage[...]


def _conv_silu(raw, hist, tail_ref, lo, cw):
    tm = raw.shape[0]
    cols = slice(lo, lo + WIDTH)
    xfull = jnp.concatenate([hist[:, cols], raw], axis=0)
    conv = pltpu.roll(xfull, 3, axis=0)[SUBLANES:, :] * cw[0:1, cols]
    conv = conv + pltpu.roll(xfull, 2, axis=0)[SUBLANES:, :] * cw[1:2, cols]
    conv = conv + pltpu.roll(xfull, 1, axis=0)[SUBLANES:, :] * cw[2:3, cols]
    conv = conv + raw * cw[3:4, cols]
    tail = raw[tm - SUBLANES:tm, :]
    hist[:, cols] = tail
    tail_ref[:, cols] = tail
    return jax.nn.silu(conv)


def _inproj_kernel(*refs, r, prep, tiles_per_seq):
    (x_ref, g_ref, wa_ref, wb_ref, wc_ref, wd_ref, wg_ref, mm_ref, bias_ref, vg_ref) = refs[:10]
    if prep:
        cw_ref, gvec_ref, bd_ref, cbuf_ref, cvec_ref, w2_ref, a2_ref, g2_ref, sh_ref = refs[10:19]
        a_ref, v_ref, pb_ref, pc_ref, pd_ref, pg_ref, tail_ref, tailc_ref, hist, hist_c = refs[19:]
    else:
        a_ref, v_ref, pb_ref, pc_ref, pd_ref, pg_ref = refs[10:]
    h = _rms(x_ref[...], g_ref[...]).astype(BF16)

    def gmlp_gates(pa):
        v = _rms(jax.nn.gelu(pa[:, WIDTH:]), vg_ref[...])
        v_ref[...] = v
        return jax.nn.gelu(pa[:, :WIDTH]), v

    def gmlp_mix(u, v):
        lane_head = lax.broadcasted_iota(jnp.int32, (r, WIDTH), 1) >> 6
        mm = mm_ref[...]
        bias = bias_ref[...]
        for i in range(u.shape[0] // r):
            vc = v[i * r:(i + 1) * r]
            per_head = jnp.concatenate([jnp.where(lane_head == h, vc, 0.0).astype(BF16)
                                        for h in range(HEADS)], axis=0)
            a_ref[i * r:(i + 1) * r, :] = u[i * r:(i + 1) * r] * (_dot(mm, per_head) + bias)

    if prep:
        n_hist = B_CONV - 1

        @pl.when(pl.program_id(0) % tiles_per_seq == 0)
        def _():
            hist[...] = jnp.zeros(hist.shape, F32)
            hist[SUBLANES - n_hist:SUBLANES, :] = cbuf_ref[0]
            hist_c[...] = jnp.zeros(hist_c.shape, F32)
            hist_c[SUBLANES - 1:SUBLANES, :] = sh_ref[0]

        cvec = cvec_ref[...]
        mu = cvec[:, 0:C_PROJ]
        w0, a0, k_k, k_a, r_k = (cvec[:, C_PROJ + i * WIDTH:C_PROJ + (i + 1) * WIDTH] for i in range(5))

        def shifted_lerp(cp, lo):
            cols = slice(lo, lo + WIDTH)
            prev = pltpu.roll(jnp.concatenate([hist_c[:, cols], cp], axis=0), 1, axis=0)[SUBLANES:, :]
            tail = cp[cp.shape[0] - SUBLANES:, :]
            hist_c[:, cols] = tail
            tailc_ref[:, cols] = tail
            return cp + (prev - cp) * mu[:, cols]

        cw = cw_ref[...]
        bd = bd_ref[...]
        gvec = gvec_ref[...]
        raw_q = _dot(h, wb_ref[:, 0:WIDTH])
        raw_k = _dot(h, wb_ref[:, WIDTH:2 * WIDTH])
        raw_v = _dot(h, wb_ref[:, 2 * WIDTH:3 * WIDTH])
        qf = _conv_silu(raw_q, hist, tail_ref, 0, cw)
        pb_ref[:, 0:WIDTH] = qf * lax.rsqrt(_dot((qf * qf).astype(BF16), bd) + L2_EPS) * (HEAD_DIM ** -0.5)
        raw_gate = _dot(h, wb_ref[:, 3 * WIDTH:4 * WIDTH])
        pg = _dot(h, wg_ref[...])
        kf = _conv_silu(raw_k, hist, tail_ref, WIDTH, cw)
        pb_ref[:, WIDTH:2 * WIDTH] = kf * lax.rsqrt(_dot((kf * kf).astype(BF16), bd) + L2_EPS)
        cp_l = _dot(h, wc_ref[:, 3 * WIDTH:4 * WIDTH])
        cp_k = _dot(h, wc_ref[:, WIDTH:2 * WIDTH])
        pb_ref[:, 2 * WIDTH:3 * WIDTH] = _conv_silu(raw_v, hist, tail_ref, 2 * WIDTH, cw)
        pa = _dot(h, wa_ref[...])
        pb_ref[:, 3 * WIDTH:4 * WIDTH] = jax.nn.silu(raw_gate)
        alog = gvec[:, 0:GATE_LANES]
        dtb = gvec[:, GATE_LANES:2 * GATE_LANES]
        lane = lax.broadcasted_iota(jnp.int32, pg.shape, 1)
        pg_ref[...] = jnp.where(lane < HEADS, jax.nn.sigmoid(pg), -jnp.exp(alog) * _softplus(pg + dtb))
        cp_r = _dot(h, wc_ref[:, 0:WIDTH])
        cp_v = _dot(h, wc_ref[:, 2 * WIDTH:3 * WIDTH])
        xm_l = shifted_lerp(cp_l, 3 * WIDTH)
        w_log = -_softplus(-(w0 + _dot(jnp.tanh(xm_l[:, 0:64]).astype(BF16), w2_ref[...]))) - 0.5
        pc_ref[:, 5 * WIDTH:6 * WIDTH] = -jnp.exp(w_log)
        a = jax.nn.sigmoid(a0 + _dot(xm_l[:, 64:128].astype(BF16), a2_ref[...]))
        pc_ref[:, 6 * WIDTH:7 * WIDTH] = _dot(jax.nn.sigmoid(xm_l[:, 128:256]).astype(BF16), g2_ref[...])
        pd_ref[...] = _dot(h, wd_ref[...])
        k = shifted_lerp(cp_k, WIDTH)
        kkraw = k * k_k
        kkn = kkraw * lax.rsqrt(_dot((kkraw * kkraw).astype(BF16), bd) + L2_EPS)
        k2 = k * (1.0 + (a - 1.0) * k_a)
        pc_ref[:, WIDTH:2 * WIDTH] = k2
        pc_ref[:, 3 * WIDTH:4 * WIDTH] = kkn
        pc_ref[:, 4 * WIDTH:5 * WIDTH] = kkn * a
        rr = shifted_lerp(cp_r, 0)
        vv = shifted_lerp(cp_v, 2 * WIDTH)
        pc_ref[:, 0:WIDTH] = rr
        pc_ref[:, 2 * WIDTH:3 * WIDTH] = vv
        pc_ref[:, 7 * WIDTH:8 * WIDTH] = _group_sum(rr * k2 * r_k, bd) * vv
        gmlp_mix(*gmlp_gates(pa))
    else:
        pa = _dot(h, wa_ref[...])
        pb_ref[...] = _dot(h, wb_ref[...])
        u, v = gmlp_gates(pa)
        pc_ref[...] = _dot(h, wc_ref[...])
        pd_ref[...] = _dot(h, wd_ref[...])
        pg_ref[...] = _dot(h, wg_ref[...])
        gmlp_mix(u, v)


def _inproj(x, g, ws, mm, bias, vg, l, gdn_prep=None):
    t = x.shape[0]
    r = mm.shape[1]
    tm = max(r, min(ROW_TILE, t))
    widths = [WIDTH, WIDTH] + [w.shape[2] for w in ws[1:]]
    in_specs = ([pl.BlockSpec((tm, D_MODEL), lambda i: (i, 0)), _layer_block(g, l)]
                + [_layer_block(w, l, single_buffer=True) for w in ws]
                + [_layer_block(mm, l), _layer_block(bias, l), _layer_block(vg, l)])
    operands = [x, g, *ws, mm, bias, vg]
    out_specs = [pl.BlockSpec((tm, n), lambda i: (i, 0)) for n in widths]
    out_shape = [jax.ShapeDtypeStruct((t, n), F32) for n in widths]
    scratch = []
    tiles_per_seq = 1
    if gdn_prep is not None:
        conv_w, gvec, bd, conv_state, seq_len, cvec, w2, a2, g2, shift_state = gdn_prep
        tiles_per_seq = seq_len // tm
        per_seq = lambda i: (i // tiles_per_seq, 0, 0)
        in_specs += [_layer_block(conv_w, l), _layer_block(gvec, l), _whole(bd),
                     pl.BlockSpec((1,) + conv_state.shape[1:], per_seq),
                     _layer_block(cvec, l), _layer_block(w2, l), _layer_block(a2, l), _layer_block(g2, l),
                     pl.BlockSpec((1,) + shift_state.shape[1:], per_seq)]
        operands += [conv_w, gvec, bd, conv_state, cvec, w2, a2, g2, shift_state]
        widths[3] = 8 * WIDTH
        out_specs[3] = pl.BlockSpec((tm, 8 * WIDTH), lambda i: (i, 0))
        out_shape[3] = jax.ShapeDtypeStruct((t, 8 * WIDTH), F32)
        for w in (B_QKV, C_PROJ):
            out_specs.append(pl.BlockSpec((SUBLANES, w), lambda i: (i, 0)))
            out_shape.append(jax.ShapeDtypeStruct((t // tm * SUBLANES, w), F32))
            scratch.append(pltpu.VMEM((SUBLANES, w), F32))
    return pl.pallas_call(
        functools.partial(_inproj_kernel, r=r, prep=gdn_prep is not None, tiles_per_seq=tiles_per_seq),
        grid=(t // tm,),
        in_specs=in_specs,
        out_specs=out_specs,
        out_shape=out_shape,
        scratch_shapes=scratch,
        compiler_params=pltpu.CompilerParams(
            dimension_semantics=("arbitrary" if gdn_prep is not None else "parallel",),
            vmem_limit_bytes=VMEM_LIMIT),
        name="inproj",
    )(*operands)


def _store_state(sout_ref, prev_ref, new_state):
    if prev_ref is None:
        sout_ref[...] = new_state
    else:
        n_prev = prev_ref.shape[0]
        sout_ref[0:n_prev] = prev_ref[...]
        sout_ref[n_prev] = new_state


def _recurrent_call(kernel_fn, name, x_blocks, s0, s0_layer, carry_in, consts, prev, *,
                    nbs, chunk, carry_width, out_width=WIDTH, **kernel_flags):
    nseq, length, _ = x_blocks[0].shape
    in_rows = min(chunk, length)
    n_chunks = length // in_rows
    blk = lambda b, c: (b, c, 0)
    state_block = (nbs, HEADS, HEAD_DIM, HEAD_DIM)
    in_specs = [pl.BlockSpec((nbs, in_rows, x.shape[2]), blk) for x in x_blocks]
    in_specs += [pl.BlockSpec((None,) + state_block, lambda b, c: (s0_layer, b, 0, 0, 0)),
                 pl.BlockSpec((None, nbs) + carry_in.shape[2:], lambda b, c: (s0_layer, b, 0, 0))]
    in_specs += [_whole(a) if l is None else _layer_block(a, l) for a, l in consts]
    operands = list(x_blocks) + [s0, carry_in] + [a for a, _ in consts]
    if prev is None:
        state_shape = (nseq, HEADS, HEAD_DIM, HEAD_DIM)
        state_spec = pl.BlockSpec(state_block, lambda b, c: (b, 0, 0, 0))
    else:
        n_prev = prev.shape[0]
        in_specs.append(pl.BlockSpec((n_prev,) + state_block, lambda b, c: (0, b, 0, 0, 0)))
        operands.append(prev)
        state_shape = (n_prev + 1, nseq, HEADS, HEAD_DIM, HEAD_DIM)
        state_spec = pl.BlockSpec((n_prev + 1,) + state_block, lambda b, c: (0, b, 0, 0, 0))
    scratch = [pltpu.VMEM((nbs * HEADS, HEAD_DIM, HEAD_DIM), F32),
               pltpu.VMEM((nbs, chunk + SUBLANES, carry_width), F32)]
    if in_rows < chunk:
        scratch += [pltpu.VMEM((nbs, chunk, x.shape[2]), F32) for x in x_blocks]
    return pl.pallas_call(
        functools.partial(kernel_fn, nbs=nbs, chunk=chunk, in_rows=in_rows, n_x=len(x_blocks),
                          n_consts=len(consts), has_prev=prev is not None, carry=n_chunks > 1,
                          **kernel_flags),
        grid=(nseq // nbs, n_chunks),
        in_specs=in_specs,
        out_specs=[pl.BlockSpec((nbs, in_rows, out_width), blk), state_spec],
        out_shape=[jax.ShapeDtypeStruct((nseq, length, out_width), F32),
                   jax.ShapeDtypeStruct(state_shape, F32)],
        scratch_shapes=scratch,
        compiler_params=pltpu.CompilerParams(dimension_semantics=("parallel", "arbitrary")),
        name=name,
    )(*operands)


def _unpack_refs(refs, n_x, n_consts, has_prev, staged):
    x_refs = refs[:n_x]
    s0_ref, carry_ref = refs[n_x:n_x + 2]
    consts = refs[n_x + 2:n_x + 2 + n_consts]
    pos = n_x + 2 + n_consts
    prev_ref = refs[pos] if has_prev else None
    pos += int(has_prev)
    o_ref, sout_ref, state_sc, rows_sc = refs[pos:pos + 4]
    stages = refs[pos + 4:] if staged else (None,) * n_x
    return x_refs, s0_ref, carry_ref, consts, prev_ref, o_ref, sout_ref, state_sc, rows_sc, stages


def _gdn_kernel(*refs, nbs, chunk, in_rows, n_x, n_consts, has_prev, carry, prepped):
    ((pb_ref, pg_ref), s0_ref, cbuf_ref, (cw_ref, vec_ref, bd_ref, eb_ref, eg_ref), prev_ref,
     o_ref, sout_ref, s_sc, xbuf, (stage_b, stage_g)) = _unpack_refs(refs, n_x, n_consts, has_prev,
                                                                    in_rows < chunk)
    c = pl.program_id(1)
    nb = nbs * HEADS
    rows = nbs * chunk
    n_hist = B_CONV - 1

    @pl.when(c == 0)
    def _():
        s_sc[...] = s0_ref[...].reshape(nb, HEAD_DIM, HEAD_DIM)
        if not prepped:
            xbuf[:, 0:SUBLANES, :] = jnp.zeros((nbs, SUBLANES, B_QKV), F32)
            xbuf[:, SUBLANES - n_hist:SUBLANES, :] = cbuf_ref[...]

    pb = _stage_rows(pb_ref, stage_b, in_rows)
    pg = _stage_rows(pg_ref, stage_g, in_rows)
    vec = vec_ref[...]
    og = vec[:, 2 * GATE_LANES:2 * GATE_LANES + WIDTH]
    bd = bd_ref[...]
    if prepped:
        qkv = pb[:, :, 0:B_QKV].reshape(rows, B_QKV)
        gate = pb[:, :, B_QKV:B_QKV + WIDTH].reshape(rows, WIDTH)
        beta_all = g_all = pg
        qn = qkv[:, 0:WIDTH]
        kn = qkv[:, WIDTH:2 * WIDTH]
    else:
        raw = pb[:, :, 0:B_QKV]
        xbuf[:, SUBLANES:SUBLANES + chunk, :] = raw
        cw = cw_ref[...]
        xfull = xbuf[...]
        conv = pltpu.roll(xfull, 3, axis=1)[:, SUBLANES:, :] * cw[0:1]
        conv = conv + pltpu.roll(xfull, 2, axis=1)[:, SUBLANES:, :] * cw[1:2]
        conv = conv + pltpu.roll(xfull, 1, axis=1)[:, SUBLANES:, :] * cw[2:3]
        conv = conv + raw * cw[3:4]
        if carry:
            xbuf[:, 0:SUBLANES, :] = xbuf[:, chunk:chunk + SUBLANES, :]
        qkv = jax.nn.silu(conv).reshape(rows, B_QKV)
        gate = jax.nn.silu(pb[:, :, B_QKV:B_QKV + WIDTH]).reshape(rows, WIDTH)
        alog = vec[:, 0:GATE_LANES]
        dtb = vec[:, GATE_LANES:2 * GATE_LANES]
        beta_all = jax.nn.sigmoid(pg)
        g_all = -jnp.exp(alog) * _softplus(pg + dtb)
        if in_rows < chunk:
            valid = lax.broadcasted_iota(jnp.int32, pg.shape, 1) < in_rows
            beta_all = jnp.where(valid, beta_all, 0.0)
            g_all = jnp.where(valid, g_all, 0.0)
        qf = qkv[:, 0:WIDTH]
        kf = qkv[:, WIDTH:2 * WIDTH]
        qn = qf * lax.rsqrt(_dot((qf * qf).astype(BF16), bd) + L2_EPS) * (HEAD_DIM ** -0.5)
        kn = kf * lax.rsqrt(_dot((kf * kf).astype(BF16), bd) + L2_EPS)
    vf = qkv[:, 2 * WIDTH:]

    row = lax.broadcasted_iota(jnp.int32, (chunk, chunk), 0)
    col = lax.broadcasted_iota(jnp.int32, (chunk, chunk), 1)
    tril = row >= col
    strict = row > col
    trilb = tril.astype(BF16)
    b1, b2, _ = _split3(beta_all.reshape(rows, GATE_LANES))
    beta_f = _dot(jnp.concatenate([b1, b2], axis=1), eb_ref[...])
    gc_small = [_cumsum_rows(trilb, g_all[b]) for b in range(nbs)]
    gc = _dot(jnp.concatenate(_split3(jnp.concatenate(gc_small, axis=0)), axis=1),
              eg_ref[...]).reshape(nbs, chunk, WIDTH)
    glast = gc[:, chunk - 1:chunk, :]
    e_g = jnp.exp(gc).reshape(rows, WIDTH)
    e_gl = jnp.exp(glast - gc).reshape(rows, WIDTH)
    bk = beta_f * kn

    def heads(x):
        return _split_heads(x.reshape(nbs, chunk, WIDTH).astype(BF16), 0)

    qkk = _bmm_nt(jnp.concatenate([heads(qn), heads(bk)], axis=1), heads(kn))
    grow_all = [_transpose_rows(g) for g in gc_small]
    gcol = jnp.stack([gc[b, :, h * HEAD_DIM:h * HEAD_DIM + 1] for b in range(nbs) for h in range(HEADS)])
    grow = jnp.stack([grow_all[b][HEADS + h:HEADS + h + 1, :] for b in range(nbs) for h in range(HEADS)])
    decay = jnp.where(tril, jnp.exp(jnp.minimum(gcol - grow, 0.0)), 0.0)
    qk = qkk[:, :chunk] * decay
    lm = jnp.where(strict, qkk[:, chunk:] * decay, 0.0)
    tinv = _unit_lower_inverse(lm, row, col, in_rows)
    rhs = jnp.concatenate([heads(beta_f * vf), heads(bk * e_g)], axis=2)
    uw = _bmm(tinv.astype(BF16), rhs)
    u = uw[:, :, :HEAD_DIM]
    wk = uw[:, :, HEAD_DIM:]
    s = s_sc[...]
    ws = _bmm(jnp.concatenate([wk.astype(BF16), heads(qn * e_g)], axis=1),
              s.astype(BF16))
    wnb = (u - ws[:, :chunk]).astype(BF16)
    o = ws[:, chunk:] + _bmm(qk.astype(BF16), wnb)
    s_sc[...] = _split_heads(jnp.exp(glast), 0) * s + _bmm_tn(heads(kn * e_gl), wnb)
    of = _merge_heads(o, nbs)
    of = of * lax.rsqrt(_group_sum(of * of, bd) * (1.0 / HEAD_DIM) + NORM_EPS) * og * gate
    o_ref[...] = of.reshape(nbs, chunk, WIDTH)[:, 0:in_rows, :]

    @pl.when(c == pl.num_programs(1) - 1)
    def _():
        _store_state(sout_ref, prev_ref, s_sc[...].reshape(nbs, HEADS, HEAD_DIM, HEAD_DIM))


def _rwkv_kernel(*refs, nbs, chunk, in_rows, n_x, n_consts, has_prev, carry, prepped):
    ((pc_ref,), s0_ref, sh_ref, (vec_ref, w2_ref, a2_ref, g2_ref, bd_ref), prev_ref,
     o_ref, sout_ref, st_sc, xs, (stage_c,)) = _unpack_refs(refs, n_x, n_consts, has_prev, in_rows < chunk)
    c = pl.program_id(1)
    rows = nbs * chunk

    @pl.when(c == 0)
    def _():
        for b in range(nbs):
            for h in range(HEADS):
                st_sc[b * HEADS + h] = s0_ref[b, h].T
        if not prepped:
            xs[:, SUBLANES - 1:SUBLANES, :] = sh_ref[...]

    vec = vec_ref[...]
    mu = vec[:, 0:C_PROJ]
    w0, a0, k_k, k_a, r_k, ln_g, ln_b = (vec[:, C_PROJ + i * WIDTH:C_PROJ + (i + 1) * WIDTH] for i in range(7))
    bd = bd_ref[...]
    if prepped:
        xp = pc_ref[...].reshape(rows, 8 * WIDTH)
        r, k2, v, kkn, bf, logw, gg, bonus = (xp[:, i * WIDTH:(i + 1) * WIDTH] for i in range(8))
    else:
        cp = _stage_rows(pc_ref, stage_c, in_rows)
        xs[:, SUBLANES:SUBLANES + chunk, :] = cp
        prev = xs[:, SUBLANES - 1:SUBLANES - 1 + chunk, :]
        if carry:
            xs[:, 0:SUBLANES, :] = xs[:, chunk:chunk + SUBLANES, :]
        xm = (cp + (prev - cp) * mu).reshape(rows, C_PROJ)
        r = xm[:, 0:WIDTH]
        k = xm[:, WIDTH:2 * WIDTH]
        v = xm[:, 2 * WIDTH:3 * WIDTH]
        wl = xm[:, 3 * WIDTH:3 * WIDTH + 64]
        al = xm[:, 3 * WIDTH + 64:3 * WIDTH + 128]
        gl = xm[:, 3 * WIDTH + 128:4 * WIDTH]
        w_log = -_softplus(-(w0 + _dot(jnp.tanh(wl).astype(BF16), w2_ref[...]))) - 0.5
        logw = -jnp.exp(w_log)
        a = jax.nn.sigmoid(a0 + _dot(al.astype(BF16), a2_ref[...]))
        gg = _dot(jax.nn.sigmoid(gl).astype(BF16), g2_ref[...])
        kkraw = k * k_k
        k2 = k * (1.0 + (a - 1.0) * k_a)
        bonus = _group_sum(r * k2 * r_k, bd) * v
        if in_rows < chunk:
            t_in_chunk = lax.broadcasted_iota(jnp.int32, (nbs, chunk, WIDTH), 1).reshape(rows, WIDTH)
            valid = t_in_chunk < in_rows
            logw = jnp.where(valid, logw, 0.0)
            kkraw = jnp.where(valid, kkraw, 0.0)
            k2 = jnp.where(valid, k2, 0.0)
        kkn = kkraw * lax.rsqrt(_dot((kkraw * kkraw).astype(BF16), bd) + L2_EPS)
        bf = kkn * a

    row = lax.broadcasted_iota(jnp.int32, (chunk, chunk), 0)
    col = lax.broadcasted_iota(jnp.int32, (chunk, chunk), 1)
    tril = row >= col
    strict = row > col
    trilb = tril.astype(BF16)

    def heads(x):
        return _split_heads(x.reshape(nbs, chunk, WIDTH).astype(BF16), 0)

    logw3 = logw.reshape(nbs, chunk, WIDTH)
    glog3 = jnp.stack([_cumsum_rows(trilb, logw3[b]) for b in range(nbs)])
    glast = glog3[:, chunk - 1:chunk, :]
    glog = glog3.reshape(rows, WIDTH)
    e_n = jnp.exp(-glog)
    e_l = jnp.exp(glast - glog3).reshape(rows, WIDTH)
    lhs = jnp.concatenate([heads(kkn * jnp.exp(glog - logw)), heads(r * jnp.exp(glog))], axis=1)
    rhs = jnp.concatenate([heads(bf * e_n), heads(k2 * e_n)], axis=1)
    pair = _bmm_nt(lhs, rhs)
    st = st_sc[...]
    ls = _bmm_nt(lhs, st.astype(BF16))
    lb = jnp.where(strict, pair[:, :chunk, :chunk], 0.0)
    lk = jnp.where(strict, pair[:, :chunk, chunk:], 0.0)
    arb = jnp.where(tril, pair[:, chunk:, :chunk], 0.0)
    ark = jnp.where(tril, pair[:, chunk:, chunk:], 0.0)
    tinv = _unit_lower_inverse(lb, row, col, in_rows)
    vhb = heads(v)
    ub = _bmm(tinv.astype(BF16), (ls[:, :chunk] + _bmm(lk.astype(BF16), vhb)).astype(BF16)).astype(BF16)
    y = ls[:, chunk:] + _bmm(jnp.concatenate([ark, -arb], axis=2).astype(BF16),
                             jnp.concatenate([vhb, ub], axis=1))
    upd = _bmm_tn(jnp.concatenate([vhb, -ub], axis=1),
                  jnp.concatenate([heads(k2 * e_l), heads(bf * e_l)], axis=1))
    st_sc[...] = _split_heads(jnp.exp(glast), 0) * st + upd
    yf = _merge_heads(y, nbs)
    dev = yf - _group_sum(yf, bd) * (1.0 / HEAD_DIM)
    var = _group_sum(dev * dev, bd) * (1.0 / HEAD_DIM)
    yn = dev * lax.rsqrt(var + GN_EPS) * ln_g + ln_b
    out = (yn + bonus) * gg
    o_ref[...] = out.reshape(nbs, chunk, WIDTH)[:, 0:in_rows, :]

    @pl.when(c == pl.num_programs(1) - 1)
    def _():
        _store_state(sout_ref, prev_ref,
                     jnp.stack([jnp.stack([st_sc[b * HEADS + h].T for h in range(HEADS)])
                                for b in range(nbs)]))


def _round_up(x, m):
    return (x + m - 1) // m * m


def _pool_offsets(stride):
    offs = []
    a = 0
    for k in range(len(POOL_WINDOWS)):
        a = _round_up(a + (1 << k) * stride, SUBLANES)
        offs.append(a)
    return offs


def _pool_kernel(hist_ref, dp_ref, wbd_ref, scale_ref, out_ref, s1, s2, *, rows, stride, start):
    offs = _pool_offsets(stride)
    d0 = offs[-1]
    n = d0 + rows
    dp = dp_ref[...]
    s1[0:d0, :] = hist_ref[0]
    s1[d0:n, :] = dp
    lane = lax.broadcasted_iota(jnp.int32, (1, WIDTH), 1)
    src, dst = s1, s2
    for k, a in enumerate(offs):
        sh = (1 << k) * stride
        shifted = jnp.where(lane >= k * HEAD_DIM, src[a - sh:n - sh, :], 0.0)
        if k < len(offs) - 1:
            dst[a:n, :] = src[a:n, :] + shifted
            src, dst = dst, src
        else:
            sums = src[a:n, :] + shifted
    assert stride & (stride - 1) == 0
    pos = start + (lax.broadcasted_iota(jnp.int32, (rows, WIDTH), 0) >> (stride.bit_length() - 1))
    window = jnp.left_shift(2, lax.broadcasted_iota(jnp.int32, (rows, WIDTH), 1) >> 6)
    cnt = jnp.minimum(pos + 1, window).astype(F32)
    diff = sums / cnt - dp
    out_ref[...] = _dot(diff.astype(BF16), wbd_ref[...]) * scale_ref[...]


def _pool(hist, dp, wbd, scale, l, *, nseq, rows, stride, start):
    d0 = _pool_offsets(stride)[-1]
    return pl.pallas_call(
        functools.partial(_pool_kernel, rows=rows, stride=stride, start=start),
        grid=(nseq,),
        in_specs=[pl.BlockSpec((1, d0, WIDTH), lambda b: (b, 0, 0)),
                  pl.BlockSpec((rows, WIDTH), lambda b: (b, 0)),
                  _layer_block(wbd, l), _layer_block(scale, l)],
        out_specs=pl.BlockSpec((rows, WIDTH), lambda b: (b, 0)),
        out_shape=jax.ShapeDtypeStruct((nseq * rows, WIDTH), F32),
        scratch_shapes=[pltpu.VMEM((d0 + rows, WIDTH), F32)] * 2,
        compiler_params=pltpu.CompilerParams(dimension_semantics=("parallel",)),
        name="pool",
    )(hist, dp, wbd, scale)


def _ffn_kernel(x_ref, ma_ref, mb_ref, mc_ref, md_ref, wo_ref, g2_ref, wu_ref, wd_ref, gf_ref,
                o_ref, *, tf, final):
    mixed = None
    for i, m_ref in enumerate((ma_ref, mb_ref, mc_ref, md_ref)):
        part = _dot(m_ref[...].astype(BF16), wo_ref[i * WIDTH:(i + 1) * WIDTH, :])
        mixed = part if mixed is None else mixed + part
    x = x_ref[...] + mixed
    hm = _rms(x, g2_ref[...]).astype(BF16)
    down = None
    for j in range(D_FF // tf):
        up = jnp.maximum(_dot(hm, wu_ref[:, j * tf:(j + 1) * tf]), 0.0)
        part = _dot((up * up).astype(BF16), wd_ref[j * tf:(j + 1) * tf, :])
        down = part if down is None else down + part
    x = x + down
    if final:
        x = _rms(x, gf_ref[...])
    o_ref[...] = x


def _ffn(x, mixed, wo, g2, wu, wd, gf, l):
    t = x.shape[0]
    tm = min(ROW_TILE, t)
    row = lambda i: (i, 0)
    return pl.pallas_call(
        functools.partial(_ffn_kernel, tf=1024, final=l == DEPTH - 1),
        grid=(t // tm,),
        in_specs=[pl.BlockSpec((tm, D_MODEL), row)] + [pl.BlockSpec((tm, WIDTH), row)] * 4
        + [_layer_block(wo, l, single_buffer=True), _layer_block(g2, l),
           _layer_block(wu, l, single_buffer=True), _layer_block(wd, l, single_buffer=True), _whole(gf)],
        out_specs=pl.BlockSpec((tm, D_MODEL), row),
        out_shape=jax.ShapeDtypeStruct((t, D_MODEL), F32),
        compiler_params=pltpu.CompilerParams(dimension_semantics=("parallel",),
                                             vmem_limit_bytes=VMEM_LIMIT),
        name="ffn",
    )(x, *mixed, wo, g2, wu, wd, gf)


def _rows(v):
    return v[:, None, :]


def kernel(x_prompt, x_sample, state_b_conv, state_b_ssm, state_c_shift, state_c_wkv, state_d_pool, norm1_g, w_in, a_ws, a_bs, a_vnorm_g, b_conv_w, b_a_log, b_dt_bias, b_onorm_g, c_mu, c_w0, c_w2, c_a0, c_a2, c_g2, c_k_k, c_k_a, c_r_k, c_ln_g, c_ln_b, d_w, d_scale, w_out, norm2_g, w_up, w_down, final_g):
    nb, seq, _ = x_prompt.shape
    ns, dseq, _ = x_sample.shape
    a_chunk = a_ws.shape[-1]

    a_end = 2 * WIDTH
    b_end = a_end + 4 * WIDTH
    c_off = b_end + 2 * HEADS
    d_off = c_off + C_PROJ
    w_in_b = w_in.astype(BF16)
    w_cols = [w_in_b[:, :, :a_end], w_in_b[:, :, a_end:b_end], w_in_b[:, :, c_off:d_off], w_in_b[:, :, d_off:],
              jnp.pad(w_in_b[:, :, b_end:c_off], ((0, 0), (0, 0), (0, GATE_LANES - 2 * HEADS)))]
    wo = w_out.astype(BF16)
    wu = w_up.astype(BF16)
    wd = w_down.astype(BF16)
    g1 = _rows(norm1_g)
    g2 = _rows(norm2_g)
    gf = final_g[None, :]
    vg = _rows(a_vnorm_g)
    causal = jnp.tril(jnp.ones((a_chunk, a_chunk), dtype=bool))
    wm = jnp.where(causal, a_ws, 0.0)
    bias_t = jnp.repeat(jnp.transpose(a_bs, (0, 2, 1)), HEAD_DIM, axis=2)
    mm_p = jnp.transpose(wm, (0, 2, 1, 3)).reshape(DEPTH, a_chunk, HEADS * a_chunk).astype(BF16)
    srow = jnp.arange(ns * dseq)
    step_onehot = (srow[:, None] % dseq == jnp.arange(dseq)[None, :]).astype(F32)
    same_seq = (srow[:, None] // dseq) == (srow[None, :] // dseq)
    mm_s = jnp.einsum('it,lhts,js->lihj', step_onehot, wm[:, :, :dseq, :dseq], step_onehot,
                      precision=lax.Precision.HIGHEST)
    mm_s = jnp.where(same_seq[None, :, None, :], mm_s, 0.0).reshape(
        DEPTH, ns * dseq, HEADS * ns * dseq).astype(BF16)
    bias_s = jnp.tile(bias_t[:, :dseq], (1, ns, 1))
    lead = jnp.zeros((DEPTH, HEADS), F32)
    tail = jnp.zeros((DEPTH, GATE_LANES - 2 * HEADS), F32)
    gvec = jnp.concatenate([lead, b_a_log, tail, lead, b_dt_bias, tail, jnp.tile(b_onorm_g, (1, HEADS))],
                           axis=1)[:, None, :]
    cvec = jnp.concatenate([c_mu, c_w0, c_a0, c_k_k, c_k_a, c_r_k.reshape(DEPTH, WIDTH), c_ln_g, c_ln_b],
                           axis=1)[:, None, :]
    c_w2b, c_a2b, c_g2b = c_w2.astype(BF16), c_a2.astype(BF16), c_g2.astype(BF16)
    n_groups = len(POOL_WINDOWS)
    wbd = jnp.einsum('lgcd,gh->lgchd', d_w, jnp.eye(n_groups, dtype=F32)).reshape(
        DEPTH, WIDTH, WIDTH).astype(BF16)
    dscale = _rows(d_scale)
    lane_head = jnp.arange(WIDTH) // HEAD_DIM
    gate_lane = jnp.arange(GATE_LANES)
    bd = (lane_head[:, None] == lane_head[None, :]).astype(BF16)
    eb = jnp.tile((gate_lane[:, None] == lane_head[None, :]).astype(BF16), (2, 1))
    eg = jnp.tile((gate_lane[:, None] == lane_head[None, :] + HEADS).astype(BF16), (3, 1))

    def run_group(x, l, grp, prev_b, prev_c):
        nseq, length = grp["nseq"], grp["length"]
        conv_state = grp["b_conv"][grp["state_layer"](l)]
        if grp["prep"]:
            a_out, a_v, pb, pc, pd, pg, tails, tails_c = _inproj(
                x, g1, w_cols, grp["mm"], grp["bias"], vg, l,
                gdn_prep=(b_conv_w, gvec, bd, conv_state, length, cvec, c_w2b, c_a2b, c_g2b,
                          grp["c_shift"][grp["state_layer"](l)]))
            b_rows = tails.reshape(nseq, -1, SUBLANES, B_QKV)[:, -1]
            c_last = tails_c.reshape(nseq, -1, SUBLANES, C_PROJ)[:, -1, SUBLANES - 1]
        else:
            a_out, a_v, pb, pc, pd, pg = _inproj(x, g1, w_cols, grp["mm"], grp["bias"], vg, l)
            b_rows = pb.reshape(nseq, length, 4 * WIDTH)[:, :, :B_QKV]
            c_last = pc.reshape(nseq, length, C_PROJ)[:, length - 1]
        b_tail = jnp.concatenate([conv_state, b_rows], axis=1)[:, -(B_CONV - 1):]
        pb3 = pb.reshape(nseq, length, 4 * WIDTH)
        pc3 = pc.reshape(nseq, length, pc.shape[1])
        b_out, b_state = _recurrent_call(
            _gdn_kernel, "gdn", [pb3, pg.reshape(nseq, length, GATE_LANES)], grp["b_ssm"], grp["state_layer"](l),
            grp["b_conv"], [(b_conv_w, l), (gvec, l), (bd, None), (eb, None), (eg, None)], prev_b,
            nbs=grp["nbs"], chunk=grp["chunk"], carry_width=B_QKV, prepped=grp["prep"])
        c_out, c_state = _recurrent_call(
            _rwkv_kernel, "rwkv", [pc3], grp["c_wkv"], grp["state_layer"](l), grp["c_shift"],
            [(cvec, l), (c_w2b, l), (c_a2b, l), (c_g2b, l), (bd, None)], prev_c,
            nbs=grp["nbs"], chunk=grp["chunk"], carry_width=C_PROJ, prepped=grp["prep"])
        d_out = grp["pool"](pd, l)
        x = _ffn(x, (a_out, b_out.reshape(nseq * length, WIDTH), c_out.reshape(nseq * length, WIDTH), d_out),
                 wo, g2, wu, wd, gf, l)
        return x, a_v, b_tail, c_last, pd.reshape(nseq, length, WIDTH), b_state, c_state

    d0 = _pool_offsets(1)[-1]
    zero_hist = jnp.zeros((nb, d0, WIDTH), F32)
    prompt = dict(
        nseq=nb, length=seq, nbs=PROMPT_SEQS_PER_STEP, chunk=MIX_CHUNK, mm=mm_p, bias=bias_t,
        b_ssm=jnp.zeros((1, nb, HEADS, HEAD_DIM, HEAD_DIM), F32), c_wkv=jnp.zeros((1, nb, HEADS, HEAD_DIM, HEAD_DIM), F32),
        b_conv=jnp.zeros((1, nb, B_CONV - 1, B_QKV), F32), c_shift=jnp.zeros((1, nb, 1, C_PROJ), F32),
        state_layer=lambda l: 0, prep=True,
        pool=lambda pd, l: _pool(zero_hist, pd, wbd, dscale, l, nseq=nb, rows=seq, stride=1, start=0))

    def sample_pool(pd, l):
        pd_t = jnp.transpose(pd.reshape(ns, dseq, WIDTH), (1, 0, 2)).reshape(dseq * ns, WIDTH)
        hist = jnp.transpose(state_d_pool[l], (1, 0, 2)).reshape(1, POOL_BUF * ns, WIDTH)
        out_t = _pool(hist, pd_t, wbd, dscale, l, nseq=1, rows=dseq * ns, stride=ns, start=PAST_LEN)
        return jnp.transpose(out_t.reshape(dseq, ns, WIDTH), (1, 0, 2)).reshape(ns * dseq, WIDTH)

    sample = dict(
        nseq=ns, length=dseq, nbs=SAMPLE_SEQS_PER_STEP, chunk=SUBLANES, mm=mm_s, bias=bias_s,
        b_ssm=state_b_ssm, c_wkv=state_c_wkv, b_conv=state_b_conv, c_shift=state_c_shift[:, :, None, :],
        state_layer=lambda l: l, prep=False, pool=sample_pool)

    xp = x_prompt.reshape(nb * seq, D_MODEL)
    xs = x_sample.reshape(ns * dseq, D_MODEL)
    p_bc, p_cs, p_dp, s_av, s_bc, s_cs, s_dp = [], [], [], [], [], [], []
    p_bs = p_cw = s_bs = s_cw = None
    for l in range(DEPTH):
        xp, _, b_tail, c_last, pd3, p_bs, p_cw = run_group(xp, l, prompt, p_bs, p_cw)
        p_bs = p_bs if l else p_bs[None]
        p_cw = p_cw if l else p_cw[None]
        p_bc.append(b_tail)
        p_cs.append(c_last)
        p_dp.append(pd3[:, seq - POOL_BUF:])

        xs, a_v, b_tail, c_last, pd3, s_bs, s_cw = run_group(xs, l, sample, s_bs, s_cw)
        s_bs = s_bs if l else s_bs[None]
        s_cw = s_cw if l else s_cw[None]
        s_av.append(a_v.reshape(ns, dseq, WIDTH))
        s_bc.append(b_tail)
        s_cs.append(c_last)
        s_dp.append(jnp.concatenate([state_d_pool[l], pd3], axis=1)[:, -POOL_BUF:])

    return (xp.reshape(nb, seq, D_MODEL), xs.reshape(ns, dseq, D_MODEL),
            jnp.stack(p_bc), p_bs, jnp.stack(p_cs), p_cw, jnp.stack(p_dp),
            jnp.stack(s_av), jnp.stack(s_bc), s_bs, jnp.stack(s_cs), s_cw, jnp.stack(s_dp))
```

```python
import functools

import jax
import jax.numpy as jnp
from jax import lax
from jax.experimental import pallas as pl
from jax.experimental.pallas import tpu as pltpu

F32 = jnp.float32
BF16 = jnp.bfloat16

D_MODEL = 1024
DEPTH = 2
HEADS = 4
HEAD_DIM = 64
WIDTH = HEADS * HEAD_DIM
B_QKV = 3 * WIDTH
B_CONV = 4
C_PROJ = 4 * WIDTH
POOL_WINDOWS = (2, 4, 8, 16)
POOL_BUF = 15
D_FF = 4 * D_MODEL
PAST_LEN = 16384
NORM_EPS = 1e-6
L2_EPS = 1e-6
GN_EPS = 64e-5
GATE_LANES = 128
SUBLANES = 8
VMEM_LIMIT = 48 * 1024 * 1024
MIX_CHUNK = 64
PROMPT_SEQS_PER_STEP = 8
SAMPLE_SEQS_PER_STEP = 32
ROW_TILE = 512


def _rms(x, g):
    return x * lax.rsqrt(jnp.mean(x * x, axis=-1, keepdims=True) + NORM_EPS) * g


def _softplus(x):
    return jnp.maximum(x, 0.0) + jnp.log1p(jnp.exp(-jnp.abs(x)))


def _dot(a, b):
    return jnp.dot(a, b, preferred_element_type=F32)


def _bmm(a, b):
    return jnp.einsum('bij,bjk->bik', a, b, preferred_element_type=F32)


def _bmm_nt(a, b):
    return jnp.einsum('bik,bjk->bij', a, b, preferred_element_type=F32)


def _bmm_tn(a, b):
    return jnp.einsum('bki,bkj->bij', a, b, preferred_element_type=F32)


def _layer_block(arr, l, single_buffer=False):
    shape = arr.shape[1:]
    index = lambda *_: (l,) + (0,) * len(shape)
    if single_buffer:
        return pl.BlockSpec((None,) + shape, index, pipeline_mode=pl.Buffered(1))
    return pl.BlockSpec((None,) + shape, index)


def _whole(arr):
    return pl.BlockSpec(arr.shape, lambda *_: (0,) * arr.ndim)


def _split_heads(x, lane0):
    return jnp.stack([x[b, :, lane0 + h * HEAD_DIM:lane0 + (h + 1) * HEAD_DIM]
                      for b in range(x.shape[0]) for h in range(HEADS)])


def _merge_heads(x, nbs):
    return jnp.concatenate([jnp.concatenate([x[b * HEADS + h] for h in range(HEADS)], axis=1)
                            for b in range(nbs)], axis=0)


def _group_sum(x, bd):
    hi = x.astype(BF16)
    lo = (x - hi.astype(F32)).astype(BF16)
    return _dot(hi, bd) + _dot(lo, bd)


def _split3(x):
    p1 = x.astype(BF16)
    r1 = x - p1.astype(F32)
    p2 = r1.astype(BF16)
    return p1, p2, (r1 - p2.astype(F32)).astype(BF16)


def _cumsum_rows(tril_b, x):
    w = x.shape[1]
    y = _dot(tril_b, jnp.concatenate(_split3(x), axis=1))
    return y[:, :w] + (y[:, w:2 * w] + y[:, 2 * w:])


def _unit_lower_inverse(lm, row, col, n_valid):
    def sub_diag_block(shift):
        return (((row >> (shift + 1)) == (col >> (shift + 1)))
                & (((row >> shift) & 1) == 1) & (((col >> shift) & 1) == 0))

    m = (row == col).astype(F32) - jnp.where(sub_diag_block(0), lm, 0.0)
    shift = 1
    while (1 << shift) < n_valid:
        cs = jnp.where(sub_diag_block(shift), lm, 0.0).astype(BF16)
        mb = m.astype(BF16)
        m = m - _bmm(_bmm(mb, cs).astype(BF16), mb)
        shift += 1
    return m


def _transpose_rows(a):
    n = a.shape[0]
    if n < GATE_LANES:
        a = jnp.concatenate([a, jnp.zeros((GATE_LANES - n, a.shape[1]), a.dtype)], axis=0)
    return a.T[:, :n]


def _stage_rows(ref, stage, in_rows):
    if stage is None:
        return ref[...]
    stage[...] = jnp.zeros(stage.shape, stage.dtype)
    stage[:, 0:in_rows, :] = ref[...]
    return stage[...]


def _conv_silu(raw, hist, tail_ref, lo, cw):
    tm = raw.shape[0]
    cols = slice(lo, lo + WIDTH)
    xfull = jnp.concatenate([hist[:, cols], raw], axis=0)
    conv = pltpu.roll(xfull, 3, axis=0)[SUBLANES:, :] * cw[0:1, cols]
    conv = conv + pltpu.roll(xfull, 2, axis=0)[SUBLANES:, :] * cw[1:2, cols]
    conv = conv + pltpu.roll(xfull, 1, axis=0)[SUBLANES:, :] * cw[2:3, cols]
    conv = conv + raw * cw[3:4, cols]
    tail = raw[tm - SUBLANES:tm, :]
    hist[:, cols] = tail
    tail_ref[:, cols] = tail
    return jax.nn.silu(conv)


def _inproj_kernel(*refs, r, prep, tiles_per_seq):
    (x_ref, g_ref, wa_ref, wb_ref, wc_ref, wd_ref, wg_ref, mm_ref, bias_ref, vg_ref) = refs[:10]
    if prep:
        cw_ref, gvec_ref, bd_ref, cbuf_ref, cvec_ref, w2_ref, a2_ref, g2_ref, sh_ref = refs[10:19]
        a_ref, v_ref, pb_ref, pc_ref, pd_ref, pg_ref, tail_ref, tailc_ref, hist, hist_c = refs[19:]
    else:
        a_ref, v_ref, pb_ref, pc_ref, pd_ref, pg_ref = refs[10:]
    h = _rms(x_ref[...], g_ref[...]).astype(BF16)

    def gmlp_gates(pa):
        v = _rms(jax.nn.gelu(pa[:, WIDTH:]), vg_ref[...])
        v_ref[...] = v
        return jax.nn.gelu(pa[:, :WIDTH]), v

    def gmlp_mix(u, v):
        lane_head = lax.broadcasted_iota(jnp.int32, (r, WIDTH), 1) >> 6
        mm = mm_ref[...]
        bias = bias_ref[...]
        for i in range(u.shape[0] // r):
            vc = v[i * r:(i + 1) * r]
            per_head = jnp.concatenate([jnp.where(lane_head == h, vc, 0.0).astype(BF16)
                                        for h in range(HEADS)], axis=0)
            a_ref[i * r:(i + 1) * r, :] = u[i * r:(i + 1) * r] * (_dot(mm, per_head) + bias)

    if prep:
        n_hist = B_CONV - 1

        @pl.when(pl.program_id(0) % tiles_per_seq == 0)
        def _():
            hist[...] = jnp.zeros(hist.shape, F32)
            hist[SUBLANES - n_hist:SUBLANES, :] = cbuf_ref[0]
            hist_c[...] = jnp.zeros(hist_c.shape, F32)
            hist_c[SUBLANES - 1:SUBLANES, :] = sh_ref[0]

        cvec = cvec_ref[...]
        mu = cvec[:, 0:C_PROJ]
        w0, a0, k_k, k_a, r_k = (cvec[:, C_PROJ + i * WIDTH:C_PROJ + (i + 1) * WIDTH] for i in range(5))

        def shifted_lerp(cp, lo):
            cols = slice(lo, lo + WIDTH)
            prev = pltpu.roll(jnp.concatenate([hist_c[:, cols], cp], axis=0), 1, axis=0)[SUBLANES:, :]
            tail = cp[cp.shape[0] - SUBLANES:, :]
            hist_c[:, cols] = tail
            tailc_ref[:, cols] = tail
            return cp + (prev - cp) * mu[:, cols]

        cw = cw_ref[...]
        bd = bd_ref[...]
        gvec = gvec_ref[...]
        raw_q = _dot(h, wb_ref[:, 0:WIDTH])
        raw_k = _dot(h, wb_ref[:, WIDTH:2 * WIDTH])
        raw_v = _dot(h, wb_ref[:, 2 * WIDTH:3 * WIDTH])
        qf = _conv_silu(raw_q, hist, tail_ref, 0, cw)
        pb_ref[:, 0:WIDTH] = qf * lax.rsqrt(_dot((qf * qf).astype(BF16), bd) + L2_EPS) * (HEAD_DIM ** -0.5)
        raw_gate = _dot(h, wb_ref[:, 3 * WIDTH:4 * WIDTH])
        pg = _dot(h, wg_ref[...])
        kf = _conv_silu(raw_k, hist, tail_ref, WIDTH, cw)
        pb_ref[:, WIDTH:2 * WIDTH] = kf * lax.rsqrt(_dot((kf * kf).astype(BF16), bd) + L2_EPS)
        cp_l = _dot(h, wc_ref[:, 3 * WIDTH:4 * WIDTH])
        cp_k = _dot(h, wc_ref[:, WIDTH:2 * WIDTH])
        pb_ref[:, 2 * WIDTH:3 * WIDTH] = _conv_silu(raw_v, hist, tail_ref, 2 * WIDTH, cw)
        pa = _dot(h, wa_ref[...])
        pb_ref[:, 3 * WIDTH:4 * WIDTH] = jax.nn.silu(raw_gate)
        alog = gvec[:, 0:GATE_LANES]
        dtb = gvec[:, GATE_LANES:2 * GATE_LANES]
        lane = lax.broadcasted_iota(jnp.int32, pg.shape, 1)
        pg_ref[...] = jnp.where(lane < HEADS, jax.nn.sigmoid(pg), -jnp.exp(alog) * _softplus(pg + dtb))
        cp_r = _dot(h, wc_ref[:, 0:WIDTH])
        cp_v = _dot(h, wc_ref[:, 2 * WIDTH:3 * WIDTH])
        xm_l = shifted_lerp(cp_l, 3 * WIDTH)
        w_log = -_softplus(-(w0 + _dot(jnp.tanh(xm_l[:, 0:64]).astype(BF16), w2_ref[...]))) - 0.5
        pc_ref[:, 5 * WIDTH:6 * WIDTH] = -jnp.exp(w_log)
        a = jax.nn.sigmoid(a0 + _dot(xm_l[:, 64:128].astype(BF16), a2_ref[...]))
        pc_ref[:, 6 * WIDTH:7 * WIDTH] = _dot(jax.nn.sigmoid(xm_l[:, 128:256]).astype(BF16), g2_ref[...])
        pd_ref[...] = _dot(h, wd_ref[...])
        k = shifted_lerp(cp_k, WIDTH)
        kkraw = k * k_k
        kkn = kkraw * lax.rsqrt(_dot((kkraw * kkraw).astype(BF16), bd) + L2_EPS)
        k2 = k * (1.0 + (a - 1.0) * k_a)
        pc_ref[:, WIDTH:2 * WIDTH] = k2
        pc_ref[:, 3 * WIDTH:4 * WIDTH] = kkn
        pc_ref[:, 4 * WIDTH:5 * WIDTH] = kkn * a
        rr = shifted_lerp(cp_r, 0)
        vv = shifted_lerp(cp_v, 2 * WIDTH)
        pc_ref[:, 0:WIDTH] = rr
        pc_ref[:, 2 * WIDTH:3 * WIDTH] = vv
        pc_ref[:, 7 * WIDTH:8 * WIDTH] = _group_sum(rr * k2 * r_k, bd) * vv
        gmlp_mix(*gmlp_gates(pa))
    else:
        pa = _dot(h, wa_ref[...])
        pb_ref[...] = _dot(h, wb_ref[...])
        u, v = gmlp_gates(pa)
        pc_ref[...] = _dot(h, wc_ref[...])
        pd_ref[...] = _dot(h, wd_ref[...])
        pg_ref[...] = _dot(h, wg_ref[...])
        gmlp_mix(u, v)


def _inproj(x, g, ws, mm, bias, vg, l, gdn_prep=None):
    t = x.shape[0]
    r = mm.shape[1]
    tm = max(r, min(ROW_TILE, t))
    widths = [WIDTH, WIDTH] + [w.shape[2] for w in ws[1:]]
    in_specs = ([pl.BlockSpec((tm, D_MODEL), lambda i: (i, 0)), _layer_block(g, l)]
                + [_layer_block(w, l, single_buffer=True) for w in ws]
                + [_layer_block(mm, l), _layer_block(bias, l), _layer_block(vg, l)])
    operands = [x, g, *ws, mm, bias, vg]
    out_specs = [pl.BlockSpec((tm, n), lambda i: (i, 0)) for n in widths]
    out_shape = [jax.ShapeDtypeStruct((t, n), F32) for n in widths]
    scratch = []
    tiles_per_seq = 1
    if gdn_prep is not None:
        conv_w, gvec, bd, conv_state, seq_len, cvec, w2, a2, g2, shift_state = gdn_prep
        tiles_per_seq = seq_len // tm
        per_seq = lambda i: (i // tiles_per_seq, 0, 0)
        in_specs += [_layer_block(conv_w, l), _layer_block(gvec, l), _whole(bd),
                     pl.BlockSpec((1,) + conv_state.shape[1:], per_seq),
                     _layer_block(cvec, l), _layer_block(w2, l), _layer_block(a2, l), _layer_block(g2, l),
                     pl.BlockSpec((1,) + shift_state.shape[1:], per_seq)]
        operands += [conv_w, gvec, bd, conv_state, cvec, w2, a2, g2, shift_state]
        widths[3] = 8 * WIDTH
        out_specs[3] = pl.BlockSpec((tm, 8 * WIDTH), lambda i: (i, 0))
        out_shape[3] = jax.ShapeDtypeStruct((t, 8 * WIDTH), F32)
        for w in (B_QKV, C_PROJ):
            out_specs.append(pl.BlockSpec((SUBLANES, w), lambda i: (i, 0)))
            out_shape.append(jax.ShapeDtypeStruct((t // tm * SUBLANES, w), F32))
            scratch.append(pltpu.VMEM((SUBLANES, w), F32))
    return pl.pallas_call(
        functools.partial(_inproj_kernel, r=r, prep=gdn_prep is not None, tiles_per_seq=tiles_per_seq),
        grid=(t // tm,),
        in_specs=in_specs,
        out_specs=out_specs,
        out_shape=out_shape,
        scratch_shapes=scratch,
        compiler_params=pltpu.CompilerParams(
            dimension_semantics=("arbitrary" if gdn_prep is not None else "parallel",),
            vmem_limit_bytes=VMEM_LIMIT),
        name="inproj",
    )(*operands)


def _store_state(sout_ref, prev_ref, new_state):
    if prev_ref is None:
        sout_ref[...] = new_state
    else:
        n_prev = prev_ref.shape[0]
        sout_ref[0:n_prev] = prev_ref[...]
        sout_ref[n_prev] = new_state


def _recurrent_call(kernel_fn, name, x_blocks, s0, s0_layer, carry_in, consts, prev, *,
                    nbs, chunk, carry_width, out_width=WIDTH, **kernel_flags):
    nseq, length, _ = x_blocks[0].shape
    in_rows = min(chunk, length)
    n_chunks = length // in_rows
    blk = lambda b, c: (b, c, 0)
    state_block = (nbs, HEADS, HEAD_DIM, HEAD_DIM)
    in_specs = [pl.BlockSpec((nbs, in_rows, x.shape[2]), blk) for x in x_blocks]
    in_specs += [pl.BlockSpec((None,) + state_block, lambda b, c: (s0_layer, b, 0, 0, 0)),
                 pl.BlockSpec((None, nbs) + carry_in.shape[2:], lambda b, c: (s0_layer, b, 0, 0))]
    in_specs += [_whole(a) if l is None else _layer_block(a, l) for a, l in consts]
    operands = list(x_blocks) + [s0, carry_in] + [a for a, _ in consts]
    if prev is None:
        state_shape = (nseq, HEADS, HEAD_DIM, HEAD_DIM)
        state_spec = pl.BlockSpec(state_block, lambda b, c: (b, 0, 0, 0))
    else:
        n_prev = prev.shape[0]
        in_specs.append(pl.BlockSpec((n_prev,) + state_block, lambda b, c: (0, b, 0, 0, 0)))
        operands.append(prev)
        state_shape = (n_prev + 1, nseq, HEADS, HEAD_DIM, HEAD_DIM)
        state_spec = pl.BlockSpec((n_prev + 1,) + state_block, lambda b, c: (0, b, 0, 0, 0))
    scratch = [pltpu.VMEM((nbs * HEADS, HEAD_DIM, HEAD_DIM), F32),
               pltpu.VMEM((nbs, chunk + SUBLANES, carry_width), F32)]
    if in_rows < chunk:
        scratch += [pltpu.VMEM((nbs, chunk, x.shape[2]), F32) for x in x_blocks]
    return pl.pallas_call(
        functools.partial(kernel_fn, nbs=nbs, chunk=chunk, in_rows=in_rows, n_x=len(x_blocks),
                          n_consts=len(consts), has_prev=prev is not None, carry=n_chunks > 1,
                          **kernel_flags),
        grid=(nseq // nbs, n_chunks),
        in_specs=in_specs,
        out_specs=[pl.BlockSpec((nbs, in_rows, out_width), blk), state_spec],
        out_shape=[jax.ShapeDtypeStruct((nseq, length, out_width), F32),
                   jax.ShapeDtypeStruct(state_shape, F32)],
        scratch_shapes=scratch,
        compiler_params=pltpu.CompilerParams(dimension_semantics=("parallel", "arbitrary")),
        name=name,
    )(*operands)


def _unpack_refs(refs, n_x, n_consts, has_prev, staged):
    x_refs = refs[:n_x]
    s0_ref, carry_ref = refs[n_x:n_x + 2]
    consts = refs[n_x + 2:n_x + 2 + n_consts]
    pos = n_x + 2 + n_consts
    prev_ref = refs[pos] if has_prev else None
    pos += int(has_prev)
    o_ref, sout_ref, state_sc, rows_sc = refs[pos:pos + 4]
    stages = refs[pos + 4:] if staged else (None,) * n_x
    return x_refs, s0_ref, carry_ref, consts, prev_ref, o_ref, sout_ref, state_sc, rows_sc, stages


def _gdn_kernel(*refs, nbs, chunk, in_rows, n_x, n_consts, has_prev, carry, prepped):
    ((pb_ref, pg_ref), s0_ref, cbuf_ref, (cw_ref, vec_ref, bd_ref, eb_ref, eg_ref), prev_ref,
     o_ref, sout_ref, s_sc, xbuf, (stage_b, stage_g)) = _unpack_refs(refs, n_x, n_consts, has_prev,
                                                                    in_rows < chunk)
    c = pl.program_id(1)
    nb = nbs * HEADS
    rows = nbs * chunk
    n_hist = B_CONV - 1

    @pl.when(c == 0)
    def _():
        s_sc[...] = s0_ref[...].reshape(nb, HEAD_DIM, HEAD_DIM)
        if not prepped:
            xbuf[:, 0:SUBLANES, :] = jnp.zeros((nbs, SUBLANES, B_QKV), F32)
            xbuf[:, SUBLANES - n_hist:SUBLANES, :] = cbuf_ref[...]

    pb = _stage_rows(pb_ref, stage_b, in_rows)
    pg = _stage_rows(pg_ref, stage_g, in_rows)
    vec = vec_ref[...]
    og = vec[:, 2 * GATE_LANES:2 * GATE_LANES + WIDTH]
    bd = bd_ref[...]
    if prepped:
        qkv = pb[:, :, 0:B_QKV].reshape(rows, B_QKV)
        gate = pb[:, :, B_QKV:B_QKV + WIDTH].reshape(rows, WIDTH)
        beta_all = g_all = pg
        qn = qkv[:, 0:WIDTH]
        kn = qkv[:, WIDTH:2 * WIDTH]
    else:
        raw = pb[:, :, 0:B_QKV]
        xbuf[:, SUBLANES:SUBLANES + chunk, :] = raw
        cw = cw_ref[...]
        xfull = xbuf[...]
        conv = pltpu.roll(xfull, 3, axis=1)[:, SUBLANES:, :] * cw[0:1]
        conv = conv + pltpu.roll(xfull, 2, axis=1)[:, SUBLANES:, :] * cw[1:2]
        conv = conv + pltpu.roll(xfull, 1, axis=1)[:, SUBLANES:, :] * cw[2:3]
        conv = conv + raw * cw[3:4]
        if carry:
            xbuf[:, 0:SUBLANES, :] = xbuf[:, chunk:chunk + SUBLANES, :]
        qkv = jax.nn.silu(conv).reshape(rows, B_QKV)
        gate = jax.nn.silu(pb[:, :, B_QKV:B_QKV + WIDTH]).reshape(rows, WIDTH)
        alog = vec[:, 0:GATE_LANES]
        dtb = vec[:, GATE_LANES:2 * GATE_LANES]
        beta_all = jax.nn.sigmoid(pg)
        g_all = -jnp.exp(alog) * _softplus(pg + dtb)
        if in_rows < chunk:
            valid = lax.broadcasted_iota(jnp.int32, pg.shape, 1) < in_rows
            beta_all = jnp.where(valid, beta_all, 0.0)
            g_all = jnp.where(valid, g_all, 0.0)
        qf = qkv[:, 0:WIDTH]
        kf = qkv[:, WIDTH:2 * WIDTH]
        qn = qf * lax.rsqrt(_dot((qf * qf).astype(BF16), bd) + L2_EPS) * (HEAD_DIM ** -0.5)
        kn = kf * lax.rsqrt(_dot((kf * kf).astype(BF16), bd) + L2_EPS)
    vf = qkv[:, 2 * WIDTH:]

    row = lax.broadcasted_iota(jnp.int32, (chunk, chunk), 0)
    col = lax.broadcasted_iota(jnp.int32, (chunk, chunk), 1)
    tril = row >= col
    strict = row > col
    trilb = tril.astype(BF16)
    b1, b2, _ = _split3(beta_all.reshape(rows, GATE_LANES))
    beta_f = _dot(jnp.concatenate([b1, b2], axis=1), eb_ref[...])
    gc_small = [_cumsum_rows(trilb, g_all[b]) for b in range(nbs)]
    gc = _dot(jnp.concatenate(_split3(jnp.concatenate(gc_small, axis=0)), axis=1),
              eg_ref[...]).reshape(nbs, chunk, WIDTH)
    glast = gc[:, chunk - 1:chunk, :]
    e_g = jnp.exp(gc).reshape(rows, WIDTH)
    e_gl = jnp.exp(glast - gc).reshape(rows, WIDTH)
    bk = beta_f * kn

    def heads(x):
        return _split_heads(x.reshape(nbs, chunk, WIDTH).astype(BF16), 0)

    qkk = _bmm_nt(jnp.concatenate([heads(qn), heads(bk)], axis=1), heads(kn))
    grow_all = [_transpose_rows(g) for g in gc_small]
    gcol = jnp.stack([gc[b, :, h * HEAD_DIM:h * HEAD_DIM + 1] for b in range(nbs) for h in range(HEADS)])
    grow = jnp.stack([grow_all[b][HEADS + h:HEADS + h + 1, :] for b in range(nbs) for h in range(HEADS)])
    decay = jnp.where(tril, jnp.exp(jnp.minimum(gcol - grow, 0.0)), 0.0)
    qk = qkk[:, :chunk] * decay
    lm = jnp.where(strict, qkk[:, chunk:] * decay, 0.0)
    tinv = _unit_lower_inverse(lm, row, col, in_rows)
    rhs = jnp.concatenate([heads(beta_f * vf), heads(bk * e_g)], axis=2)
    uw = _bmm(tinv.astype(BF16), rhs)
    u = uw[:, :, :HEAD_DIM]
    wk = uw[:, :, HEAD_DIM:]
    s = s_sc[...]
    ws = _bmm(jnp.concatenate([wk.astype(BF16), heads(qn * e_g)], axis=1),
              s.astype(BF16))
    wnb = (u - ws[:, :chunk]).astype(BF16)
    o = ws[:, chunk:] + _bmm(qk.astype(BF16), wnb)
    s_sc[...] = _split_heads(jnp.exp(glast), 0) * s + _bmm_tn(heads(kn * e_gl), wnb)
    of = _merge_heads(o, nbs)
    of = of * lax.rsqrt(_group_sum(of * of, bd) * (1.0 / HEAD_DIM) + NORM_EPS) * og * gate
    o_ref[...] = of.reshape(nbs, chunk, WIDTH)[:, 0:in_rows, :]

    @pl.when(c == pl.num_programs(1) - 1)
    def _():
        _store_state(sout_ref, prev_ref, s_sc[...].reshape(nbs, HEADS, HEAD_DIM, HEAD_DIM))


def _rwkv_kernel(*refs, nbs, chunk, in_rows, n_x, n_consts, has_prev, carry, prepped):
    ((pc_ref,), s0_ref, sh_ref, (vec_ref, w2_ref, a2_ref, g2_ref, bd_ref), prev_ref,
     o_ref, sout_ref, st_sc, xs, (stage_c,)) = _unpack_refs(refs, n_x, n_consts, has_prev, in_rows < chunk)
    c = pl.program_id(1)
    rows = nbs * chunk

    @pl.when(c == 0)
    def _():
        st_sc[...] = s0_ref[...].reshape(nbs * HEADS, HEAD_DIM, HEAD_DIM)
        if not prepped:
            xs[:, SUBLANES - 1:SUBLANES, :] = sh_ref[...]

    vec = vec_ref[...]
    mu = vec[:, 0:C_PROJ]
    w0, a0, k_k, k_a, r_k, ln_g, ln_b = (vec[:, C_PROJ + i * WIDTH:C_PROJ + (i + 1) * WIDTH] for i in range(7))
    bd = bd_ref[...]
    if prepped:
        xp = pc_ref[...].reshape(rows, 8 * WIDTH)
        r, k2, v, kkn, bf, logw, gg, bonus = (xp[:, i * WIDTH:(i + 1) * WIDTH] for i in range(8))
    else:
        cp = _stage_rows(pc_ref, stage_c, in_rows)
        xs[:, SUBLANES:SUBLANES + chunk, :] = cp
        prev = xs[:, SUBLANES - 1:SUBLANES - 1 + chunk, :]
        if carry:
            xs[:, 0:SUBLANES, :] = xs[:, chunk:chunk + SUBLANES, :]
        xm = (cp + (prev - cp) * mu).reshape(rows, C_PROJ)
        r = xm[:, 0:WIDTH]
        k = xm[:, WIDTH:2 * WIDTH]
        v = xm[:, 2 * WIDTH:3 * WIDTH]
        wl = xm[:, 3 * WIDTH:3 * WIDTH + 64]
        al = xm[:, 3 * WIDTH + 64:3 * WIDTH + 128]
        gl = xm[:, 3 * WIDTH + 128:4 * WIDTH]
        w_log = -_softplus(-(w0 + _dot(jnp.tanh(wl).astype(BF16), w2_ref[...]))) - 0.5
        logw = -jnp.exp(w_log)
        a = jax.nn.sigmoid(a0 + _dot(al.astype(BF16), a2_ref[...]))
        gg = _dot(jax.nn.sigmoid(gl).astype(BF16), g2_ref[...])
        kkraw = k * k_k
        k2 = k * (1.0 + (a - 1.0) * k_a)
        bonus = _group_sum(r * k2 * r_k, bd) * v
        if in_rows < chunk:
            t_in_chunk = lax.broadcasted_iota(jnp.int32, (nbs, chunk, WIDTH), 1).reshape(rows, WIDTH)
            valid = t_in_chunk < in_rows
            logw = jnp.where(valid, logw, 0.0)
            kkraw = jnp.where(valid, kkraw, 0.0)
            k2 = jnp.where(valid, k2, 0.0)
        kkn = kkraw * lax.rsqrt(_dot((kkraw * kkraw).astype(BF16), bd) + L2_EPS)
        bf = kkn * a

    row = lax.broadcasted_iota(jnp.int32, (chunk, chunk), 0)
    col = lax.broadcasted_iota(jnp.int32, (chunk, chunk), 1)
    tril = row >= col
    strict = row > col
    trilb = tril.astype(BF16)

    def heads(x):
        return _split_heads(x.reshape(nbs, chunk, WIDTH).astype(BF16), 0)

    logw3 = logw.reshape(nbs, chunk, WIDTH)
    glog3 = jnp.stack([_cumsum_rows(trilb, logw3[b]) for b in range(nbs)])
    glast = glog3[:, chunk - 1:chunk, :]
    glog = glog3.reshape(rows, WIDTH)
    e_n = jnp.exp(-glog)
    e_l = jnp.exp(glast - glog3).reshape(rows, WIDTH)
    lhs = jnp.concatenate([heads(kkn * jnp.exp(glog - logw)), heads(r * jnp.exp(glog))], axis=1)
    rhs = jnp.concatenate([heads(bf * e_n), heads(k2 * e_n)], axis=1)
    pair = _bmm_nt(lhs, rhs)
    st = st_sc[...]
    ls = _bmm(lhs, st.astype(BF16))
    lb = jnp.where(strict, pair[:, :chunk, :chunk], 0.0)
    lk = jnp.where(strict, pair[:, :chunk, chunk:], 0.0)
    arb = jnp.where(tril, pair[:, chunk:, :chunk], 0.0)
    ark = jnp.where(tril, pair[:, chunk:, chunk:], 0.0)
    tinv = _unit_lower_inverse(lb, row, col, in_rows)
    vhb = heads(v)
    ub = _bmm(tinv.astype(BF16), (ls[:, :chunk] + _bmm(lk.astype(BF16), vhb)).astype(BF16)).astype(BF16)
    y = ls[:, chunk:] + _bmm(jnp.concatenate([ark, -arb], axis=2).astype(BF16),
                             jnp.concatenate([vhb, ub], axis=1))
    upd = _bmm_tn(jnp.concatenate([heads(k2 * e_l), heads(bf * e_l)], axis=1),
                  jnp.concatenate([vhb, -ub], axis=1))
    e_last = jnp.exp(glast)
    decay_cols = []
    for b in range(nbs):
        halves = [_transpose_rows(e_last[b, :, j * GATE_LANES:(j + 1) * GATE_LANES]) for j in range(2)]
        for h in range(HEADS):
            lo = (h % 2) * HEAD_DIM
            decay_cols.append(halves[h // 2][lo:lo + HEAD_DIM, 0:1])
    st_sc[...] = jnp.stack(decay_cols) * st + upd
    yf = _merge_heads(y, nbs)
    dev = yf - _group_sum(yf, bd) * (1.0 / HEAD_DIM)
    var = _group_sum(dev * dev, bd) * (1.0 / HEAD_DIM)
    yn = dev * lax.rsqrt(var + GN_EPS) * ln_g + ln_b
    out = (yn + bonus) * gg
    o_ref[...] = out.reshape(nbs, chunk, WIDTH)[:, 0:in_rows, :]

    @pl.when(c == pl.num_programs(1) - 1)
    def _():
        _store_state(sout_ref, prev_ref, st_sc[...].reshape(nbs, HEADS, HEAD_DIM, HEAD_DIM))


def _round_up(x, m):
    return (x + m - 1) // m * m


def _pool_offsets(stride):
    offs = []
    a = 0
    for k in range(len(POOL_WINDOWS)):
        a = _round_up(a + (1 << k) * stride, SUBLANES)
        offs.append(a)
    return offs


def _pool_kernel(hist_ref, dp_ref, wbd_ref, scale_ref, out_ref, s1, s2, *, rows, stride, start):
    offs = _pool_offsets(stride)
    d0 = offs[-1]
    n = d0 + rows
    dp = dp_ref[...]
    s1[0:d0, :] = hist_ref[0]
    s1[d0:n, :] = dp
    lane = lax.broadcasted_iota(jnp.int32, (1, WIDTH), 1)
    src, dst = s1, s2
    for k, a in enumerate(offs):
        sh = (1 << k) * stride
        shifted = jnp.where(lane >= k * HEAD_DIM, src[a - sh:n - sh, :], 0.0)
        if k < len(offs) - 1:
            dst[a:n, :] = src[a:n, :] + shifted
            src, dst = dst, src
        else:
            sums = src[a:n, :] + shifted
    assert stride & (stride - 1) == 0
    pos = start + (lax.broadcasted_iota(jnp.int32, (rows, WIDTH), 0) >> (stride.bit_length() - 1))
    window = jnp.left_shift(2, lax.broadcasted_iota(jnp.int32, (rows, WIDTH), 1) >> 6)
    cnt = jnp.minimum(pos + 1, window).astype(F32)
    diff = sums / cnt - dp
    out_ref[...] = _dot(diff.astype(BF16), wbd_ref[...]) * scale_ref[...]


def _pool(hist, dp, wbd, scale, l, *, nseq, rows, stride, start):
    d0 = _pool_offsets(stride)[-1]
    return pl.pallas_call(
        functools.partial(_pool_kernel, rows=rows, stride=stride, start=start),
        grid=(nseq,),
        in_specs=[pl.BlockSpec((1, d0, WIDTH), lambda b: (b, 0, 0)),
                  pl.BlockSpec((rows, WIDTH), lambda b: (b, 0)),
                  _layer_block(wbd, l), _layer_block(scale, l)],
        out_specs=pl.BlockSpec((rows, WIDTH), lambda b: (b, 0)),
        out_shape=jax.ShapeDtypeStruct((nseq * rows, WIDTH), F32),
        scratch_shapes=[pltpu.VMEM((d0 + rows, WIDTH), F32)] * 2,
        compiler_params=pltpu.CompilerParams(dimension_semantics=("parallel",)),
        name="pool",
    )(hist, dp, wbd, scale)


def _ffn_kernel(x_ref, ma_ref, mb_ref, mc_ref, md_ref, wo_ref, g2_ref, wu_ref, wd_ref, gf_ref,
                o_ref, *, tf, final):
    mixed = None
    for i, m_ref in enumerate((ma_ref, mb_ref, mc_ref, md_ref)):
        part = _dot(m_ref[...].astype(BF16), wo_ref[i * WIDTH:(i + 1) * WIDTH, :])
        mixed = part if mixed is None else mixed + part
    x = x_ref[...] + mixed
    hm = _rms(x, g2_ref[...]).astype(BF16)
    down = None
    for j in range(D_FF // tf):
        up = jnp.maximum(_dot(hm, wu_ref[:, j * tf:(j + 1) * tf]), 0.0)
        part = _dot((up * up).astype(BF16), wd_ref[j * tf:(j + 1) * tf, :])
        down = part if down is None else down + part
    x = x + down
    if final:
        x = _rms(x, gf_ref[...])
    o_ref[...] = x


def _ffn(x, mixed, wo, g2, wu, wd, gf, l):
    t = x.shape[0]
    tm = min(ROW_TILE, t)
    row = lambda i: (i, 0)
    return pl.pallas_call(
        functools.partial(_ffn_kernel, tf=1024, final=l == DEPTH - 1),
        grid=(t // tm,),
        in_specs=[pl.BlockSpec((tm, D_MODEL), row)] + [pl.BlockSpec((tm, WIDTH), row)] * 4
        + [_layer_block(wo, l, single_buffer=True), _layer_block(g2, l),
           _layer_block(wu, l, single_buffer=True), _layer_block(wd, l, single_buffer=True), _whole(gf)],
        out_specs=pl.BlockSpec((tm, D_MODEL), row),
        out_shape=jax.ShapeDtypeStruct((t, D_MODEL), F32),
        compiler_params=pltpu.CompilerParams(dimension_semantics=("parallel",),
                                             vmem_limit_bytes=VMEM_LIMIT),
        name="ffn",
    )(x, *mixed, wo, g2, wu, wd, gf)


def _rows(v):
    return v[:, None, :]


def kernel(x_prompt, x_sample, state_b_conv, state_b_ssm, state_c_shift, state_c_wkv, state_d_pool, norm1_g, w_in, a_ws, a_bs, a_vnorm_g, b_conv_w, b_a_log, b_dt_bias, b_onorm_g, c_mu, c_w0, c_w2, c_a0, c_a2, c_g2, c_k_k, c_k_a, c_r_k, c_ln_g, c_ln_b, d_w, d_scale, w_out, norm2_g, w_up, w_down, final_g):
    nb, seq, _ = x_prompt.shape
    ns, dseq, _ = x_sample.shape
    a_chunk = a_ws.shape[-1]

    a_end = 2 * WIDTH
    b_end = a_end + 4 * WIDTH
    c_off = b_end + 2 * HEADS
    d_off = c_off + C_PROJ
    w_in_b = w_in.astype(BF16)
    w_cols = [w_in_b[:, :, :a_end], w_in_b[:, :, a_end:b_end], w_in_b[:, :, c_off:d_off], w_in_b[:, :, d_off:],
              jnp.pad(w_in_b[:, :, b_end:c_off], ((0, 0), (0, 0), (0, GATE_LANES - 2 * HEADS)))]
    wo = w_out.astype(BF16)
    wu = w_up.astype(BF16)
    wd = w_down.astype(BF16)
    g1 = _rows(norm1_g)
    g2 = _rows(norm2_g)
    gf = final_g[None, :]
    vg = _rows(a_vnorm_g)
    causal = jnp.tril(jnp.ones((a_chunk, a_chunk), dtype=bool))
    wm = jnp.where(causal, a_ws, 0.0)
    bias_t = jnp.repeat(jnp.transpose(a_bs, (0, 2, 1)), HEAD_DIM, axis=2)
    mm_p = jnp.transpose(wm, (0, 2, 1, 3)).reshape(DEPTH, a_chunk, HEADS * a_chunk).astype(BF16)
    srow = jnp.arange(ns * dseq)
    step_onehot = (srow[:, None] % dseq == jnp.arange(dseq)[None, :]).astype(F32)
    same_seq = (srow[:, None] // dseq) == (srow[None, :] // dseq)
    mm_s = jnp.einsum('it,lhts,js->lihj', step_onehot, wm[:, :, :dseq, :dseq], step_onehot,
                      precision=lax.Precision.HIGHEST)
    mm_s = jnp.where(same_seq[None, :, None, :], mm_s, 0.0).reshape(
        DEPTH, ns * dseq, HEADS * ns * dseq).astype(BF16)
    bias_s = jnp.tile(bias_t[:, :dseq], (1, ns, 1))
    lead = jnp.zeros((DEPTH, HEADS), F32)
    tail = jnp.zeros((DEPTH, GATE_LANES - 2 * HEADS), F32)
    gvec = jnp.concatenate([lead, b_a_log, tail, lead, b_dt_bias, tail, jnp.tile(b_onorm_g, (1, HEADS))],
                           axis=1)[:, None, :]
    cvec = jnp.concatenate([c_mu, c_w0, c_a0, c_k_k, c_k_a, c_r_k.reshape(DEPTH, WIDTH), c_ln_g, c_ln_b],
                           axis=1)[:, None, :]
    c_w2b, c_a2b, c_g2b = c_w2.astype(BF16), c_a2.astype(BF16), c_g2.astype(BF16)
    n_groups = len(POOL_WINDOWS)
    wbd = jnp.einsum('lgcd,gh->lgchd', d_w, jnp.eye(n_groups, dtype=F32)).reshape(
        DEPTH, WIDTH, WIDTH).astype(BF16)
    dscale = _rows(d_scale)
    lane_head = jnp.arange(WIDTH) // HEAD_DIM
    gate_lane = jnp.arange(GATE_LANES)
    bd = (lane_head[:, None] == lane_head[None, :]).astype(BF16)
    eb = jnp.tile((gate_lane[:, None] == lane_head[None, :]).astype(BF16), (2, 1))
    eg = jnp.tile((gate_lane[:, None] == lane_head[None, :] + HEADS).astype(BF16), (3, 1))

    def run_group(x, l, grp, prev_b, prev_c):
        nseq, length = grp["nseq"], grp["length"]
        conv_state = grp["b_conv"][grp["state_layer"](l)]
        if grp["prep"]:
            a_out, a_v, pb, pc, pd, pg, tails, tails_c = _inproj(
                x, g1, w_cols, grp["mm"], grp["bias"], vg, l,
                gdn_prep=(b_conv_w, gvec, bd, conv_state, length, cvec, c_w2b, c_a2b, c_g2b,
                          grp["c_shift"][grp["state_layer"](l)]))
            b_rows = tails.reshape(nseq, -1, SUBLANES, B_QKV)[:, -1]
            c_last = tails_c.reshape(nseq, -1, SUBLANES, C_PROJ)[:, -1, SUBLANES - 1]
        else:
            a_out, a_v, pb, pc, pd, pg = _inproj(x, g1, w_cols, grp["mm"], grp["bias"], vg, l)
            b_rows = pb.reshape(nseq, length, 4 * WIDTH)[:, :, :B_QKV]
            c_last = pc.reshape(nseq, length, C_PROJ)[:, length - 1]
        b_tail = jnp.concatenate([conv_state, b_rows], axis=1)[:, -(B_CONV - 1):]
        pb3 = pb.reshape(nseq, length, 4 * WIDTH)
        pc3 = pc.reshape(nseq, length, pc.shape[1])
        b_out, b_state = _recurrent_call(
            _gdn_kernel, "gdn", [pb3, pg.reshape(nseq, length, GATE_LANES)], grp["b_ssm"], grp["state_layer"](l),
            grp["b_conv"], [(b_conv_w, l), (gvec, l), (bd, None), (eb, None), (eg, None)], prev_b,
            nbs=grp["nbs"], chunk=grp["chunk"], carry_width=B_QKV, prepped=grp["prep"])
        c_out, c_state = _recurrent_call(
            _rwkv_kernel, "rwkv", [pc3], grp["c_wkv"], grp["state_layer"](l), grp["c_shift"],
            [(cvec, l), (c_w2b, l), (c_a2b, l), (c_g2b, l), (bd, None)], prev_c,
            nbs=grp["nbs"], chunk=grp["chunk"], carry_width=C_PROJ, prepped=grp["prep"])
        d_out = grp["pool"](pd, l)
        x = _ffn(x, (a_out, b_out.reshape(nseq * length, WIDTH), c_out.reshape(nseq * length, WIDTH), d_out),
                 wo, g2, wu, wd, gf, l)
        return x, a_v, b_tail, c_last, pd.reshape(nseq, length, WIDTH), b_state, c_state

    d0 = _pool_offsets(1)[-1]
    zero_hist = jnp.zeros((nb, d0, WIDTH), F32)
    prompt = dict(
        nseq=nb, length=seq, nbs=PROMPT_SEQS_PER_STEP, chunk=MIX_CHUNK, mm=mm_p, bias=bias_t,
        b_ssm=jnp.zeros((1, nb, HEADS, HEAD_DIM, HEAD_DIM), F32), c_wkv=jnp.zeros((1, nb, HEADS, HEAD_DIM, HEAD_DIM), F32),
        b_conv=jnp.zeros((1, nb, B_CONV - 1, B_QKV), F32), c_shift=jnp.zeros((1, nb, 1, C_PROJ), F32),
        state_layer=lambda l: 0, prep=True,
        pool=lambda pd, l: _pool(zero_hist, pd, wbd, dscale, l, nseq=nb, rows=seq, stride=1, start=0))

    def sample_pool(pd, l):
        pd_t = jnp.transpose(pd.reshape(ns, dseq, WIDTH), (1, 0, 2)).reshape(dseq * ns, WIDTH)
        hist = jnp.transpose(state_d_pool[l], (1, 0, 2)).reshape(1, POOL_BUF * ns, WIDTH)
        out_t = _pool(hist, pd_t, wbd, dscale, l, nseq=1, rows=dseq * ns, stride=ns, start=PAST_LEN)
        return jnp.transpose(out_t.reshape(dseq, ns, WIDTH), (1, 0, 2)).reshape(ns * dseq, WIDTH)

    sample = dict(
        nseq=ns, length=dseq, nbs=SAMPLE_SEQS_PER_STEP, chunk=SUBLANES, mm=mm_s, bias=bias_s,
        b_ssm=state_b_ssm, c_wkv=state_c_wkv, b_conv=state_b_conv, c_shift=state_c_shift[:, :, None, :],
        state_layer=lambda l: l, prep=False, pool=sample_pool)

    xp = x_prompt.reshape(nb * seq, D_MODEL)
    xs = x_sample.reshape(ns * dseq, D_MODEL)
    p_bc, p_cs, p_dp, s_av, s_bc, s_cs, s_dp = [], [], [], [], [], [], []
    p_bs = p_cw = s_bs = s_cw = None
    for l in range(DEPTH):
        xp, _, b_tail, c_last, pd3, p_bs, p_cw = run_group(xp, l, prompt, p_bs, p_cw)
        p_bs = p_bs if l else p_bs[None]
        p_cw = p_cw if l else p_cw[None]
        p_bc.append(b_tail)
        p_cs.append(c_last)
        p_dp.append(pd3[:, seq - POOL_BUF:])

        xs, a_v, b_tail, c_last, pd3, s_bs, s_cw = run_group(xs, l, sample, s_bs, s_cw)
        s_bs = s_bs if l else s_bs[None]
        s_cw = s_cw if l else s_cw[None]
        s_av.append(a_v.reshape(ns, dseq, WIDTH))
        s_bc.append(b_tail)
        s_cs.append(c_last)
        s_dp.append(jnp.concatenate([state_d_pool[l], pd3], axis=1)[:, -POOL_BUF:])

    return (xp.reshape(nb, seq, D_MODEL), xs.reshape(ns, dseq, D_MODEL),
            jnp.stack(p_bc), p_bs, jnp.stack(p_cs), p_cw, jnp.stack(p_dp),
            jnp.stack(s_av), jnp.stack(s_bc), s_bs, jnp.stack(s_cs), s_cw, jnp.stack(s_dp))
```

```python
import functools

import jax
import jax.numpy as jnp
from jax import lax
from jax.experimental import pallas as pl
from jax.experimental.pallas import tpu as pltpu

F32 = jnp.float32
BF16 = jnp.bfloat16

D_MODEL = 1024
DEPTH = 2
HEADS = 4
HEAD_DIM = 64
HEAD_SHIFT = HEAD_DIM.bit_length() - 1
WIDTH = HEADS * HEAD_DIM
B_QKV = 3 * WIDTH
B_CONV = 4
C_PROJ = 4 * WIDTH
C_DECAY_LORA = 64
C_RATE_LORA = 64
C_GATE_LORA = 128
POOL_WINDOWS = (2, 4, 8, 16)
POOL_BUF = max(POOL_WINDOWS) - 1
POOL_GROUP = WIDTH // len(POOL_WINDOWS)
POOL_GROUP_SHIFT = POOL_GROUP.bit_length() - 1
D_FF = 4 * D_MODEL
PAST_LEN = 16384
NORM_EPS = 1e-6
L2_EPS = 1e-6
GN_EPS = 64e-5
V7X_LANES = 128
V7X_SUBLANES = 8
V7X_VMEM_BYTES = 64 * 1024 * 1024
GATE_LANES = V7X_LANES
SUBLANES = V7X_SUBLANES
VMEM_LIMIT = V7X_VMEM_BYTES * 3 // 4
MIX_CHUNK = 64
PROMPT_SEQS_PER_STEP = 8
SAMPLE_SEQS_PER_STEP = 32
ROW_TILE = 512
FF_TILE = 1024


def _rms(x, g):
    return x * lax.rsqrt(jnp.mean(x * x, axis=-1, keepdims=True) + NORM_EPS) * g


def _softplus(x):
    return jnp.maximum(x, 0.0) + jnp.log1p(jnp.exp(-jnp.abs(x)))


def _dot(a, b):
    return jnp.dot(a, b, preferred_element_type=F32)


def _bmm(a, b):
    return jnp.einsum('bij,bjk->bik', a, b, preferred_element_type=F32)


def _bmm_nt(a, b):
    return jnp.einsum('bik,bjk->bij', a, b, preferred_element_type=F32)


def _bmm_tn(a, b):
    return jnp.einsum('bki,bkj->bij', a, b, preferred_element_type=F32)


def _layer_block(arr, l, single_buffer=False):
    shape = arr.shape[1:]
    index = lambda *_: (l,) + (0,) * len(shape)
    if single_buffer:
        return pl.BlockSpec((None,) + shape, index, pipeline_mode=pl.Buffered(1))
    return pl.BlockSpec((None,) + shape, index)


def _whole(arr):
    return pl.BlockSpec(arr.shape, lambda *_: (0,) * arr.ndim)


def _split_heads(x, lane0):
    return jnp.stack([x[b, :, lane0 + h * HEAD_DIM:lane0 + (h + 1) * HEAD_DIM]
                      for b in range(x.shape[0]) for h in range(HEADS)])


def _merge_heads(x, nbs):
    return jnp.concatenate([jnp.concatenate([x[b * HEADS + h] for h in range(HEADS)], axis=1)
                            for b in range(nbs)], axis=0)


def _group_sum(x, bd):
    hi = x.astype(BF16)
    lo = (x - hi.astype(F32)).astype(BF16)
    return _dot(hi, bd) + _dot(lo, bd)


def _split3(x):
    p1 = x.astype(BF16)
    r1 = x - p1.astype(F32)
    p2 = r1.astype(BF16)
    return p1, p2, (r1 - p2.astype(F32)).astype(BF16)


def _cumsum_rows(tril_b, x):
    w = x.shape[1]
    y = _dot(tril_b, jnp.concatenate(_split3(x), axis=1))
    return y[:, :w] + (y[:, w:2 * w] + y[:, 2 * w:])


def _unit_lower_inverse(lm, row, col, n_valid):
    def sub_diag_block(shift):
        return (((row >> (shift + 1)) == (col >> (shift + 1)))
                & (((row >> shift) & 1) == 1) & (((col >> shift) & 1) == 0))

    m = (row == col).astype(F32) - jnp.where(sub_diag_block(0), lm, 0.0)
    shift = 1
    while (1 << shift) < n_valid:
        cs = jnp.where(sub_diag_block(shift), lm, 0.0).astype(BF16)
        mb = m.astype(BF16)
        m = m - _bmm(_bmm(mb, cs).astype(BF16), mb)
        shift += 1
    return m


def _lora_inputs(x, lane0):
    a = lane0 + C_DECAY_LORA
    b = a + C_RATE_LORA
    return x[:, lane0:a], x[:, a:b], x[:, b:b + C_GATE_LORA]


def _transpose_rows(a):
    n = a.shape[0]
    if n < GATE_LANES:
        a = jnp.concatenate([a, jnp.zeros((GATE_LANES - n, a.shape[1]), a.dtype)], axis=0)
    return a.T[:, :n]


def _stage_rows(ref, stage, in_rows):
    if stage is None:
        return ref[...]
    stage[...] = jnp.zeros(stage.shape, stage.dtype)
    stage[:, 0:in_rows, :] = ref[...]
    return stage[...]


def _conv_silu(raw, hist, tail_ref, lo, cw):
    tm = raw.shape[0]
    cols = slice(lo, lo + WIDTH)
    xfull = jnp.concatenate([hist[:, cols], raw], axis=0)
    conv = pltpu.roll(xfull, 3, axis=0)[SUBLANES:, :] * cw[0:1, cols]
    conv = conv + pltpu.roll(xfull, 2, axis=0)[SUBLANES:, :] * cw[1:2, cols]
    conv = conv + pltpu.roll(xfull, 1, axis=0)[SUBLANES:, :] * cw[2:3, cols]
    conv = conv + raw * cw[3:4, cols]
    tail = raw[tm - SUBLANES:tm, :]
    hist[:, cols] = tail
    tail_ref[:, cols] = tail
    return jax.nn.silu(conv)


def _inproj_kernel(*refs, r, prep, tiles_per_seq):
    (x_ref, g_ref, wa_ref, wb_ref, wc_ref, wd_ref, wg_ref, mm_ref, bias_ref, vg_ref) = refs[:10]
    if prep:
        cw_ref, gvec_ref, bd_ref, cbuf_ref, cvec_ref, w2_ref, a2_ref, g2_ref, sh_ref = refs[10:19]
        a_ref, v_ref, pb_ref, pc_ref, pd_ref, pg_ref, tail_ref, tailc_ref, hist, hist_c = refs[19:]
    else:
        a_ref, v_ref, pb_ref, pc_ref, pd_ref, pg_ref = refs[10:]
    h = _rms(x_ref[...], g_ref[...]).astype(BF16)

    def gmlp_gates(pa):
        v = _rms(jax.nn.gelu(pa[:, WIDTH:]), vg_ref[...])
        v_ref[...] = v
        return jax.nn.gelu(pa[:, :WIDTH]), v

    def gmlp_mix(u, v):
        lane_head = lax.broadcasted_iota(jnp.int32, (r, WIDTH), 1) >> HEAD_SHIFT
        mm = mm_ref[...]
        bias = bias_ref[...]
        for i in range(u.shape[0] // r):
            vc = v[i * r:(i + 1) * r]
            per_head = jnp.concatenate([jnp.where(lane_head == h, vc, 0.0).astype(BF16)
                                        for h in range(HEADS)], axis=0)
            a_ref[i * r:(i + 1) * r, :] = u[i * r:(i + 1) * r] * (_dot(mm, per_head) + bias)

    if prep:
        n_hist = B_CONV - 1

        @pl.when(pl.program_id(0) % tiles_per_seq == 0)
        def _():
            hist[...] = jnp.zeros(hist.shape, F32)
            hist[SUBLANES - n_hist:SUBLANES, :] = cbuf_ref[0]
            hist_c[...] = jnp.zeros(hist_c.shape, F32)
            hist_c[SUBLANES - 1:SUBLANES, :] = sh_ref[0]

        cvec = cvec_ref[...]
        mu = cvec[:, 0:C_PROJ]
        w0, a0, k_k, k_a, r_k = (cvec[:, C_PROJ + i * WIDTH:C_PROJ + (i + 1) * WIDTH] for i in range(5))

        def shifted_lerp(cp, lo):
            cols = slice(lo, lo + WIDTH)
            prev = pltpu.roll(jnp.concatenate([hist_c[:, cols], cp], axis=0), 1, axis=0)[SUBLANES:, :]
            tail = cp[cp.shape[0] - SUBLANES:, :]
            hist_c[:, cols] = tail
            tailc_ref[:, cols] = tail
            return cp + (prev - cp) * mu[:, cols]

        cw = cw_ref[...]
        bd = bd_ref[...]
        gvec = gvec_ref[...]
        raw_q = _dot(h, wb_ref[:, 0:WIDTH])
        raw_k = _dot(h, wb_ref[:, WIDTH:2 * WIDTH])
        raw_v = _dot(h, wb_ref[:, 2 * WIDTH:3 * WIDTH])
        qf = _conv_silu(raw_q, hist, tail_ref, 0, cw)
        pb_ref[:, 0:WIDTH] = qf * lax.rsqrt(_dot((qf * qf).astype(BF16), bd) + L2_EPS) * (HEAD_DIM ** -0.5)
        raw_gate = _dot(h, wb_ref[:, 3 * WIDTH:4 * WIDTH])
        pg = _dot(h, wg_ref[...])
        kf = _conv_silu(raw_k, hist, tail_ref, WIDTH, cw)
        pb_ref[:, WIDTH:2 * WIDTH] = kf * lax.rsqrt(_dot((kf * kf).astype(BF16), bd) + L2_EPS)
        cp_l = _dot(h, wc_ref[:, 3 * WIDTH:4 * WIDTH])
        cp_k = _dot(h, wc_ref[:, WIDTH:2 * WIDTH])
        pb_ref[:, 2 * WIDTH:3 * WIDTH] = _conv_silu(raw_v, hist, tail_ref, 2 * WIDTH, cw)
        pa = _dot(h, wa_ref[...])
        pb_ref[:, 3 * WIDTH:4 * WIDTH] = jax.nn.silu(raw_gate)
        alog = gvec[:, 0:GATE_LANES]
        dtb = gvec[:, GATE_LANES:2 * GATE_LANES]
        lane = lax.broadcasted_iota(jnp.int32, pg.shape, 1)
        pg_ref[...] = jnp.where(lane < HEADS, jax.nn.sigmoid(pg), -jnp.exp(alog) * _softplus(pg + dtb))
        cp_r = _dot(h, wc_ref[:, 0:WIDTH])
        cp_v = _dot(h, wc_ref[:, 2 * WIDTH:3 * WIDTH])
        xm_l = shifted_lerp(cp_l, 3 * WIDTH)
        wl, al, gl = _lora_inputs(xm_l, 0)
        w_log = -_softplus(-(w0 + _dot(jnp.tanh(wl).astype(BF16), w2_ref[...]))) - 0.5
        pc_ref[:, 5 * WIDTH:6 * WIDTH] = -jnp.exp(w_log)
        a = jax.nn.sigmoid(a0 + _dot(al.astype(BF16), a2_ref[...]))
        pc_ref[:, 6 * WIDTH:7 * WIDTH] = _dot(jax.nn.sigmoid(gl).astype(BF16), g2_ref[...])
        pd_ref[...] = _dot(h, wd_ref[...])
        k = shifted_lerp(cp_k, WIDTH)
        kkraw = k * k_k
        kkn = kkraw * lax.rsqrt(_dot((kkraw * kkraw).astype(BF16), bd) + L2_EPS)
        k2 = k * (1.0 + (a - 1.0) * k_a)
        pc_ref[:, WIDTH:2 * WIDTH] = k2
        pc_ref[:, 3 * WIDTH:4 * WIDTH] = kkn
        pc_ref[:, 4 * WIDTH:5 * WIDTH] = kkn * a
        rr = shifted_lerp(cp_r, 0)
        vv = shifted_lerp(cp_v, 2 * WIDTH)
        pc_ref[:, 0:WIDTH] = rr
        pc_ref[:, 2 * WIDTH:3 * WIDTH] = vv
        pc_ref[:, 7 * WIDTH:8 * WIDTH] = _group_sum(rr * k2 * r_k, bd) * vv
        gmlp_mix(*gmlp_gates(pa))
    else:
        pa = _dot(h, wa_ref[...])
        pb_ref[...] = _dot(h, wb_ref[...])
        u, v = gmlp_gates(pa)
        pc_ref[...] = _dot(h, wc_ref[...])
        pd_ref[...] = _dot(h, wd_ref[...])
        pg_ref[...] = _dot(h, wg_ref[...])
        gmlp_mix(u, v)


def _inproj(x, g, ws, mm, bias, vg, l, token_prep=None):
    t = x.shape[0]
    r = mm.shape[1]
    tm = max(r, min(ROW_TILE, t))
    widths = [WIDTH, WIDTH] + [w.shape[2] for w in ws[1:]]
    in_specs = ([pl.BlockSpec((tm, D_MODEL), lambda i: (i, 0)), _layer_block(g, l)]
                + [_layer_block(w, l, single_buffer=True) for w in ws]
                + [_layer_block(mm, l), _layer_block(bias, l), _layer_block(vg, l)])
    operands = [x, g, *ws, mm, bias, vg]
    out_specs = [pl.BlockSpec((tm, n), lambda i: (i, 0)) for n in widths]
    out_shape = [jax.ShapeDtypeStruct((t, n), F32) for n in widths]
    scratch = []
    tiles_per_seq = 1
    if token_prep is not None:
        conv_w, gvec, bd, conv_state, seq_len, cvec, w2, a2, g2, shift_state = token_prep
        tiles_per_seq = seq_len // tm
        per_seq = lambda i: (i // tiles_per_seq, 0, 0)
        in_specs += [_layer_block(conv_w, l), _layer_block(gvec, l), _whole(bd),
                     pl.BlockSpec((1,) + conv_state.shape[1:], per_seq),
                     _layer_block(cvec, l), _layer_block(w2, l), _layer_block(a2, l), _layer_block(g2, l),
                     pl.BlockSpec((1,) + shift_state.shape[1:], per_seq)]
        operands += [conv_w, gvec, bd, conv_state, cvec, w2, a2, g2, shift_state]
        widths[3] = 8 * WIDTH
        out_specs[3] = pl.BlockSpec((tm, 8 * WIDTH), lambda i: (i, 0))
        out_shape[3] = jax.ShapeDtypeStruct((t, 8 * WIDTH), F32)
        for w in (B_QKV, C_PROJ):
            out_specs.append(pl.BlockSpec((SUBLANES, w), lambda i: (i, 0)))
            out_shape.append(jax.ShapeDtypeStruct((t // tm * SUBLANES, w), F32))
            scratch.append(pltpu.VMEM((SUBLANES, w), F32))
    return pl.pallas_call(
        functools.partial(_inproj_kernel, r=r, prep=token_prep is not None, tiles_per_seq=tiles_per_seq),
        grid=(t // tm,),
        in_specs=in_specs,
        out_specs=out_specs,
        out_shape=out_shape,
        scratch_shapes=scratch,
        compiler_params=pltpu.CompilerParams(
            dimension_semantics=("arbitrary" if token_prep is not None else "parallel",),
            vmem_limit_bytes=VMEM_LIMIT),
        name="inproj",
    )(*operands)


def _store_state(sout_ref, prev_ref, new_state):
    if prev_ref is None:
        sout_ref[...] = new_state
    else:
        n_prev = prev_ref.shape[0]
        sout_ref[0:n_prev] = prev_ref[...]
        sout_ref[n_prev] = new_state


def _recurrent_call(kernel_fn, name, x_blocks, s0, s0_layer, carry_in, consts, prev, *,
                    nbs, chunk, carry_width, out_width=WIDTH, **kernel_flags):
    nseq, length, _ = x_blocks[0].shape
    in_rows = min(chunk, length)
    n_chunks = length // in_rows
    blk = lambda b, c: (b, c, 0)
    state_block = (nbs, HEADS, HEAD_DIM, HEAD_DIM)
    in_specs = [pl.BlockSpec((nbs, in_rows, x.shape[2]), blk) for x in x_blocks]
    in_specs += [pl.BlockSpec((None,) + state_block, lambda b, c: (s0_layer, b, 0, 0, 0)),
                 pl.BlockSpec((None, nbs) + carry_in.shape[2:], lambda b, c: (s0_layer, b, 0, 0))]
    in_specs += [_whole(a) if l is None else _layer_block(a, l) for a, l in consts]
    operands = list(x_blocks) + [s0, carry_in] + [a for a, _ in consts]
    if prev is None:
        state_shape = (nseq, HEADS, HEAD_DIM, HEAD_DIM)
        state_spec = pl.BlockSpec(state_block, lambda b, c: (b, 0, 0, 0))
    else:
        n_prev = prev.shape[0]
        in_specs.append(pl.BlockSpec((n_prev,) + state_block, lambda b, c: (0, b, 0, 0, 0)))
        operands.append(prev)
        state_shape = (n_prev + 1, nseq, HEADS, HEAD_DIM, HEAD_DIM)
        state_spec = pl.BlockSpec((n_prev + 1,) + state_block, lambda b, c: (0, b, 0, 0, 0))
    scratch = [pltpu.VMEM((nbs * HEADS, HEAD_DIM, HEAD_DIM), F32),
               pltpu.VMEM((nbs, chunk + SUBLANES, carry_width), F32)]
    if in_rows < chunk:
        scratch += [pltpu.VMEM((nbs, chunk, x.shape[2]), F32) for x in x_blocks]
    return pl.pallas_call(
        functools.partial(kernel_fn, nbs=nbs, chunk=chunk, in_rows=in_rows, n_x=len(x_blocks),
                          n_consts=len(consts), has_prev=prev is not None, carry=n_chunks > 1,
                          **kernel_flags),
        grid=(nseq // nbs, n_chunks),
        in_specs=in_specs,
        out_specs=[pl.BlockSpec((nbs, in_rows, out_width), blk), state_spec],
        out_shape=[jax.ShapeDtypeStruct((nseq, length, out_width), F32),
                   jax.ShapeDtypeStruct(state_shape, F32)],
        scratch_shapes=scratch,
        compiler_params=pltpu.CompilerParams(dimension_semantics=("parallel", "arbitrary")),
        name=name,
    )(*operands)


def _unpack_refs(refs, n_x, n_consts, has_prev, staged):
    x_refs = refs[:n_x]
    s0_ref, carry_ref = refs[n_x:n_x + 2]
    consts = refs[n_x + 2:n_x + 2 + n_consts]
    pos = n_x + 2 + n_consts
    prev_ref = refs[pos] if has_prev else None
    pos += int(has_prev)
    o_ref, sout_ref, state_sc, rows_sc = refs[pos:pos + 4]
    stages = refs[pos + 4:] if staged else (None,) * n_x
    return x_refs, s0_ref, carry_ref, consts, prev_ref, o_ref, sout_ref, state_sc, rows_sc, stages


def _gdn_kernel(*refs, nbs, chunk, in_rows, n_x, n_consts, has_prev, carry, prepped):
    ((pb_ref, pg_ref), s0_ref, cbuf_ref, (cw_ref, vec_ref, bd_ref, eb_ref, eg_ref), prev_ref,
     o_ref, sout_ref, s_sc, xbuf, (stage_b, stage_g)) = _unpack_refs(refs, n_x, n_consts, has_prev,
                                                                    in_rows < chunk)
    c = pl.program_id(1)
    nb = nbs * HEADS
    rows = nbs * chunk
    n_hist = B_CONV - 1

    @pl.when(c == 0)
    def _():
        s_sc[...] = s0_ref[...].reshape(nb, HEAD_DIM, HEAD_DIM)
        if not prepped:
            xbuf[:, 0:SUBLANES, :] = jnp.zeros((nbs, SUBLANES, B_QKV), F32)
            xbuf[:, SUBLANES - n_hist:SUBLANES, :] = cbuf_ref[...]

    pb = _stage_rows(pb_ref, stage_b, in_rows)
    pg = _stage_rows(pg_ref, stage_g, in_rows)
    vec = vec_ref[...]
    og = vec[:, 2 * GATE_LANES:2 * GATE_LANES + WIDTH]
    bd = bd_ref[...]
    if prepped:
        qkv = pb[:, :, 0:B_QKV].reshape(rows, B_QKV)
        gate = pb[:, :, B_QKV:B_QKV + WIDTH].reshape(rows, WIDTH)
        beta_all = g_all = pg
        qn = qkv[:, 0:WIDTH]
        kn = qkv[:, WIDTH:2 * WIDTH]
    else:
        raw = pb[:, :, 0:B_QKV]
        xbuf[:, SUBLANES:SUBLANES + chunk, :] = raw
        cw = cw_ref[...]
        xfull = xbuf[...]
        conv = pltpu.roll(xfull, 3, axis=1)[:, SUBLANES:, :] * cw[0:1]
        conv = conv + pltpu.roll(xfull, 2, axis=1)[:, SUBLANES:, :] * cw[1:2]
        conv = conv + pltpu.roll(xfull, 1, axis=1)[:, SUBLANES:, :] * cw[2:3]
        conv = conv + raw * cw[3:4]
        if carry:
            xbuf[:, 0:SUBLANES, :] = xbuf[:, chunk:chunk + SUBLANES, :]
        qkv = jax.nn.silu(conv).reshape(rows, B_QKV)
        gate = jax.nn.silu(pb[:, :, B_QKV:B_QKV + WIDTH]).reshape(rows, WIDTH)
        alog = vec[:, 0:GATE_LANES]
        dtb = vec[:, GATE_LANES:2 * GATE_LANES]
        beta_all = jax.nn.sigmoid(pg)
        g_all = -jnp.exp(alog) * _softplus(pg + dtb)
        if in_rows < chunk:
            valid = lax.broadcasted_iota(jnp.int32, pg.shape, 1) < in_rows
            beta_all = jnp.where(valid, beta_all, 0.0)
            g_all = jnp.where(valid, g_all, 0.0)
        qf = qkv[:, 0:WIDTH]
        kf = qkv[:, WIDTH:2 * WIDTH]
        qn = qf * lax.rsqrt(_dot((qf * qf).astype(BF16), bd) + L2_EPS) * (HEAD_DIM ** -0.5)
        kn = kf * lax.rsqrt(_dot((kf * kf).astype(BF16), bd) + L2_EPS)
    vf = qkv[:, 2 * WIDTH:]

    row = lax.broadcasted_iota(jnp.int32, (chunk, chunk), 0)
    col = lax.broadcasted_iota(jnp.int32, (chunk, chunk), 1)
    tril = row >= col
    strict = row > col
    trilb = tril.astype(BF16)
    b1, b2, _ = _split3(beta_all.reshape(rows, GATE_LANES))
    beta_f = _dot(jnp.concatenate([b1, b2], axis=1), eb_ref[...])
    gc_small = [_cumsum_rows(trilb, g_all[b]) for b in range(nbs)]
    gc = _dot(jnp.concatenate(_split3(jnp.concatenate(gc_small, axis=0)), axis=1),
              eg_ref[...]).reshape(nbs, chunk, WIDTH)
    glast = gc[:, chunk - 1:chunk, :]
    e_g = jnp.exp(gc).reshape(rows, WIDTH)
    e_gl = jnp.exp(glast - gc).reshape(rows, WIDTH)
    bk = beta_f * kn

    def heads(x):
        return _split_heads(x.reshape(nbs, chunk, WIDTH).astype(BF16), 0)

    qkk = _bmm_nt(jnp.concatenate([heads(qn), heads(bk)], axis=1), heads(kn))
    grow_all = [_transpose_rows(g) for g in gc_small]
    gcol = jnp.stack([gc[b, :, h * HEAD_DIM:h * HEAD_DIM + 1] for b in range(nbs) for h in range(HEADS)])
    grow = jnp.stack([grow_all[b][HEADS + h:HEADS + h + 1, :] for b in range(nbs) for h in range(HEADS)])
    decay = jnp.where(tril, jnp.exp(jnp.minimum(gcol - grow, 0.0)), 0.0)
    qk = qkk[:, :chunk] * decay
    lm = jnp.where(strict, qkk[:, chunk:] * decay, 0.0)
    tinv = _unit_lower_inverse(lm, row, col, in_rows)
    rhs = jnp.concatenate([heads(beta_f * vf), heads(bk * e_g)], axis=2)
    uw = _bmm(tinv.astype(BF16), rhs)
    u = uw[:, :, :HEAD_DIM]
    wk = uw[:, :, HEAD_DIM:]
    s = s_sc[...]
    ws = _bmm(jnp.concatenate([wk.astype(BF16), heads(qn * e_g)], axis=1),
              s.astype(BF16))
    wnb = (u - ws[:, :chunk]).astype(BF16)
    o = ws[:, chunk:] + _bmm(qk.astype(BF16), wnb)
    s_sc[...] = _split_heads(jnp.exp(glast), 0) * s + _bmm_tn(heads(kn * e_gl), wnb)
    of = _merge_heads(o, nbs)
    of = of * lax.rsqrt(_group_sum(of * of, bd) * (1.0 / HEAD_DIM) + NORM_EPS) * og * gate
    o_ref[...] = of.reshape(nbs, chunk, WIDTH)[:, 0:in_rows, :]

    @pl.when(c == pl.num_programs(1) - 1)
    def _():
        _store_state(sout_ref, prev_ref, s_sc[...].reshape(nbs, HEADS, HEAD_DIM, HEAD_DIM))


def _rwkv_kernel(*refs, nbs, chunk, in_rows, n_x, n_consts, has_prev, carry, prepped):
    ((pc_ref,), s0_ref, sh_ref, (vec_ref, w2_ref, a2_ref, g2_ref, bd_ref), prev_ref,
     o_ref, sout_ref, st_sc, xs, (stage_c,)) = _unpack_refs(refs, n_x, n_consts, has_prev, in_rows < chunk)
    c = pl.program_id(1)
    rows = nbs * chunk

    @pl.when(c == 0)
    def _():
        for b in range(nbs):
            for h in range(HEADS):
                st_sc[b * HEADS + h] = s0_ref[b, h].T
        if not prepped:
            xs[:, SUBLANES - 1:SUBLANES, :] = sh_ref[...]

    vec = vec_ref[...]
    mu = vec[:, 0:C_PROJ]
    w0, a0, k_k, k_a, r_k, ln_g, ln_b = (vec[:, C_PROJ + i * WIDTH:C_PROJ + (i + 1) * WIDTH] for i in range(7))
    bd = bd_ref[...]
    if prepped:
        xp = pc_ref[...].reshape(rows, 8 * WIDTH)
        r, k2, v, kkn, bf, logw, gg, bonus = (xp[:, i * WIDTH:(i + 1) * WIDTH] for i in range(8))
    else:
        cp = _stage_rows(pc_ref, stage_c, in_rows)
        xs[:, SUBLANES:SUBLANES + chunk, :] = cp
        prev = xs[:, SUBLANES - 1:SUBLANES - 1 + chunk, :]
        if carry:
            xs[:, 0:SUBLANES, :] = xs[:, chunk:chunk + SUBLANES, :]
        xm = (cp + (prev - cp) * mu).reshape(rows, C_PROJ)
        r = xm[:, 0:WIDTH]
        k = xm[:, WIDTH:2 * WIDTH]
        v = xm[:, 2 * WIDTH:3 * WIDTH]
        wl, al, gl = _lora_inputs(xm, 3 * WIDTH)
        w_log = -_softplus(-(w0 + _dot(jnp.tanh(wl).astype(BF16), w2_ref[...]))) - 0.5
        logw = -jnp.exp(w_log)
        a = jax.nn.sigmoid(a0 + _dot(al.astype(BF16), a2_ref[...]))
        gg = _dot(jax.nn.sigmoid(gl).astype(BF16), g2_ref[...])
        kkraw = k * k_k
        k2 = k * (1.0 + (a - 1.0) * k_a)
        bonus = _group_sum(r * k2 * r_k, bd) * v
        if in_rows < chunk:
            t_in_chunk = lax.broadcasted_iota(jnp.int32, (nbs, chunk, WIDTH), 1).reshape(rows, WIDTH)
            valid = t_in_chunk < in_rows
            logw = jnp.where(valid, logw, 0.0)
            kkraw = jnp.where(valid, kkraw, 0.0)
            k2 = jnp.where(valid, k2, 0.0)
        kkn = kkraw * lax.rsqrt(_dot((kkraw * kkraw).astype(BF16), bd) + L2_EPS)
        bf = kkn * a

    row = lax.broadcasted_iota(jnp.int32, (chunk, chunk), 0)
    col = lax.broadcasted_iota(jnp.int32, (chunk, chunk), 1)
    tril = row >= col
    strict = row > col
    trilb = tril.astype(BF16)

    def heads(x):
        return _split_heads(x.reshape(nbs, chunk, WIDTH).astype(BF16), 0)

    logw3 = logw.reshape(nbs, chunk, WIDTH)
    glog3 = jnp.stack([_cumsum_rows(trilb, logw3[b]) for b in range(nbs)])
    glast = glog3[:, chunk - 1:chunk, :]
    glog = glog3.reshape(rows, WIDTH)
    e_n = jnp.exp(-glog)
    e_l = jnp.exp(glast - glog3).reshape(rows, WIDTH)
    lhs = jnp.concatenate([heads(kkn * jnp.exp(glog - logw)), heads(r * jnp.exp(glog))], axis=1)
    rhs = jnp.concatenate([heads(bf * e_n), heads(k2 * e_n)], axis=1)
    pair = _bmm_nt(lhs, rhs)
    st = st_sc[...]
    ls = _bmm_nt(lhs, st.astype(BF16))
    lb = jnp.where(strict, pair[:, :chunk, :chunk], 0.0)
    lk = jnp.where(strict, pair[:, :chunk, chunk:], 0.0)
    arb = jnp.where(tril, pair[:, chunk:, :chunk], 0.0)
    ark = jnp.where(tril, pair[:, chunk:, chunk:], 0.0)
    tinv = _unit_lower_inverse(lb, row, col, in_rows)
    vhb = heads(v)
    ub = _bmm(tinv.astype(BF16), (ls[:, :chunk] + _bmm(lk.astype(BF16), vhb)).astype(BF16)).astype(BF16)
    y = ls[:, chunk:] + _bmm(jnp.concatenate([ark, -arb], axis=2).astype(BF16),
                             jnp.concatenate([vhb, ub], axis=1))
    upd = _bmm_tn(jnp.concatenate([vhb, -ub], axis=1),
                  jnp.concatenate([heads(k2 * e_l), heads(bf * e_l)], axis=1))
    st_sc[...] = _split_heads(jnp.exp(glast), 0) * st + upd
    yf = _merge_heads(y, nbs)
    dev = yf - _group_sum(yf, bd) * (1.0 / HEAD_DIM)
    var = _group_sum(dev * dev, bd) * (1.0 / HEAD_DIM)
    yn = dev * lax.rsqrt(var + GN_EPS) * ln_g + ln_b
    out = (yn + bonus) * gg
    o_ref[...] = out.reshape(nbs, chunk, WIDTH)[:, 0:in_rows, :]

    @pl.when(c == pl.num_programs(1) - 1)
    def _():
        _store_state(sout_ref, prev_ref,
                     jnp.stack([jnp.stack([st_sc[b * HEADS + h].T for h in range(HEADS)])
                                for b in range(nbs)]))


def _round_up(x, m):
    return (x + m - 1) // m * m


def _pool_offsets(stride):
    offs = []
    a = 0
    for k in range(len(POOL_WINDOWS)):
        a = _round_up(a + (1 << k) * stride, SUBLANES)
        offs.append(a)
    return offs


def _pool_kernel(hist_ref, dp_ref, wbd_ref, scale_ref, out_ref, s1, s2, *, rows, stride, start):
    offs = _pool_offsets(stride)
    d0 = offs[-1]
    n = d0 + rows
    dp = dp_ref[...]
    s1[0:d0, :] = hist_ref[0]
    s1[d0:n, :] = dp
    lane = lax.broadcasted_iota(jnp.int32, (1, WIDTH), 1)
    src, dst = s1, s2
    for k, a in enumerate(offs):
        sh = (1 << k) * stride
        shifted = jnp.where(lane >= k * POOL_GROUP, src[a - sh:n - sh, :], 0.0)
        if k < len(offs) - 1:
            dst[a:n, :] = src[a:n, :] + shifted
            src, dst = dst, src
        else:
            sums = src[a:n, :] + shifted
    assert stride & (stride - 1) == 0
    pos = start + (lax.broadcasted_iota(jnp.int32, (rows, WIDTH), 0) >> (stride.bit_length() - 1))
    group = lax.broadcasted_iota(jnp.int32, (rows, WIDTH), 1) >> POOL_GROUP_SHIFT
    window = jnp.left_shift(POOL_WINDOWS[0], group)
    cnt = jnp.minimum(pos + 1, window).astype(F32)
    diff = sums / cnt - dp
    out_ref[...] = _dot(diff.astype(BF16), wbd_ref[...]) * scale_ref[...]


def _pool(hist, dp, wbd, scale, l, *, nseq, rows, stride, start):
    d0 = _pool_offsets(stride)[-1]
    return pl.pallas_call(
        functools.partial(_pool_kernel, rows=rows, stride=stride, start=start),
        grid=(nseq,),
        in_specs=[pl.BlockSpec((1, d0, WIDTH), lambda b: (b, 0, 0)),
                  pl.BlockSpec((rows, WIDTH), lambda b: (b, 0)),
                  _layer_block(wbd, l), _layer_block(scale, l)],
        out_specs=pl.BlockSpec((rows, WIDTH), lambda b: (b, 0)),
        out_shape=jax.ShapeDtypeStruct((nseq * rows, WIDTH), F32),
        scratch_shapes=[pltpu.VMEM((d0 + rows, WIDTH), F32)] * 2,
        compiler_params=pltpu.CompilerParams(dimension_semantics=("parallel",)),
        name="pool",
    )(hist, dp, wbd, scale)


def _ffn_kernel(x_ref, ma_ref, mb_ref, mc_ref, md_ref, wo_ref, g2_ref, wu_ref, wd_ref, gf_ref,
                o_ref, *, tf, final):
    mixed = None
    for i, m_ref in enumerate((ma_ref, mb_ref, mc_ref, md_ref)):
        part = _dot(m_ref[...].astype(BF16), wo_ref[i * WIDTH:(i + 1) * WIDTH, :])
        mixed = part if mixed is None else mixed + part
    x = x_ref[...] + mixed
    hm = _rms(x, g2_ref[...]).astype(BF16)
    down = None
    for j in range(D_FF // tf):
        up = jnp.maximum(_dot(hm, wu_ref[:, j * tf:(j + 1) * tf]), 0.0)
        part = _dot((up * up).astype(BF16), wd_ref[j * tf:(j + 1) * tf, :])
        down = part if down is None else down + part
    x = x + down
    if final:
        x = _rms(x, gf_ref[...])
    o_ref[...] = x


def _ffn(x, mixed, wo, g2, wu, wd, gf, l):
    t = x.shape[0]
    tm = min(ROW_TILE, t)
    row = lambda i: (i, 0)
    return pl.pallas_call(
        functools.partial(_ffn_kernel, tf=FF_TILE, final=l == DEPTH - 1),
        grid=(t // tm,),
        in_specs=[pl.BlockSpec((tm, D_MODEL), row)] + [pl.BlockSpec((tm, WIDTH), row)] * 4
        + [_layer_block(wo, l, single_buffer=True), _layer_block(g2, l),
           _layer_block(wu, l, single_buffer=True), _layer_block(wd, l, single_buffer=True), _whole(gf)],
        out_specs=pl.BlockSpec((tm, D_MODEL), row),
        out_shape=jax.ShapeDtypeStruct((t, D_MODEL), F32),
        compiler_params=pltpu.CompilerParams(dimension_semantics=("parallel",),
                                             vmem_limit_bytes=VMEM_LIMIT),
        name="ffn",
    )(x, *mixed, wo, g2, wu, wd, gf)


def _rows(v):
    return v[:, None, :]


def kernel(x_prompt, x_sample, state_b_conv, state_b_ssm, state_c_shift, state_c_wkv, state_d_pool, norm1_g, w_in, a_ws, a_bs, a_vnorm_g, b_conv_w, b_a_log, b_dt_bias, b_onorm_g, c_mu, c_w0, c_w2, c_a0, c_a2, c_g2, c_k_k, c_k_a, c_r_k, c_ln_g, c_ln_b, d_w, d_scale, w_out, norm2_g, w_up, w_down, final_g):
    nb, seq, _ = x_prompt.shape
    ns, dseq, _ = x_sample.shape
    a_chunk = a_ws.shape[-1]

    a_end = 2 * WIDTH
    b_end = a_end + 4 * WIDTH
    c_off = b_end + 2 * HEADS
    d_off = c_off + C_PROJ
    w_in_b = w_in.astype(BF16)
    w_cols = [w_in_b[:, :, :a_end], w_in_b[:, :, a_end:b_end], w_in_b[:, :, c_off:d_off], w_in_b[:, :, d_off:],
              jnp.pad(w_in_b[:, :, b_end:c_off], ((0, 0), (0, 0), (0, GATE_LANES - 2 * HEADS)))]
    wo = w_out.astype(BF16)
    wu = w_up.astype(BF16)
    wd = w_down.astype(BF16)
    g1 = _rows(norm1_g)
    g2 = _rows(norm2_g)
    gf = final_g[None, :]
    vg = _rows(a_vnorm_g)
    causal = jnp.tril(jnp.ones((a_chunk, a_chunk), dtype=bool))
    wm = jnp.where(causal, a_ws, 0.0)
    bias_t = jnp.repeat(jnp.transpose(a_bs, (0, 2, 1)), HEAD_DIM, axis=2)
    mm_p = jnp.transpose(wm, (0, 2, 1, 3)).reshape(DEPTH, a_chunk, HEADS * a_chunk).astype(BF16)
    srow = jnp.arange(ns * dseq)
    step_onehot = (srow[:, None] % dseq == jnp.arange(dseq)[None, :]).astype(F32)
    same_seq = (srow[:, None] // dseq) == (srow[None, :] // dseq)
    mm_s = jnp.einsum('it,lhts,js->lihj', step_onehot, wm[:, :, :dseq, :dseq], step_onehot,
                      precision=lax.Precision.HIGHEST)
    mm_s = jnp.where(same_seq[None, :, None, :], mm_s, 0.0).reshape(
        DEPTH, ns * dseq, HEADS * ns * dseq).astype(BF16)
    bias_s = jnp.tile(bias_t[:, :dseq], (1, ns, 1))
    lead = jnp.zeros((DEPTH, HEADS), F32)
    tail = jnp.zeros((DEPTH, GATE_LANES - 2 * HEADS), F32)
    gvec = jnp.concatenate([lead, b_a_log, tail, lead, b_dt_bias, tail, jnp.tile(b_onorm_g, (1, HEADS))],
                           axis=1)[:, None, :]
    cvec = jnp.concatenate([c_mu, c_w0, c_a0, c_k_k, c_k_a, c_r_k.reshape(DEPTH, WIDTH), c_ln_g, c_ln_b],
                           axis=1)[:, None, :]
    c_w2b, c_a2b, c_g2b = c_w2.astype(BF16), c_a2.astype(BF16), c_g2.astype(BF16)
    n_groups = len(POOL_WINDOWS)
    wbd = jnp.einsum('lgcd,gh->lgchd', d_w, jnp.eye(n_groups, dtype=F32)).reshape(
        DEPTH, WIDTH, WIDTH).astype(BF16)
    dscale = _rows(d_scale)
    lane_head = jnp.arange(WIDTH) // HEAD_DIM
    gate_lane = jnp.arange(GATE_LANES)
    bd = (lane_head[:, None] == lane_head[None, :]).astype(BF16)
    eb = jnp.tile((gate_lane[:, None] == lane_head[None, :]).astype(BF16), (2, 1))
    eg = jnp.tile((gate_lane[:, None] == lane_head[None, :] + HEADS).astype(BF16), (3, 1))

    def run_group(x, l, grp, prev_b, prev_c):
        nseq, length = grp["nseq"], grp["length"]
        conv_state = grp["b_conv"][grp["state_layer"](l)]
        if grp["prep"]:
            a_out, a_v, pb, pc, pd, pg, tails, tails_c = _inproj(
                x, g1, w_cols, grp["mm"], grp["bias"], vg, l,
                token_prep=(b_conv_w, gvec, bd, conv_state, length, cvec, c_w2b, c_a2b, c_g2b,
                          grp["c_shift"][grp["state_layer"](l)]))
            b_rows = tails.reshape(nseq, -1, SUBLANES, B_QKV)[:, -1]
            c_last = tails_c.reshape(nseq, -1, SUBLANES, C_PROJ)[:, -1, SUBLANES - 1]
        else:
            a_out, a_v, pb, pc, pd, pg = _inproj(x, g1, w_cols, grp["mm"], grp["bias"], vg, l)
            b_rows = pb.reshape(nseq, length, 4 * WIDTH)[:, :, :B_QKV]
            c_last = pc.reshape(nseq, length, C_PROJ)[:, length - 1]
        b_tail = jnp.concatenate([conv_state, b_rows], axis=1)[:, -(B_CONV - 1):]
        pb3 = pb.reshape(nseq, length, 4 * WIDTH)
        pc3 = pc.reshape(nseq, length, pc.shape[1])
        b_out, b_state = _recurrent_call(
            _gdn_kernel, "gdn", [pb3, pg.reshape(nseq, length, GATE_LANES)], grp["b_ssm"], grp["state_layer"](l),
            grp["b_conv"], [(b_conv_w, l), (gvec, l), (bd, None), (eb, None), (eg, None)], prev_b,
            nbs=grp["nbs"], chunk=grp["chunk"], carry_width=B_QKV, prepped=grp["prep"])
        c_out, c_state = _recurrent_call(
            _rwkv_kernel, "rwkv", [pc3], grp["c_wkv"], grp["state_layer"](l), grp["c_shift"],
            [(cvec, l), (c_w2b, l), (c_a2b, l), (c_g2b, l), (bd, None)], prev_c,
            nbs=grp["nbs"], chunk=grp["chunk"], carry_width=C_PROJ, prepped=grp["prep"])
        d_out = grp["pool"](pd, l)
        x = _ffn(x, (a_out, b_out.reshape(nseq * length, WIDTH), c_out.reshape(nseq * length, WIDTH), d_out),
                 wo, g2, wu, wd, gf, l)
        return x, a_v, b_tail, c_last, pd.reshape(nseq, length, WIDTH), b_state, c_state

    d0 = _pool_offsets(1)[-1]
    zero_hist = jnp.zeros((nb, d0, WIDTH), F32)
    prompt = dict(
        nseq=nb, length=seq, nbs=PROMPT_SEQS_PER_STEP, chunk=MIX_CHUNK, mm=mm_p, bias=bias_t,
        b_ssm=jnp.zeros((1, nb, HEADS, HEAD_DIM, HEAD_DIM), F32), c_wkv=jnp.zeros((1, nb, HEADS, HEAD_DIM, HEAD_DIM), F32),
        b_conv=jnp.zeros((1, nb, B_CONV - 1, B_QKV), F32), c_shift=jnp.zeros((1, nb, 1, C_PROJ), F32),
        state_layer=lambda l: 0, prep=True,
        pool=lambda pd, l: _pool(zero_hist, pd, wbd, dscale, l, nseq=nb, rows=seq, stride=1, start=0))

    def sample_pool(pd, l):
        pd_t = jnp.transpose(pd.reshape(ns, dseq, WIDTH), (1, 0, 2)).reshape(dseq * ns, WIDTH)
        hist = jnp.transpose(state_d_pool[l], (1, 0, 2)).reshape(1, POOL_BUF * ns, WIDTH)
        out_t = _pool(hist, pd_t, wbd, dscale, l, nseq=1, rows=dseq * ns, stride=ns, start=PAST_LEN)
        return jnp.transpose(out_t.reshape(dseq, ns, WIDTH), (1, 0, 2)).reshape(ns * dseq, WIDTH)

    sample = dict(
        nseq=ns, length=dseq, nbs=SAMPLE_SEQS_PER_STEP, chunk=SUBLANES, mm=mm_s, bias=bias_s,
        b_ssm=state_b_ssm, c_wkv=state_c_wkv, b_conv=state_b_conv, c_shift=state_c_shift[:, :, None, :],
        state_layer=lambda l: l, prep=False, pool=sample_pool)

    xp = x_prompt.reshape(nb * seq, D_MODEL)
    xs = x_sample.reshape(ns * dseq, D_MODEL)
    p_bc, p_cs, p_dp, s_av, s_bc, s_cs, s_dp = [], [], [], [], [], [], []
    p_bs = p_cw = s_bs = s_cw = None
    for l in range(DEPTH):
        xp, _, b_tail, c_last, pd3, p_bs, p_cw = run_group(xp, l, prompt, p_bs, p_cw)
        p_bs = p_bs if l else p_bs[None]
        p_cw = p_cw if l else p_cw[None]
        p_bc.append(b_tail)
        p_cs.append(c_last)
        p_dp.append(pd3[:, seq - POOL_BUF:])

        xs, a_v, b_tail, c_last, pd3, s_bs, s_cw = run_group(xs, l, sample, s_bs, s_cw)
        s_bs = s_bs if l else s_bs[None]
        s_cw = s_cw if l else s_cw[None]
        s_av.append(a_v.reshape(ns, dseq, WIDTH))
        s_bc.append(b_tail)
        s_cs.append(c_last)
        s_dp.append(jnp.concatenate([state_d_pool[l], pd3], axis=1)[:, -POOL_BUF:])

    return (xp.reshape(nb, seq, D_MODEL), xs.reshape(ns, dseq, D_MODEL),
            jnp.stack(p_bc), p_bs, jnp.stack(p_cs), p_cw, jnp.stack(p_dp),
            jnp.stack(s_av), jnp.stack(s_bc), s_bs, jnp.stack(s_cs), s_cw, jnp.stack(s_dp))
```

```python
import functools

import jax
import jax.numpy as jnp
from jax import lax
from jax.experimental import pallas as pl
from jax.experimental.pallas import tpu as pltpu

F32 = jnp.float32
BF16 = jnp.bfloat16

D_MODEL = 1024
DEPTH = 2
HEADS = 4
HEAD_DIM = 64
HEAD_SHIFT = HEAD_DIM.bit_length() - 1
WIDTH = HEADS * HEAD_DIM
B_QKV = 3 * WIDTH
B_CONV = 4
C_PROJ = 4 * WIDTH
C_DECAY_LORA = 64
C_RATE_LORA = 64
C_GATE_LORA = 128
POOL_WINDOWS = (2, 4, 8, 16)
POOL_BUF = max(POOL_WINDOWS) - 1
POOL_GROUP = WIDTH // len(POOL_WINDOWS)
POOL_GROUP_SHIFT = POOL_GROUP.bit_length() - 1
D_FF = 4 * D_MODEL
PAST_LEN = 16384
NORM_EPS = 1e-6
L2_EPS = 1e-6
GN_EPS = 64e-5
V7X_LANES = 128
V7X_SUBLANES = 8
V7X_VMEM_BYTES = 64 * 1024 * 1024
GATE_LANES = V7X_LANES
SUBLANES = V7X_SUBLANES
VMEM_LIMIT = V7X_VMEM_BYTES * 3 // 4
MIX_CHUNK = 64
PROMPT_SEQS_PER_STEP = 8
SAMPLE_SEQS_PER_STEP = 32
ROW_TILE = 512
FF_TILE = 1024


def _rms(x, g):
    return x * lax.rsqrt(jnp.mean(x * x, axis=-1, keepdims=True) + NORM_EPS) * g


def _softplus(x):
    return jnp.maximum(x, 0.0) + jnp.log1p(jnp.exp(-jnp.abs(x)))


def _dot(a, b):
    return jnp.dot(a, b, preferred_element_type=F32)


def _bmm(a, b):
    return jnp.einsum('bij,bjk->bik', a, b, preferred_element_type=F32)


def _bmm_nt(a, b):
    return jnp.einsum('bik,bjk->bij', a, b, preferred_element_type=F32)


def _bmm_tn(a, b):
    return jnp.einsum('bki,bkj->bij', a, b, preferred_element_type=F32)


def _layer_block(arr, l, single_buffer=False):
    shape = arr.shape[1:]
    index = lambda *_: (l,) + (0,) * len(shape)
    if single_buffer:
        return pl.BlockSpec((None,) + shape, index, pipeline_mode=pl.Buffered(1))
    return pl.BlockSpec((None,) + shape, index)


def _whole(arr):
    return pl.BlockSpec(arr.shape, lambda *_: (0,) * arr.ndim)


def _split_heads(x, lane0):
    return jnp.stack([x[b, :, lane0 + h * HEAD_DIM:lane0 + (h + 1) * HEAD_DIM]
                      for b in range(x.shape[0]) for h in range(HEADS)])


def _merge_heads(x, nbs):
    return jnp.concatenate([jnp.concatenate([x[b * HEADS + h] for h in range(HEADS)], axis=1)
                            for b in range(nbs)], axis=0)


def _group_sum(x, bd):
    hi = x.astype(BF16)
    lo = (x - hi.astype(F32)).astype(BF16)
    return _dot(hi, bd) + _dot(lo, bd)


def _split3(x):
    p1 = x.astype(BF16)
    r1 = x - p1.astype(F32)
    p2 = r1.astype(BF16)
    return p1, p2, (r1 - p2.astype(F32)).astype(BF16)


def _cumsum_rows(tril_b, x):
    w = x.shape[1]
    y = _dot(tril_b, jnp.concatenate(_split3(x), axis=1))
    return y[:, :w] + (y[:, w:2 * w] + y[:, 2 * w:])


def _unit_lower_inverse(lm, row, col, n_valid):
    def sub_diag_block(shift):
        return (((row >> (shift + 1)) == (col >> (shift + 1)))
                & (((row >> shift) & 1) == 1) & (((col >> shift) & 1) == 0))

    m = (row == col).astype(F32) - jnp.where(sub_diag_block(0), lm, 0.0)
    shift = 1
    while (1 << shift) < n_valid:
        cs = jnp.where(sub_diag_block(shift), lm, 0.0).astype(BF16)
        mb = m.astype(BF16)
        m = m - _bmm(_bmm(mb, cs).astype(BF16), mb)
        shift += 1
    return m


def _lora_inputs(x, lane0):
    a = lane0 + C_DECAY_LORA
    b = a + C_RATE_LORA
    return x[:, lane0:a], x[:, a:b], x[:, b:b + C_GATE_LORA]


def _lora_rows(x):
    a = C_DECAY_LORA
    b = a + C_RATE_LORA
    return x[0:a], x[a:b], x[b:b + C_GATE_LORA]


def _transpose_rows(a):
    n = a.shape[0]
    if n < GATE_LANES:
        a = jnp.concatenate([a, jnp.zeros((GATE_LANES - n, a.shape[1]), a.dtype)], axis=0)
    return a.T[:, :n]


def _stage_rows(ref, stage, in_rows):
    if stage is None:
        return ref[...]
    stage[...] = jnp.zeros(stage.shape, stage.dtype)
    stage[:, 0:in_rows, :] = ref[...]
    return stage[...]


def _conv_silu(raw, hist, tail_ref, lo, cw):
    tm = raw.shape[0]
    cols = slice(lo, lo + WIDTH)
    xfull = jnp.concatenate([hist[:, cols], raw], axis=0)
    conv = pltpu.roll(xfull, 3, axis=0)[SUBLANES:, :] * cw[0:1, cols]
    conv = conv + pltpu.roll(xfull, 2, axis=0)[SUBLANES:, :] * cw[1:2, cols]
    conv = conv + pltpu.roll(xfull, 1, axis=0)[SUBLANES:, :] * cw[2:3, cols]
    conv = conv + raw * cw[3:4, cols]
    tail = raw[tm - SUBLANES:tm, :]
    hist[:, cols] = tail
    tail_ref[:, cols] = tail
    return jax.nn.silu(conv)


def _inproj_kernel(*refs, r, prep, tiles_per_seq):
    (x_ref, g_ref, wa_ref, wb_ref, wc_ref, wd_ref, wg_ref, mm_ref, bias_ref, vg_ref) = refs[:10]
    if prep:
        cw_ref, gvec_ref, bd_ref, cbuf_ref, cvec_ref, w2_ref, a2_ref, g2_ref, sh_ref = refs[10:19]
        a_ref, v_ref, pb_ref, pc_ref, pd_ref, pg_ref, tail_ref, tailc_ref, hist, hist_c = refs[19:]
    else:
        a_ref, v_ref, pb_ref, pc_ref, pd_ref, pg_ref = refs[10:]
    h = _rms(x_ref[...], g_ref[...]).astype(BF16)

    def gmlp_gates(pa):
        v = _rms(jax.nn.gelu(pa[:, WIDTH:]), vg_ref[...])
        v_ref[...] = v
        return jax.nn.gelu(pa[:, :WIDTH]), v

    def gmlp_mix(u, v):
        lane_head = lax.broadcasted_iota(jnp.int32, (r, WIDTH), 1) >> HEAD_SHIFT
        mm = mm_ref[...]
        bias = bias_ref[...]
        for i in range(u.shape[0] // r):
            vc = v[i * r:(i + 1) * r]
            per_head = jnp.concatenate([jnp.where(lane_head == h, vc, 0.0).astype(BF16)
                                        for h in range(HEADS)], axis=0)
            a_ref[i * r:(i + 1) * r, :] = u[i * r:(i + 1) * r] * (_dot(mm, per_head) + bias)

    if prep:
        n_hist = B_CONV - 1

        @pl.when(pl.program_id(0) % tiles_per_seq == 0)
        def _():
            hist[...] = jnp.zeros(hist.shape, F32)
            hist[SUBLANES - n_hist:SUBLANES, :] = cbuf_ref[0]
            hist_c[...] = jnp.zeros(hist_c.shape, F32)
            hist_c[SUBLANES - 1:SUBLANES, :] = sh_ref[0]

        cvec = cvec_ref[...]
        mu = cvec[:, 0:C_PROJ]
        w0, a0, k_k, k_a, r_k = (cvec[:, C_PROJ + i * WIDTH:C_PROJ + (i + 1) * WIDTH] for i in range(5))

        def shifted_lerp(cp, lo):
            cols = slice(lo, lo + WIDTH)
            prev = pltpu.roll(jnp.concatenate([hist_c[:, cols], cp], axis=0), 1, axis=0)[SUBLANES:, :]
            tail = cp[cp.shape[0] - SUBLANES:, :]
            hist_c[:, cols] = tail
            tailc_ref[:, cols] = tail
            return cp + (prev - cp) * mu[:, cols]

        cw = cw_ref[...]
        bd = bd_ref[...]
        gvec = gvec_ref[...]
        raw_q = _dot(h, wb_ref[:, 0:WIDTH])
        raw_k = _dot(h, wb_ref[:, WIDTH:2 * WIDTH])
        raw_v = _dot(h, wb_ref[:, 2 * WIDTH:3 * WIDTH])
        qf = _conv_silu(raw_q, hist, tail_ref, 0, cw)
        pb_ref[:, 0:WIDTH] = qf * lax.rsqrt(_dot((qf * qf).astype(BF16), bd) + L2_EPS) * (HEAD_DIM ** -0.5)
        raw_gate = _dot(h, wb_ref[:, 3 * WIDTH:4 * WIDTH])
        pg = _dot(h, wg_ref[...])
        kf = _conv_silu(raw_k, hist, tail_ref, WIDTH, cw)
        pb_ref[:, WIDTH:2 * WIDTH] = kf * lax.rsqrt(_dot((kf * kf).astype(BF16), bd) + L2_EPS)
        cp_l = _dot(h, wc_ref[:, 3 * WIDTH:4 * WIDTH])
        cp_k = _dot(h, wc_ref[:, WIDTH:2 * WIDTH])
        pb_ref[:, 2 * WIDTH:3 * WIDTH] = _conv_silu(raw_v, hist, tail_ref, 2 * WIDTH, cw)
        pa = _dot(h, wa_ref[...])
        pb_ref[:, 3 * WIDTH:4 * WIDTH] = jax.nn.silu(raw_gate)
        alog = gvec[:, 0:GATE_LANES]
        dtb = gvec[:, GATE_LANES:2 * GATE_LANES]
        lane = lax.broadcasted_iota(jnp.int32, pg.shape, 1)
        pg_ref[...] = jnp.where(lane < HEADS, jax.nn.sigmoid(pg), -jnp.exp(alog) * _softplus(pg + dtb))
        cp_r = _dot(h, wc_ref[:, 0:WIDTH])
        cp_v = _dot(h, wc_ref[:, 2 * WIDTH:3 * WIDTH])
        xm_l = shifted_lerp(cp_l, 3 * WIDTH)
        wl, al, gl = _lora_inputs(xm_l, 0)
        w_log = -_softplus(-(w0 + _dot(jnp.tanh(wl).astype(BF16), w2_ref[...]))) - 0.5
        pc_ref[:, 5 * WIDTH:6 * WIDTH] = -jnp.exp(w_log)
        a = jax.nn.sigmoid(a0 + _dot(al.astype(BF16), a2_ref[...]))
        pc_ref[:, 6 * WIDTH:7 * WIDTH] = _dot(jax.nn.sigmoid(gl).astype(BF16), g2_ref[...])
        pd_ref[...] = _dot(h, wd_ref[...])
        k = shifted_lerp(cp_k, WIDTH)
        kkraw = k * k_k
        kkn = kkraw * lax.rsqrt(_dot((kkraw * kkraw).astype(BF16), bd) + L2_EPS)
        k2 = k * (1.0 + (a - 1.0) * k_a)
        pc_ref[:, WIDTH:2 * WIDTH] = k2
        pc_ref[:, 3 * WIDTH:4 * WIDTH] = kkn
        pc_ref[:, 4 * WIDTH:5 * WIDTH] = kkn * a
        rr = shifted_lerp(cp_r, 0)
        vv = shifted_lerp(cp_v, 2 * WIDTH)
        pc_ref[:, 0:WIDTH] = rr
        pc_ref[:, 2 * WIDTH:3 * WIDTH] = vv
        pc_ref[:, 7 * WIDTH:8 * WIDTH] = _group_sum(rr * k2 * r_k, bd) * vv
        gmlp_mix(*gmlp_gates(pa))
    else:
        pa = _dot(h, wa_ref[...])
        pb_ref[...] = _dot(h, wb_ref[...])
        u, v = gmlp_gates(pa)
        pc_ref[...] = _dot(h, wc_ref[...])
        pd_ref[...] = _dot(h, wd_ref[...])
        pg_ref[...] = _dot(h, wg_ref[...])
        gmlp_mix(u, v)


def _inproj(x, g, ws, mm, bias, vg, l, token_prep=None):
    t = x.shape[0]
    r = mm.shape[1]
    tm = max(r, min(ROW_TILE, t))
    widths = [WIDTH, WIDTH] + [w.shape[2] for w in ws[1:]]
    in_specs = ([pl.BlockSpec((tm, D_MODEL), lambda i: (i, 0)), _layer_block(g, l)]
                + [_layer_block(w, l, single_buffer=True) for w in ws]
                + [_layer_block(mm, l), _layer_block(bias, l), _layer_block(vg, l)])
    operands = [x, g, *ws, mm, bias, vg]
    out_specs = [pl.BlockSpec((tm, n), lambda i: (i, 0)) for n in widths]
    out_shape = [jax.ShapeDtypeStruct((t, n), F32) for n in widths]
    scratch = []
    tiles_per_seq = 1
    if token_prep is not None:
        conv_w, gvec, bd, conv_state, seq_len, cvec, w2, a2, g2, shift_state = token_prep
        tiles_per_seq = seq_len // tm
        per_seq = lambda i: (i // tiles_per_seq, 0, 0)
        in_specs += [_layer_block(conv_w, l), _layer_block(gvec, l), _whole(bd),
                     pl.BlockSpec((1,) + conv_state.shape[1:], per_seq),
                     _layer_block(cvec, l), _layer_block(w2, l), _layer_block(a2, l), _layer_block(g2, l),
                     pl.BlockSpec((1,) + shift_state.shape[1:], per_seq)]
        operands += [conv_w, gvec, bd, conv_state, cvec, w2, a2, g2, shift_state]
        widths[3] = 8 * WIDTH
        out_specs[3] = pl.BlockSpec((tm, 8 * WIDTH), lambda i: (i, 0))
        out_shape[3] = jax.ShapeDtypeStruct((t, 8 * WIDTH), F32)
        for w in (B_QKV, C_PROJ):
            out_specs.append(pl.BlockSpec((SUBLANES, w), lambda i: (i, 0)))
            out_shape.append(jax.ShapeDtypeStruct((t // tm * SUBLANES, w), F32))
            scratch.append(pltpu.VMEM((SUBLANES, w), F32))
    return pl.pallas_call(
        functools.partial(_inproj_kernel, r=r, prep=token_prep is not None, tiles_per_seq=tiles_per_seq),
        grid=(t // tm,),
        in_specs=in_specs,
        out_specs=out_specs,
        out_shape=out_shape,
        scratch_shapes=scratch,
        compiler_params=pltpu.CompilerParams(
            dimension_semantics=("arbitrary" if token_prep is not None else "parallel",),
            vmem_limit_bytes=VMEM_LIMIT),
        name="inproj",
    )(*operands)


def _store_state(sout_ref, prev_ref, new_state):
    if prev_ref is None:
        sout_ref[...] = new_state
    else:
        n_prev = prev_ref.shape[0]
        sout_ref[0:n_prev] = prev_ref[...]
        sout_ref[n_prev] = new_state


def _recurrent_call(kernel_fn, name, x_blocks, s0, s0_layer, carry_in, consts, prev, *,
                    nbs, chunk, carry_width, out_width=WIDTH, **kernel_flags):
    nseq, length, _ = x_blocks[0].shape
    in_rows = min(chunk, length)
    n_chunks = length // in_rows
    blk = lambda b, c: (b, c, 0)
    state_block = (nbs, HEADS, HEAD_DIM, HEAD_DIM)
    in_specs = [pl.BlockSpec((nbs, in_rows, x.shape[2]), blk) for x in x_blocks]
    in_specs += [pl.BlockSpec((None,) + state_block, lambda b, c: (s0_layer, b, 0, 0, 0)),
                 pl.BlockSpec((None, nbs) + carry_in.shape[2:], lambda b, c: (s0_layer, b, 0, 0))]
    in_specs += [_whole(a) if l is None else _layer_block(a, l) for a, l in consts]
    operands = list(x_blocks) + [s0, carry_in] + [a for a, _ in consts]
    if prev is None:
        state_shape = (nseq, HEADS, HEAD_DIM, HEAD_DIM)
        state_spec = pl.BlockSpec(state_block, lambda b, c: (b, 0, 0, 0))
    else:
        n_prev = prev.shape[0]
        in_specs.append(pl.BlockSpec((n_prev,) + state_block, lambda b, c: (0, b, 0, 0, 0)))
        operands.append(prev)
        state_shape = (n_prev + 1, nseq, HEADS, HEAD_DIM, HEAD_DIM)
        state_spec = pl.BlockSpec((n_prev + 1,) + state_block, lambda b, c: (0, b, 0, 0, 0))
    scratch = [pltpu.VMEM((nbs * HEADS, HEAD_DIM, HEAD_DIM), F32),
               pltpu.VMEM((nbs, chunk + SUBLANES, carry_width), F32)]
    if in_rows < chunk:
        scratch += [pltpu.VMEM((nbs, chunk, x.shape[2]), F32) for x in x_blocks]
    return pl.pallas_call(
        functools.partial(kernel_fn, nbs=nbs, chunk=chunk, in_rows=in_rows, n_x=len(x_blocks),
                          n_consts=len(consts), has_prev=prev is not None, carry=n_chunks > 1,
                          **kernel_flags),
        grid=(nseq // nbs, n_chunks),
        in_specs=in_specs,
        out_specs=[pl.BlockSpec((nbs, in_rows, out_width), blk), state_spec],
        out_shape=[jax.ShapeDtypeStruct((nseq, length, out_width), F32),
                   jax.ShapeDtypeStruct(state_shape, F32)],
        scratch_shapes=scratch,
        compiler_params=pltpu.CompilerParams(dimension_semantics=("parallel", "arbitrary")),
        name=name,
    )(*operands)


def _unpack_refs(refs, n_x, n_consts, has_prev, staged):
    x_refs = refs[:n_x]
    s0_ref, carry_ref = refs[n_x:n_x + 2]
    consts = refs[n_x + 2:n_x + 2 + n_consts]
    pos = n_x + 2 + n_consts
    prev_ref = refs[pos] if has_prev else None
    pos += int(has_prev)
    o_ref, sout_ref, state_sc, rows_sc = refs[pos:pos + 4]
    stages = refs[pos + 4:] if staged else (None,) * n_x
    return x_refs, s0_ref, carry_ref, consts, prev_ref, o_ref, sout_ref, state_sc, rows_sc, stages


def _gdn_kernel(*refs, nbs, chunk, in_rows, n_x, n_consts, has_prev, carry, prepped):
    ((pb_ref, pg_ref), s0_ref, cbuf_ref, (cw_ref, vec_ref, bd_ref, eb_ref, eg_ref), prev_ref,
     o_ref, sout_ref, s_sc, xbuf, (stage_b, stage_g)) = _unpack_refs(refs, n_x, n_consts, has_prev,
                                                                    in_rows < chunk)
    c = pl.program_id(1)
    nb = nbs * HEADS
    rows = nbs * chunk
    n_hist = B_CONV - 1

    @pl.when(c == 0)
    def _():
        s_sc[...] = s0_ref[...].reshape(nb, HEAD_DIM, HEAD_DIM)
        if not prepped:
            xbuf[:, 0:SUBLANES, :] = jnp.zeros((nbs, SUBLANES, B_QKV), F32)
            xbuf[:, SUBLANES - n_hist:SUBLANES, :] = cbuf_ref[...]

    pb = _stage_rows(pb_ref, stage_b, in_rows)
    pg = _stage_rows(pg_ref, stage_g, in_rows)
    vec = vec_ref[...]
    og = vec[:, 2 * GATE_LANES:2 * GATE_LANES + WIDTH]
    bd = bd_ref[...]
    if prepped:
        qkv = pb[:, :, 0:B_QKV].reshape(rows, B_QKV)
        gate = pb[:, :, B_QKV:B_QKV + WIDTH].reshape(rows, WIDTH)
        beta_all = g_all = pg
        qn = qkv[:, 0:WIDTH]
        kn = qkv[:, WIDTH:2 * WIDTH]
    else:
        raw = pb[:, :, 0:B_QKV]
        xbuf[:, SUBLANES:SUBLANES + chunk, :] = raw
        cw = cw_ref[...]
        xfull = xbuf[...]
        conv = pltpu.roll(xfull, 3, axis=1)[:, SUBLANES:, :] * cw[0:1]
        conv = conv + pltpu.roll(xfull, 2, axis=1)[:, SUBLANES:, :] * cw[1:2]
        conv = conv + pltpu.roll(xfull, 1, axis=1)[:, SUBLANES:, :] * cw[2:3]
        conv = conv + raw * cw[3:4]
        if carry:
            xbuf[:, 0:SUBLANES, :] = xbuf[:, chunk:chunk + SUBLANES, :]
        qkv = jax.nn.silu(conv).reshape(rows, B_QKV)
        gate = jax.nn.silu(pb[:, :, B_QKV:B_QKV + WIDTH]).reshape(rows, WIDTH)
        alog = vec[:, 0:GATE_LANES]
        dtb = vec[:, GATE_LANES:2 * GATE_LANES]
        beta_all = jax.nn.sigmoid(pg)
        g_all = -jnp.exp(alog) * _softplus(pg + dtb)
        if in_rows < chunk:
            valid = lax.broadcasted_iota(jnp.int32, pg.shape, 1) < in_rows
            beta_all = jnp.where(valid, beta_all, 0.0)
            g_all = jnp.where(valid, g_all, 0.0)
        qf = qkv[:, 0:WIDTH]
        kf = qkv[:, WIDTH:2 * WIDTH]
        qn = qf * lax.rsqrt(_dot((qf * qf).astype(BF16), bd) + L2_EPS) * (HEAD_DIM ** -0.5)
        kn = kf * lax.rsqrt(_dot((kf * kf).astype(BF16), bd) + L2_EPS)
    vf = qkv[:, 2 * WIDTH:]

    row = lax.broadcasted_iota(jnp.int32, (chunk, chunk), 0)
    col = lax.broadcasted_iota(jnp.int32, (chunk, chunk), 1)
    tril = row >= col
    strict = row > col
    trilb = tril.astype(BF16)
    b1, b2, _ = _split3(beta_all.reshape(rows, GATE_LANES))
    beta_f = _dot(jnp.concatenate([b1, b2], axis=1), eb_ref[...])
    gc_small = [_cumsum_rows(trilb, g_all[b]) for b in range(nbs)]
    gc = _dot(jnp.concatenate(_split3(jnp.concatenate(gc_small, axis=0)), axis=1),
              eg_ref[...]).reshape(nbs, chunk, WIDTH)
    glast = gc[:, chunk - 1:chunk, :]
    e_g = jnp.exp(gc).reshape(rows, WIDTH)
    e_gl = jnp.exp(glast - gc).reshape(rows, WIDTH)
    bk = beta_f * kn

    def heads(x):
        return _split_heads(x.reshape(nbs, chunk, WIDTH).astype(BF16), 0)

    qkk = _bmm_nt(jnp.concatenate([heads(qn), heads(bk)], axis=1), heads(kn))
    grow_all = [_transpose_rows(g) for g in gc_small]
    gcol = jnp.stack([gc[b, :, h * HEAD_DIM:h * HEAD_DIM + 1] for b in range(nbs) for h in range(HEADS)])
    grow = jnp.stack([grow_all[b][HEADS + h:HEADS + h + 1, :] for b in range(nbs) for h in range(HEADS)])
    decay = jnp.where(tril, jnp.exp(jnp.minimum(gcol - grow, 0.0)), 0.0)
    qk = qkk[:, :chunk] * decay
    lm = jnp.where(strict, qkk[:, chunk:] * decay, 0.0)
    tinv = _unit_lower_inverse(lm, row, col, in_rows)
    rhs = jnp.concatenate([heads(beta_f * vf), heads(bk * e_g)], axis=2)
    uw = _bmm(tinv.astype(BF16), rhs)
    u = uw[:, :, :HEAD_DIM]
    wk = uw[:, :, HEAD_DIM:]
    s = s_sc[...]
    ws = _bmm(jnp.concatenate([wk.astype(BF16), heads(qn * e_g)], axis=1),
              s.astype(BF16))
    wnb = (u - ws[:, :chunk]).astype(BF16)
    o = ws[:, chunk:] + _bmm(qk.astype(BF16), wnb)
    s_sc[...] = _split_heads(jnp.exp(glast), 0) * s + _bmm_tn(heads(kn * e_gl), wnb)
    of = _merge_heads(o, nbs)
    of = of * lax.rsqrt(_group_sum(of * of, bd) * (1.0 / HEAD_DIM) + NORM_EPS) * og * gate
    o_ref[...] = of.reshape(nbs, chunk, WIDTH)[:, 0:in_rows, :]

    @pl.when(c == pl.num_programs(1) - 1)
    def _():
        _store_state(sout_ref, prev_ref, s_sc[...].reshape(nbs, HEADS, HEAD_DIM, HEAD_DIM))


def _rwkv_kernel(*refs, nbs, chunk, in_rows, n_x, n_consts, has_prev, carry, prepped):
    ((pc_ref,), s0_ref, sh_ref, (vec_ref, w2_ref, a2_ref, g2_ref, bd_ref), prev_ref,
     o_ref, sout_ref, st_sc, xs, (stage_c,)) = _unpack_refs(refs, n_x, n_consts, has_prev, in_rows < chunk)
    c = pl.program_id(1)
    rows = nbs * chunk

    @pl.when(c == 0)
    def _():
        for b in range(nbs):
            for h in range(HEADS):
                st_sc[b * HEADS + h] = s0_ref[b, h].T
        if not prepped:
            xs[:, SUBLANES - 1:SUBLANES, :] = sh_ref[...]

    vec = vec_ref[...]
    mu = vec[:, 0:C_PROJ]
    w0, a0, k_k, k_a, r_k, ln_g, ln_b = (vec[:, C_PROJ + i * WIDTH:C_PROJ + (i + 1) * WIDTH] for i in range(7))
    bd = bd_ref[...]
    if prepped:
        xp = pc_ref[...].reshape(rows, 8 * WIDTH)
        r, k2, v, kkn, bf, logw, gg, bonus = (xp[:, i * WIDTH:(i + 1) * WIDTH] for i in range(8))
    else:
        cp = _stage_rows(pc_ref, stage_c, in_rows)
        xs[:, SUBLANES:SUBLANES + chunk, :] = cp
        prev = xs[:, SUBLANES - 1:SUBLANES - 1 + chunk, :]
        if carry:
            xs[:, 0:SUBLANES, :] = xs[:, chunk:chunk + SUBLANES, :]
        xm = (cp + (prev - cp) * mu).reshape(rows, C_PROJ)
        r = xm[:, 0:WIDTH]
        k = xm[:, WIDTH:2 * WIDTH]
        v = xm[:, 2 * WIDTH:3 * WIDTH]
        wl, al, gl = _lora_inputs(xm, 3 * WIDTH)
        w_log = -_softplus(-(w0 + _dot(jnp.tanh(wl).astype(BF16), w2_ref[...]))) - 0.5
        logw = -jnp.exp(w_log)
        a = jax.nn.sigmoid(a0 + _dot(al.astype(BF16), a2_ref[...]))
        gg = _dot(jax.nn.sigmoid(gl).astype(BF16), g2_ref[...])
        kkraw = k * k_k
        k2 = k * (1.0 + (a - 1.0) * k_a)
        bonus = _group_sum(r * k2 * r_k, bd) * v
        if in_rows < chunk:
            t_in_chunk = lax.broadcasted_iota(jnp.int32, (nbs, chunk, WIDTH), 1).reshape(rows, WIDTH)
            valid = t_in_chunk < in_rows
            logw = jnp.where(valid, logw, 0.0)
            kkraw = jnp.where(valid, kkraw, 0.0)
            k2 = jnp.where(valid, k2, 0.0)
        kkn = kkraw * lax.rsqrt(_dot((kkraw * kkraw).astype(BF16), bd) + L2_EPS)
        bf = kkn * a

    row = lax.broadcasted_iota(jnp.int32, (chunk, chunk), 0)
    col = lax.broadcasted_iota(jnp.int32, (chunk, chunk), 1)
    tril = row >= col
    strict = row > col
    trilb = tril.astype(BF16)

    def heads(x):
        return _split_heads(x.reshape(nbs, chunk, WIDTH).astype(BF16), 0)

    logw3 = logw.reshape(nbs, chunk, WIDTH)
    glog3 = jnp.stack([_cumsum_rows(trilb, logw3[b]) for b in range(nbs)])
    glast = glog3[:, chunk - 1:chunk, :]
    glog = glog3.reshape(rows, WIDTH)
    e_n = jnp.exp(-glog)
    e_l = jnp.exp(glast - glog3).reshape(rows, WIDTH)
    lhs = jnp.concatenate([heads(kkn * jnp.exp(glog - logw)), heads(r * jnp.exp(glog))], axis=1)
    rhs = jnp.concatenate([heads(bf * e_n), heads(k2 * e_n)], axis=1)
    pair = _bmm_nt(lhs, rhs)
    st = st_sc[...]
    ls = _bmm_nt(lhs, st.astype(BF16))
    lb = jnp.where(strict, pair[:, :chunk, :chunk], 0.0)
    lk = jnp.where(strict, pair[:, :chunk, chunk:], 0.0)
    arb = jnp.where(tril, pair[:, chunk:, :chunk], 0.0)
    ark = jnp.where(tril, pair[:, chunk:, chunk:], 0.0)
    tinv = _unit_lower_inverse(lb, row, col, in_rows)
    vhb = heads(v)
    ub = _bmm(tinv.astype(BF16), (ls[:, :chunk] + _bmm(lk.astype(BF16), vhb)).astype(BF16)).astype(BF16)
    y = ls[:, chunk:] + _bmm(jnp.concatenate([ark, -arb], axis=2).astype(BF16),
                             jnp.concatenate([vhb, ub], axis=1))
    upd = _bmm_tn(jnp.concatenate([vhb, -ub], axis=1),
                  jnp.concatenate([heads(k2 * e_l), heads(bf * e_l)], axis=1))
    st_sc[...] = _split_heads(jnp.exp(glast), 0) * st + upd
    yf = _merge_heads(y, nbs)
    dev = yf - _group_sum(yf, bd) * (1.0 / HEAD_DIM)
    var = _group_sum(dev * dev, bd) * (1.0 / HEAD_DIM)
    yn = dev * lax.rsqrt(var + GN_EPS) * ln_g + ln_b
    out = (yn + bonus) * gg
    o_ref[...] = out.reshape(nbs, chunk, WIDTH)[:, 0:in_rows, :]

    @pl.when(c == pl.num_programs(1) - 1)
    def _():
        _store_state(sout_ref, prev_ref,
                     jnp.stack([jnp.stack([st_sc[b * HEADS + h].T for h in range(HEADS)])
                                for b in range(nbs)]))


def _store_stacked(sout_ref, prev_ref, s_sc):
    if prev_ref is None:
        sout_ref[...] = s_sc[...]
    else:
        n_prev = prev_ref.shape[0]
        sout_ref[0:n_prev] = prev_ref[...]
        sout_ref[n_prev] = s_sc[...]


def _gdn_steps_kernel(*refs, steps, has_prev):
    (xq_ref, xk_ref, xv_ref, xg_ref, hq_ref, hk_ref, hv_ref, wq_ref, wk_ref, wv_ref, pg_ref, hp_ref, og_ref,
     s0_ref) = refs[:14]
    prev_ref = refs[14] if has_prev else None
    o_ref, sout_ref, s_sc = refs[14 + int(has_prev):]
    h = pl.program_id(0)
    n_hist = B_CONV - 1

    def conv_silu(x_ref, h_ref, w_ref, t):
        rows = [h_ref[i] for i in range(n_hist)] + [x_ref[i] for i in range(steps)]
        acc = rows[t] * w_ref[0]
        for i in range(1, B_CONV):
            acc = acc + rows[t + i] * w_ref[i]
        return jax.nn.silu(acc)

    s_sc[...] = s0_ref[...]
    hp = hp_ref[...]
    for t in range(steps):
        q = conv_silu(xq_ref, hq_ref, wq_ref, t)
        k = conv_silu(xk_ref, hk_ref, wk_ref, t)
        v = conv_silu(xv_ref, hv_ref, wv_ref, t)
        q = q * lax.rsqrt(jnp.sum(q * q, axis=0, keepdims=True) + L2_EPS) * (HEAD_DIM ** -0.5)
        k = k * lax.rsqrt(jnp.sum(k * k, axis=0, keepdims=True) + L2_EPS)
        beta = jax.nn.sigmoid(pg_ref[t, pl.ds(h, 1), :])
        a = jnp.exp(-jnp.exp(hp[:, 0:1]) * _softplus(pg_ref[t, pl.ds(HEADS + h, 1), :] + hp[:, 1:2]))
        ks = jnp.zeros_like(v)
        for j in range(HEAD_DIM):
            ks = ks + k[j:j + 1, :] * (a * s_sc[j])
        w = beta * (v - ks)
        o = jnp.zeros_like(v)
        for j in range(HEAD_DIM):
            sj = a * s_sc[j] + k[j:j + 1, :] * w
            s_sc[j] = sj
            o = o + q[j:j + 1, :] * sj
        on = o * lax.rsqrt(jnp.mean(o * o, axis=0, keepdims=True) + NORM_EPS) * og_ref[...]
        o_ref[t] = on * jax.nn.silu(xg_ref[t])
    _store_stacked(sout_ref, prev_ref, s_sc)


def _steps_state_specs(prev, nseq, l):
    state_block = (HEAD_DIM, HEAD_DIM, nseq)
    s0_spec = pl.BlockSpec((None, None) + state_block, lambda h: (l, h, 0, 0, 0))
    if prev is None:
        return (s0_spec, None, (HEADS,) + state_block,
                pl.BlockSpec((None,) + state_block, lambda h: (h, 0, 0, 0)))
    n_prev = prev.shape[0]
    return (s0_spec, pl.BlockSpec((n_prev, None) + state_block, lambda h: (0, h, 0, 0, 0)),
            (n_prev + 1, HEADS) + state_block,
            pl.BlockSpec((n_prev + 1, None) + state_block, lambda h: (0, h, 0, 0, 0)))


def _gdn_steps(pb, pg, conv_state, s0_t, conv_w, a_log, dt_bias, onorm_g, prev, l):
    nseq = conv_state.shape[0]
    steps = pb.shape[0] // nseq
    xt = jnp.transpose(pb.reshape(nseq, steps, 4 * WIDTH), (1, 2, 0))
    pgt = jnp.transpose(pg.reshape(nseq, steps, GATE_LANES)[:, :, :2 * HEADS], (1, 2, 0))
    ht = jnp.transpose(conv_state, (1, 2, 0))
    wt = conv_w[l][:, :, None]
    hp = jnp.stack([a_log[l], dt_bias[l]], axis=1)[:, None, :]
    og = onorm_g[l][:, None]
    hb = lambda group: (lambda h: (0, group * HEADS + h, 0))
    x_spec = lambda group: pl.BlockSpec((steps, HEAD_DIM, nseq), hb(group))
    h_spec = lambda group: pl.BlockSpec((B_CONV - 1, HEAD_DIM, nseq), hb(group))
    w_spec = lambda group: pl.BlockSpec((B_CONV, HEAD_DIM, 1), hb(group))
    s0_spec, prev_spec, state_shape, state_spec = _steps_state_specs(prev, nseq, l)
    in_specs = [x_spec(0), x_spec(1), x_spec(2), x_spec(3), h_spec(0), h_spec(1), h_spec(2),
                w_spec(0), w_spec(1), w_spec(2), _whole(pgt),
                pl.BlockSpec((None, 1, 2), lambda h: (h, 0, 0)), _whole(og), s0_spec]
    operands = [xt, xt, xt, xt, ht, ht, ht, wt, wt, wt, pgt, hp, og, s0_t]
    if prev is not None:
        in_specs.append(prev_spec)
        operands.append(prev)
    out_t, states = pl.pallas_call(
        functools.partial(_gdn_steps_kernel, steps=steps, has_prev=prev is not None),
        grid=(HEADS,),
        in_specs=in_specs,
        out_specs=[pl.BlockSpec((steps, HEAD_DIM, nseq), lambda h: (0, h, 0)), state_spec],
        out_shape=[jax.ShapeDtypeStruct((steps, WIDTH, nseq), F32), jax.ShapeDtypeStruct(state_shape, F32)],
        scratch_shapes=[pltpu.VMEM((HEAD_DIM, HEAD_DIM, nseq), F32)],
        compiler_params=pltpu.CompilerParams(dimension_semantics=("parallel",)),
        name="gdn_steps",
    )(*operands)
    return jnp.transpose(out_t, (2, 0, 1)).reshape(nseq * steps, WIDTH), states


def _rwkv_steps_kernel(*refs, steps, has_prev):
    (xr_ref, xk_ref, xv_ref, xl_ref, pr_ref, pk_ref, pv_ref, pl_ref, mr_ref, mk_ref, mv_ref, ml_ref,
     hp_ref, w2_ref, a2_ref, g2_ref, s0_ref) = refs[:17]
    prev_ref = refs[17] if has_prev else None
    o_ref, sout_ref, s_sc = refs[17 + int(has_prev):]

    def lerp(x_ref, p_ref, m_ref, t):
        x = x_ref[t]
        before = p_ref[...] if t == 0 else x_ref[t - 1]
        return x + (before - x) * m_ref[...]

    w0, a0, k_k, k_a, r_k, ln_g, ln_b = (hp_ref[i] for i in range(7))
    s_sc[...] = s0_ref[...]
    for t in range(steps):
        wl, al, gl = _lora_rows(lerp(xl_ref, pl_ref, ml_ref, t))
        w_log = -_softplus(-(w0 + _dot(w2_ref[...], jnp.tanh(wl).astype(BF16)))) - 0.5
        w = jnp.exp(-jnp.exp(w_log))
        a = jax.nn.sigmoid(a0 + _dot(a2_ref[...], al.astype(BF16)))
        gg = _dot(g2_ref[...], jax.nn.sigmoid(gl).astype(BF16))
        r = lerp(xr_ref, pr_ref, mr_ref, t)
        k = lerp(xk_ref, pk_ref, mk_ref, t)
        v = lerp(xv_ref, pv_ref, mv_ref, t)
        kk = k * k_k
        kk = kk * lax.rsqrt(jnp.sum(kk * kk, axis=0, keepdims=True) + L2_EPS)
        k2 = k * (1.0 + (a - 1.0) * k_a)
        b = kk * a
        sk = jnp.zeros_like(v)
        for j in range(HEAD_DIM):
            sk = sk + kk[j:j + 1, :] * s_sc[j]
        y = jnp.zeros_like(v)
        for j in range(HEAD_DIM):
            sj = w[j:j + 1, :] * s_sc[j] - b[j:j + 1, :] * sk + k2[j:j + 1, :] * v
            s_sc[j] = sj
            y = y + r[j:j + 1, :] * sj
        mean = jnp.mean(y, axis=0, keepdims=True)
        var = jnp.mean(jnp.square(y - mean), axis=0, keepdims=True)
        yn = (y - mean) * lax.rsqrt(var + GN_EPS) * ln_g + ln_b
        bonus = jnp.sum(r * k2 * r_k, axis=0, keepdims=True) * v
        o_ref[t] = (yn + bonus) * gg
    _store_stacked(sout_ref, prev_ref, s_sc)


def _rwkv_steps(pc, shift_state, s0_t, cvec, w2b, a2b, g2b, prev, l):
    nseq = shift_state.shape[0]
    steps = pc.shape[0] // nseq
    xt = jnp.transpose(pc.reshape(nseq, steps, C_PROJ), (1, 2, 0))
    pt = jnp.transpose(shift_state, (1, 0))
    mu = cvec[l, 0, 0:C_PROJ][:, None]
    hp = cvec[l, 0, C_PROJ:].reshape(7, WIDTH)[:, :, None]
    w2t = jnp.transpose(w2b[l], (1, 0))
    a2t = jnp.transpose(a2b[l], (1, 0))
    g2t = jnp.transpose(g2b[l], (1, 0))
    hb3 = lambda group: (lambda h: (0, group * HEADS + h, 0))
    hb2 = lambda group: (lambda h: (group * HEADS + h, 0))
    lora3 = lambda h: (0, 3, 0)
    lora2 = lambda h: (3, 0)
    s0_spec, prev_spec, state_shape, state_spec = _steps_state_specs(prev, nseq, l)
    in_specs = ([pl.BlockSpec((steps, HEAD_DIM, nseq), hb3(g)) for g in range(3)]
                + [pl.BlockSpec((steps, WIDTH, nseq), lora3)]
                + [pl.BlockSpec((HEAD_DIM, nseq), hb2(g)) for g in range(3)] + [pl.BlockSpec((WIDTH, nseq), lora2)]
                + [pl.BlockSpec((HEAD_DIM, 1), hb2(g)) for g in range(3)] + [pl.BlockSpec((WIDTH, 1), lora2)]
                + [pl.BlockSpec((7, HEAD_DIM, 1), lambda h: (0, h, 0)),
                   pl.BlockSpec((HEAD_DIM, C_DECAY_LORA), lambda h: (h, 0)),
                   pl.BlockSpec((HEAD_DIM, C_RATE_LORA), lambda h: (h, 0)),
                   pl.BlockSpec((HEAD_DIM, C_GATE_LORA), lambda h: (h, 0)), s0_spec])
    operands = [xt] * 4 + [pt] * 4 + [mu] * 4 + [hp, w2t, a2t, g2t, s0_t]
    if prev is not None:
        in_specs.append(prev_spec)
        operands.append(prev)
    out_t, states = pl.pallas_call(
        functools.partial(_rwkv_steps_kernel, steps=steps, has_prev=prev is not None),
        grid=(HEADS,),
        in_specs=in_specs,
        out_specs=[pl.BlockSpec((steps, HEAD_DIM, nseq), lambda h: (0, h, 0)), state_spec],
        out_shape=[jax.ShapeDtypeStruct((steps, WIDTH, nseq), F32), jax.ShapeDtypeStruct(state_shape, F32)],
        scratch_shapes=[pltpu.VMEM((HEAD_DIM, HEAD_DIM, nseq), F32)],
        compiler_params=pltpu.CompilerParams(dimension_semantics=("parallel",)),
        name="rwkv_steps",
    )(*operands)
    return jnp.transpose(out_t, (2, 0, 1)).reshape(nseq * steps, WIDTH), states


def _round_up(x, m):
    return (x + m - 1) // m * m


def _pool_offsets(stride):
    offs = []
    a = 0
    for k in range(len(POOL_WINDOWS)):
        a = _round_up(a + (1 << k) * stride, SUBLANES)
        offs.append(a)
    return offs


def _pool_kernel(hist_ref, dp_ref, wbd_ref, scale_ref, out_ref, s1, s2, *, rows, stride, start):
    offs = _pool_offsets(stride)
    d0 = offs[-1]
    n = d0 + rows
    dp = dp_ref[...]
    s1[0:d0, :] = hist_ref[0]
    s1[d0:n, :] = dp
    lane = lax.broadcasted_iota(jnp.int32, (1, WIDTH), 1)
    src, dst = s1, s2
    for k, a in enumerate(offs):
        sh = (1 << k) * stride
        shifted = jnp.where(lane >= k * POOL_GROUP, src[a - sh:n - sh, :], 0.0)
        if k < len(offs) - 1:
            dst[a:n, :] = src[a:n, :] + shifted
            src, dst = dst, src
        else:
            sums = src[a:n, :] + shifted
    assert stride & (stride - 1) == 0
    pos = start + (lax.broadcasted_iota(jnp.int32, (rows, WIDTH), 0) >> (stride.bit_length() - 1))
    group = lax.broadcasted_iota(jnp.int32, (rows, WIDTH), 1) >> POOL_GROUP_SHIFT
    window = jnp.left_shift(POOL_WINDOWS[0], group)
    cnt = jnp.minimum(pos + 1, window).astype(F32)
    diff = sums / cnt - dp
    out_ref[...] = _dot(diff.astype(BF16), wbd_ref[...]) * scale_ref[...]


def _pool(hist, dp, wbd, scale, l, *, nseq, rows, stride, start):
    d0 = _pool_offsets(stride)[-1]
    return pl.pallas_call(
        functools.partial(_pool_kernel, rows=rows, stride=stride, start=start),
        grid=(nseq,),
        in_specs=[pl.BlockSpec((1, d0, WIDTH), lambda b: (b, 0, 0)),
                  pl.BlockSpec((rows, WIDTH), lambda b: (b, 0)),
                  _layer_block(wbd, l), _layer_block(scale, l)],
        out_specs=pl.BlockSpec((rows, WIDTH), lambda b: (b, 0)),
        out_shape=jax.ShapeDtypeStruct((nseq * rows, WIDTH), F32),
        scratch_shapes=[pltpu.VMEM((d0 + rows, WIDTH), F32)] * 2,
        compiler_params=pltpu.CompilerParams(dimension_semantics=("parallel",)),
        name="pool",
    )(hist, dp, wbd, scale)


def _ffn_kernel(x_ref, ma_ref, mb_ref, mc_ref, md_ref, wo_ref, g2_ref, wu_ref, wd_ref, gf_ref,
                o_ref, *, tf, final):
    mixed = None
    for i, m_ref in enumerate((ma_ref, mb_ref, mc_ref, md_ref)):
        part = _dot(m_ref[...].astype(BF16), wo_ref[i * WIDTH:(i + 1) * WIDTH, :])
        mixed = part if mixed is None else mixed + part
    x = x_ref[...] + mixed
    hm = _rms(x, g2_ref[...]).astype(BF16)
    down = None
    for j in range(D_FF // tf):
        up = jnp.maximum(_dot(hm, wu_ref[:, j * tf:(j + 1) * tf]), 0.0)
        part = _dot((up * up).astype(BF16), wd_ref[j * tf:(j + 1) * tf, :])
        down = part if down is None else down + part
    x = x + down
    if final:
        x = _rms(x, gf_ref[...])
    o_ref[...] = x


def _ffn(x, mixed, wo, g2, wu, wd, gf, l):
    t = x.shape[0]
    tm = min(ROW_TILE, t)
    row = lambda i: (i, 0)
    return pl.pallas_call(
        functools.partial(_ffn_kernel, tf=FF_TILE, final=l == DEPTH - 1),
        grid=(t // tm,),
        in_specs=[pl.BlockSpec((tm, D_MODEL), row)] + [pl.BlockSpec((tm, WIDTH), row)] * 4
        + [_layer_block(wo, l, single_buffer=True), _layer_block(g2, l),
           _layer_block(wu, l, single_buffer=True), _layer_block(wd, l, single_buffer=True), _whole(gf)],
        out_specs=pl.BlockSpec((tm, D_MODEL), row),
        out_shape=jax.ShapeDtypeStruct((t, D_MODEL), F32),
        compiler_params=pltpu.CompilerParams(dimension_semantics=("parallel",),
                                             vmem_limit_bytes=VMEM_LIMIT),
        name="ffn",
    )(x, *mixed, wo, g2, wu, wd, gf)


def _rows(v):
    return v[:, None, :]


def kernel(x_prompt, x_sample, state_b_conv, state_b_ssm, state_c_shift, state_c_wkv, state_d_pool, norm1_g, w_in, a_ws, a_bs, a_vnorm_g, b_conv_w, b_a_log, b_dt_bias, b_onorm_g, c_mu, c_w0, c_w2, c_a0, c_a2, c_g2, c_k_k, c_k_a, c_r_k, c_ln_g, c_ln_b, d_w, d_scale, w_out, norm2_g, w_up, w_down, final_g):
    nb, seq, _ = x_prompt.shape
    ns, dseq, _ = x_sample.shape
    a_chunk = a_ws.shape[-1]

    a_end = 2 * WIDTH
    b_end = a_end + 4 * WIDTH
    c_off = b_end + 2 * HEADS
    d_off = c_off + C_PROJ
    w_in_b = w_in.astype(BF16)
    w_cols = [w_in_b[:, :, :a_end], w_in_b[:, :, a_end:b_end], w_in_b[:, :, c_off:d_off], w_in_b[:, :, d_off:],
              jnp.pad(w_in_b[:, :, b_end:c_off], ((0, 0), (0, 0), (0, GATE_LANES - 2 * HEADS)))]
    wo = w_out.astype(BF16)
    wu = w_up.astype(BF16)
    wd = w_down.astype(BF16)
    g1 = _rows(norm1_g)
    g2 = _rows(norm2_g)
    gf = final_g[None, :]
    vg = _rows(a_vnorm_g)
    causal = jnp.tril(jnp.ones((a_chunk, a_chunk), dtype=bool))
    wm = jnp.where(causal, a_ws, 0.0)
    bias_t = jnp.repeat(jnp.transpose(a_bs, (0, 2, 1)), HEAD_DIM, axis=2)
    mm_p = jnp.transpose(wm, (0, 2, 1, 3)).reshape(DEPTH, a_chunk, HEADS * a_chunk).astype(BF16)
    srow = jnp.arange(ns * dseq)
    step_onehot = (srow[:, None] % dseq == jnp.arange(dseq)[None, :]).astype(F32)
    same_seq = (srow[:, None] // dseq) == (srow[None, :] // dseq)
    mm_s = jnp.einsum('it,lhts,js->lihj', step_onehot, wm[:, :, :dseq, :dseq], step_onehot,
                      precision=lax.Precision.HIGHEST)
    mm_s = jnp.where(same_seq[None, :, None, :], mm_s, 0.0).reshape(
        DEPTH, ns * dseq, HEADS * ns * dseq).astype(BF16)
    bias_s = jnp.tile(bias_t[:, :dseq], (1, ns, 1))
    lead = jnp.zeros((DEPTH, HEADS), F32)
    tail = jnp.zeros((DEPTH, GATE_LANES - 2 * HEADS), F32)
    gvec = jnp.concatenate([lead, b_a_log, tail, lead, b_dt_bias, tail, jnp.tile(b_onorm_g, (1, HEADS))],
                           axis=1)[:, None, :]
    cvec = jnp.concatenate([c_mu, c_w0, c_a0, c_k_k, c_k_a, c_r_k.reshape(DEPTH, WIDTH), c_ln_g, c_ln_b],
                           axis=1)[:, None, :]
    c_w2b, c_a2b, c_g2b = c_w2.astype(BF16), c_a2.astype(BF16), c_g2.astype(BF16)
    n_groups = len(POOL_WINDOWS)
    wbd = jnp.einsum('lgcd,gh->lgchd', d_w, jnp.eye(n_groups, dtype=F32)).reshape(
        DEPTH, WIDTH, WIDTH).astype(BF16)
    dscale = _rows(d_scale)
    lane_head = jnp.arange(WIDTH) // HEAD_DIM
    gate_lane = jnp.arange(GATE_LANES)
    bd = (lane_head[:, None] == lane_head[None, :]).astype(BF16)
    eb = jnp.tile((gate_lane[:, None] == lane_head[None, :]).astype(BF16), (2, 1))
    eg = jnp.tile((gate_lane[:, None] == lane_head[None, :] + HEADS).astype(BF16), (3, 1))

    def run_group(x, l, grp, prev_b, prev_c):
        nseq, length = grp["nseq"], grp["length"]
        conv_state = grp["b_conv"][grp["state_layer"](l)]
        if grp["prep"]:
            a_out, a_v, pb, pc, pd, pg, tails, tails_c = _inproj(
                x, g1, w_cols, grp["mm"], grp["bias"], vg, l,
                token_prep=(b_conv_w, gvec, bd, conv_state, length, cvec, c_w2b, c_a2b, c_g2b,
                          grp["c_shift"][grp["state_layer"](l)]))
            b_rows = tails.reshape(nseq, -1, SUBLANES, B_QKV)[:, -1]
            c_last = tails_c.reshape(nseq, -1, SUBLANES, C_PROJ)[:, -1, SUBLANES - 1]
        else:
            a_out, a_v, pb, pc, pd, pg = _inproj(x, g1, w_cols, grp["mm"], grp["bias"], vg, l)
            b_rows = pb.reshape(nseq, length, 4 * WIDTH)[:, :, :B_QKV]
            c_last = pc.reshape(nseq, length, C_PROJ)[:, length - 1]
        b_tail = jnp.concatenate([conv_state, b_rows], axis=1)[:, -(B_CONV - 1):]
        pb3 = pb.reshape(nseq, length, 4 * WIDTH)
        pc3 = pc.reshape(nseq, length, pc.shape[1])
        if grp["prep"]:
            b_out, b_state = _recurrent_call(
                _gdn_kernel, "gdn", [pb3, pg.reshape(nseq, length, GATE_LANES)], grp["b_ssm"],
                grp["state_layer"](l), grp["b_conv"],
                [(b_conv_w, l), (gvec, l), (bd, None), (eb, None), (eg, None)], prev_b,
                nbs=grp["nbs"], chunk=grp["chunk"], carry_width=B_QKV, prepped=True)
            c_out, c_state = _recurrent_call(
                _rwkv_kernel, "rwkv", [pc3], grp["c_wkv"], grp["state_layer"](l), grp["c_shift"],
                [(cvec, l), (c_w2b, l), (c_a2b, l), (c_g2b, l), (bd, None)], prev_c,
                nbs=grp["nbs"], chunk=grp["chunk"], carry_width=C_PROJ, prepped=True)
            b_out = b_out.reshape(nseq * length, WIDTH)
            c_out = c_out.reshape(nseq * length, WIDTH)
        else:
            b_out, b_state = _gdn_steps(pb, pg, conv_state, grp["b_ssm_t"], b_conv_w, b_a_log, b_dt_bias,
                                        b_onorm_g, prev_b, l)
            c_out, c_state = _rwkv_steps(pc, grp["c_shift"][l][:, 0], grp["c_wkv_t"], cvec, c_w2b, c_a2b, c_g2b,
                                         prev_c, l)
        d_out = grp["pool"](pd, l)
        x = _ffn(x, (a_out, b_out, c_out, d_out), wo, g2, wu, wd, gf, l)
        return x, a_v, b_tail, c_last, pd.reshape(nseq, length, WIDTH), b_state, c_state

    d0 = _pool_offsets(1)[-1]
    zero_hist = jnp.zeros((nb, d0, WIDTH), F32)
    prompt = dict(
        nseq=nb, length=seq, nbs=PROMPT_SEQS_PER_STEP, chunk=MIX_CHUNK, mm=mm_p, bias=bias_t,
        b_ssm=jnp.zeros((1, nb, HEADS, HEAD_DIM, HEAD_DIM), F32), c_wkv=jnp.zeros((1, nb, HEADS, HEAD_DIM, HEAD_DIM), F32),
        b_conv=jnp.zeros((1, nb, B_CONV - 1, B_QKV), F32), c_shift=jnp.zeros((1, nb, 1, C_PROJ), F32),
        state_layer=lambda l: 0, prep=True,
        pool=lambda pd, l: _pool(zero_hist, pd, wbd, dscale, l, nseq=nb, rows=seq, stride=1, start=0))

    def sample_pool(pd, l):
        pd_t = jnp.transpose(pd.reshape(ns, dseq, WIDTH), (1, 0, 2)).reshape(dseq * ns, WIDTH)
        hist = jnp.transpose(state_d_pool[l], (1, 0, 2)).reshape(1, POOL_BUF * ns, WIDTH)
        out_t = _pool(hist, pd_t, wbd, dscale, l, nseq=1, rows=dseq * ns, stride=ns, start=PAST_LEN)
        return jnp.transpose(out_t.reshape(dseq, ns, WIDTH), (1, 0, 2)).reshape(ns * dseq, WIDTH)

    sample = dict(
        nseq=ns, length=dseq, nbs=SAMPLE_SEQS_PER_STEP, chunk=SUBLANES, mm=mm_s, bias=bias_s,
        b_ssm_t=jnp.transpose(state_b_ssm, (0, 2, 3, 4, 1)), c_wkv_t=jnp.transpose(state_c_wkv, (0, 2, 3, 4, 1)),
        b_conv=state_b_conv, c_shift=state_c_shift[:, :, None, :],
        state_layer=lambda l: l, prep=False, pool=sample_pool)

    xp = x_prompt.reshape(nb * seq, D_MODEL)
    xs = x_sample.reshape(ns * dseq, D_MODEL)
    p_bc, p_cs, p_dp, s_av, s_bc, s_cs, s_dp = [], [], [], [], [], [], []
    p_bs = p_cw = s_bs = s_cw = None
    for l in range(DEPTH):
        xp, _, b_tail, c_last, pd3, p_bs, p_cw = run_group(xp, l, prompt, p_bs, p_cw)
        p_bs = p_bs if l else p_bs[None]
        p_cw = p_cw if l else p_cw[None]
        p_bc.append(b_tail)
        p_cs.append(c_last)
        p_dp.append(pd3[:, seq - POOL_BUF:])

        xs, a_v, b_tail, c_last, pd3, s_bs, s_cw = run_group(xs, l, sample, s_bs, s_cw)
        s_bs = s_bs if l else s_bs[None]
        s_cw = s_cw if l else s_cw[None]
        s_av.append(a_v.reshape(ns, dseq, WIDTH))
        s_bc.append(b_tail)
        s_cs.append(c_last)
        s_dp.append(jnp.concatenate([state_d_pool[l], pd3], axis=1)[:, -POOL_BUF:])

    return (xp.reshape(nb, seq, D_MODEL), xs.reshape(ns, dseq, D_MODEL),
            jnp.stack(p_bc), p_bs, jnp.stack(p_cs), p_cw, jnp.stack(p_dp),
            jnp.stack(s_av), jnp.stack(s_bc), jnp.transpose(s_bs, (0, 4, 1, 2, 3)), jnp.stack(s_cs),
            jnp.transpose(s_cw, (0, 4, 1, 2, 3)), jnp.stack(s_dp))
```

```python
import functools

import jax
import jax.numpy as jnp
from jax import lax
from jax.experimental import pallas as pl
from jax.experimental.pallas import tpu as pltpu

F32 = jnp.float32
BF16 = jnp.bfloat16

D_MODEL = 1024
DEPTH = 2
HEADS = 4
HEAD_DIM = 64
HEAD_SHIFT = HEAD_DIM.bit_length() - 1
WIDTH = HEADS * HEAD_DIM
B_QKV = 3 * WIDTH
B_CONV = 4
C_PROJ = 4 * WIDTH
C_DECAY_LORA = 64
C_RATE_LORA = 64
C_GATE_LORA = 128
POOL_WINDOWS = (2, 4, 8, 16)
POOL_BUF = max(POOL_WINDOWS) - 1
POOL_GROUP = WIDTH // len(POOL_WINDOWS)
POOL_GROUP_SHIFT = POOL_GROUP.bit_length() - 1
D_FF = 4 * D_MODEL
PAST_LEN = 16384
NORM_EPS = 1e-6
L2_EPS = 1e-6
GN_EPS = 64e-5
V7X_LANES = 128
V7X_SUBLANES = 8
V7X_VMEM_BYTES = 64 * 1024 * 1024
GATE_LANES = V7X_LANES
SUBLANES = V7X_SUBLANES
VMEM_LIMIT = V7X_VMEM_BYTES * 3 // 4
MIX_CHUNK = 64
PROMPT_SEQS_PER_STEP = 8
ROW_TILE = 512
FF_TILE = 1024


def _rms(x, g):
    return x * lax.rsqrt(jnp.mean(x * x, axis=-1, keepdims=True) + NORM_EPS) * g


def _softplus(x):
    return jnp.maximum(x, 0.0) + jnp.log1p(jnp.exp(-jnp.abs(x)))


def _dot(a, b):
    return jnp.dot(a, b, preferred_element_type=F32)


def _bmm(a, b):
    return jnp.einsum('bij,bjk->bik', a, b, preferred_element_type=F32)


def _bmm_nt(a, b):
    return jnp.einsum('bik,bjk->bij', a, b, preferred_element_type=F32)


def _bmm_tn(a, b):
    return jnp.einsum('bki,bkj->bij', a, b, preferred_element_type=F32)


def _layer_block(arr, l, single_buffer=False):
    shape = arr.shape[1:]
    index = lambda *_: (l,) + (0,) * len(shape)
    if single_buffer:
        return pl.BlockSpec((None,) + shape, index, pipeline_mode=pl.Buffered(1))
    return pl.BlockSpec((None,) + shape, index)


def _whole(arr):
    return pl.BlockSpec(arr.shape, lambda *_: (0,) * arr.ndim)


def _split_heads(x, lane0):
    return jnp.stack([x[b, :, lane0 + h * HEAD_DIM:lane0 + (h + 1) * HEAD_DIM]
                      for b in range(x.shape[0]) for h in range(HEADS)])


def _merge_heads(x, nbs):
    return jnp.concatenate([jnp.concatenate([x[b * HEADS + h] for h in range(HEADS)], axis=1)
                            for b in range(nbs)], axis=0)


def _group_sum(x, bd):
    hi = x.astype(BF16)
    lo = (x - hi.astype(F32)).astype(BF16)
    return _dot(hi, bd) + _dot(lo, bd)


def _split3(x):
    p1 = x.astype(BF16)
    r1 = x - p1.astype(F32)
    p2 = r1.astype(BF16)
    return p1, p2, (r1 - p2.astype(F32)).astype(BF16)


def _cumsum_rows(tril_b, x):
    w = x.shape[1]
    y = _dot(tril_b, jnp.concatenate(_split3(x), axis=1))
    return y[:, :w] + (y[:, w:2 * w] + y[:, 2 * w:])


def _unit_lower_inverse(lm, row, col, n_valid):
    def sub_diag_block(shift):
        return (((row >> (shift + 1)) == (col >> (shift + 1)))
                & (((row >> shift) & 1) == 1) & (((col >> shift) & 1) == 0))

    m = (row == col).astype(F32) - jnp.where(sub_diag_block(0), lm, 0.0)
    shift = 1
    while (1 << shift) < n_valid:
        cs = jnp.where(sub_diag_block(shift), lm, 0.0).astype(BF16)
        mb = m.astype(BF16)
        m = m - _bmm(_bmm(mb, cs).astype(BF16), mb)
        shift += 1
    return m


def _lora_inputs(x, lane0):
    a = lane0 + C_DECAY_LORA
    b = a + C_RATE_LORA
    return x[:, lane0:a], x[:, a:b], x[:, b:b + C_GATE_LORA]


def _lora_rows(x):
    a = C_DECAY_LORA
    b = a + C_RATE_LORA
    return x[0:a], x[a:b], x[b:b + C_GATE_LORA]


def _transpose_rows(a):
    n = a.shape[0]
    if n < GATE_LANES:
        a = jnp.concatenate([a, jnp.zeros((GATE_LANES - n, a.shape[1]), a.dtype)], axis=0)
    return a.T[:, :n]


def _conv_silu(raw, hist, tail_ref, lo, cw):
    tm = raw.shape[0]
    cols = slice(lo, lo + WIDTH)
    xfull = jnp.concatenate([hist[:, cols], raw], axis=0)
    conv = pltpu.roll(xfull, 3, axis=0)[SUBLANES:, :] * cw[0:1, cols]
    conv = conv + pltpu.roll(xfull, 2, axis=0)[SUBLANES:, :] * cw[1:2, cols]
    conv = conv + pltpu.roll(xfull, 1, axis=0)[SUBLANES:, :] * cw[2:3, cols]
    conv = conv + raw * cw[3:4, cols]
    tail = raw[tm - SUBLANES:tm, :]
    hist[:, cols] = tail
    tail_ref[:, cols] = tail
    return jax.nn.silu(conv)


def _inproj_kernel(*refs, r, prep, tiles_per_seq):
    (x_ref, g_ref, wa_ref, wb_ref, wc_ref, wd_ref, wg_ref, mm_ref, bias_ref, vg_ref) = refs[:10]
    if prep:
        cw_ref, gvec_ref, bd_ref, cbuf_ref, cvec_ref, w2_ref, a2_ref, g2_ref, sh_ref = refs[10:19]
        a_ref, v_ref, pb_ref, pc_ref, pd_ref, pg_ref, tail_ref, tailc_ref, hist, hist_c = refs[19:]
    else:
        a_ref, v_ref, pb_ref, pc_ref, pd_ref, pg_ref = refs[10:]
    h = _rms(x_ref[...], g_ref[...]).astype(BF16)

    def gmlp_gates(pa):
        v = _rms(jax.nn.gelu(pa[:, WIDTH:]), vg_ref[...])
        v_ref[...] = v
        return jax.nn.gelu(pa[:, :WIDTH]), v

    def gmlp_mix(u, v):
        lane_head = lax.broadcasted_iota(jnp.int32, (r, WIDTH), 1) >> HEAD_SHIFT
        mm = mm_ref[...]
        bias = bias_ref[...]
        for i in range(u.shape[0] // r):
            vc = v[i * r:(i + 1) * r]
            per_head = jnp.concatenate([jnp.where(lane_head == h, vc, 0.0).astype(BF16)
                                        for h in range(HEADS)], axis=0)
            a_ref[i * r:(i + 1) * r, :] = u[i * r:(i + 1) * r] * (_dot(mm, per_head) + bias)

    if prep:
        n_hist = B_CONV - 1

        @pl.when(pl.program_id(0) % tiles_per_seq == 0)
        def _():
            hist[...] = jnp.zeros(hist.shape, F32)
            hist[SUBLANES - n_hist:SUBLANES, :] = cbuf_ref[0]
            hist_c[...] = jnp.zeros(hist_c.shape, F32)
            hist_c[SUBLANES - 1:SUBLANES, :] = sh_ref[0]

        cvec = cvec_ref[...]
        mu = cvec[:, 0:C_PROJ]
        w0, a0, k_k, k_a, r_k = (cvec[:, C_PROJ + i * WIDTH:C_PROJ + (i + 1) * WIDTH] for i in range(5))

        def shifted_lerp(cp, lo):
            cols = slice(lo, lo + WIDTH)
            prev = pltpu.roll(jnp.concatenate([hist_c[:, cols], cp], axis=0), 1, axis=0)[SUBLANES:, :]
            tail = cp[cp.shape[0] - SUBLANES:, :]
            hist_c[:, cols] = tail
            tailc_ref[:, cols] = tail
            return cp + (prev - cp) * mu[:, cols]

        cw = cw_ref[...]
        bd = bd_ref[...]
        gvec = gvec_ref[...]
        raw_q = _dot(h, wb_ref[:, 0:WIDTH])
        raw_k = _dot(h, wb_ref[:, WIDTH:2 * WIDTH])
        raw_v = _dot(h, wb_ref[:, 2 * WIDTH:3 * WIDTH])
        qf = _conv_silu(raw_q, hist, tail_ref, 0, cw)
        pb_ref[:, 0:WIDTH] = qf * lax.rsqrt(_dot((qf * qf).astype(BF16), bd) + L2_EPS) * (HEAD_DIM ** -0.5)
        raw_gate = _dot(h, wb_ref[:, 3 * WIDTH:4 * WIDTH])
        pg = _dot(h, wg_ref[...])
        kf = _conv_silu(raw_k, hist, tail_ref, WIDTH, cw)
        pb_ref[:, WIDTH:2 * WIDTH] = kf * lax.rsqrt(_dot((kf * kf).astype(BF16), bd) + L2_EPS)
        cp_l = _dot(h, wc_ref[:, 3 * WIDTH:4 * WIDTH])
        cp_k = _dot(h, wc_ref[:, WIDTH:2 * WIDTH])
        pb_ref[:, 2 * WIDTH:3 * WIDTH] = _conv_silu(raw_v, hist, tail_ref, 2 * WIDTH, cw)
        pa = _dot(h, wa_ref[...])
        pb_ref[:, 3 * WIDTH:4 * WIDTH] = jax.nn.silu(raw_gate)
        alog = gvec[:, 0:GATE_LANES]
        dtb = gvec[:, GATE_LANES:2 * GATE_LANES]
        lane = lax.broadcasted_iota(jnp.int32, pg.shape, 1)
        pg_ref[...] = jnp.where(lane < HEADS, jax.nn.sigmoid(pg), -jnp.exp(alog) * _softplus(pg + dtb))
        cp_r = _dot(h, wc_ref[:, 0:WIDTH])
        cp_v = _dot(h, wc_ref[:, 2 * WIDTH:3 * WIDTH])
        xm_l = shifted_lerp(cp_l, 3 * WIDTH)
        wl, al, gl = _lora_inputs(xm_l, 0)
        w_log = -_softplus(-(w0 + _dot(jnp.tanh(wl).astype(BF16), w2_ref[...]))) - 0.5
        pc_ref[:, 5 * WIDTH:6 * WIDTH] = -jnp.exp(w_log)
        a = jax.nn.sigmoid(a0 + _dot(al.astype(BF16), a2_ref[...]))
        pc_ref[:, 6 * WIDTH:7 * WIDTH] = _dot(jax.nn.sigmoid(gl).astype(BF16), g2_ref[...])
        pd_ref[...] = _dot(h, wd_ref[...])
        k = shifted_lerp(cp_k, WIDTH)
        kkraw = k * k_k
        kkn = kkraw * lax.rsqrt(_dot((kkraw * kkraw).astype(BF16), bd) + L2_EPS)
        k2 = k * (1.0 + (a - 1.0) * k_a)
        pc_ref[:, WIDTH:2 * WIDTH] = k2
        pc_ref[:, 3 * WIDTH:4 * WIDTH] = kkn
        pc_ref[:, 4 * WIDTH:5 * WIDTH] = kkn * a
        rr = shifted_lerp(cp_r, 0)
        vv = shifted_lerp(cp_v, 2 * WIDTH)
        pc_ref[:, 0:WIDTH] = rr
        pc_ref[:, 2 * WIDTH:3 * WIDTH] = vv
        pc_ref[:, 7 * WIDTH:8 * WIDTH] = _group_sum(rr * k2 * r_k, bd) * vv
        gmlp_mix(*gmlp_gates(pa))
    else:
        pa = _dot(h, wa_ref[...])
        pb_ref[...] = _dot(h, wb_ref[...])
        u, v = gmlp_gates(pa)
        pc_ref[...] = _dot(h, wc_ref[...])
        pd_ref[...] = _dot(h, wd_ref[...])
        pg_ref[...] = _dot(h, wg_ref[...])
        gmlp_mix(u, v)


def _inproj(x, g, ws, mm, bias, vg, l, token_prep=None):
    t = x.shape[0]
    r = mm.shape[1]
    tm = max(r, min(ROW_TILE, t))
    widths = [WIDTH, WIDTH] + [w.shape[2] for w in ws[1:]]
    in_specs = ([pl.BlockSpec((tm, D_MODEL), lambda i: (i, 0)), _layer_block(g, l)]
                + [_layer_block(w, l, single_buffer=True) for w in ws]
                + [_layer_block(mm, l), _layer_block(bias, l), _layer_block(vg, l)])
    operands = [x, g, *ws, mm, bias, vg]
    out_specs = [pl.BlockSpec((tm, n), lambda i: (i, 0)) for n in widths]
    out_shape = [jax.ShapeDtypeStruct((t, n), F32) for n in widths]
    scratch = []
    tiles_per_seq = 1
    if token_prep is not None:
        conv_w, gvec, bd, conv_state, seq_len, cvec, w2, a2, g2, shift_state = token_prep
        tiles_per_seq = seq_len // tm
        per_seq = lambda i: (i // tiles_per_seq, 0, 0)
        in_specs += [_layer_block(conv_w, l), _layer_block(gvec, l), _whole(bd),
                     pl.BlockSpec((1,) + conv_state.shape[1:], per_seq),
                     _layer_block(cvec, l), _layer_block(w2, l), _layer_block(a2, l), _layer_block(g2, l),
                     pl.BlockSpec((1,) + shift_state.shape[1:], per_seq)]
        operands += [conv_w, gvec, bd, conv_state, cvec, w2, a2, g2, shift_state]
        widths[3] = 8 * WIDTH
        out_specs[3] = pl.BlockSpec((tm, 8 * WIDTH), lambda i: (i, 0))
        out_shape[3] = jax.ShapeDtypeStruct((t, 8 * WIDTH), F32)
        for w in (B_QKV, C_PROJ):
            out_specs.append(pl.BlockSpec((SUBLANES, w), lambda i: (i, 0)))
            out_shape.append(jax.ShapeDtypeStruct((t // tm * SUBLANES, w), F32))
            scratch.append(pltpu.VMEM((SUBLANES, w), F32))
    return pl.pallas_call(
        functools.partial(_inproj_kernel, r=r, prep=token_prep is not None, tiles_per_seq=tiles_per_seq),
        grid=(t // tm,),
        in_specs=in_specs,
        out_specs=out_specs,
        out_shape=out_shape,
        scratch_shapes=scratch,
        compiler_params=pltpu.CompilerParams(
            dimension_semantics=("arbitrary" if token_prep is not None else "parallel",),
            vmem_limit_bytes=VMEM_LIMIT),
        name="inproj",
    )(*operands)


def _store_state(sout_ref, prev_ref, new_state):
    if prev_ref is None:
        sout_ref[...] = new_state
    else:
        n_prev = prev_ref.shape[0]
        sout_ref[0:n_prev] = prev_ref[...]
        sout_ref[n_prev] = new_state


def _recurrent_call(kernel_fn, name, x_blocks, s0, consts, prev, *, nbs, chunk):
    nseq, length, _ = x_blocks[0].shape
    blk = lambda b, c: (b, c, 0)
    state_block = (nbs, HEADS, HEAD_DIM, HEAD_DIM)
    in_specs = [pl.BlockSpec((nbs, chunk, x.shape[2]), blk) for x in x_blocks]
    in_specs.append(pl.BlockSpec(state_block, lambda b, c: (b, 0, 0, 0)))
    in_specs += [_whole(a) if l is None else _layer_block(a, l) for a, l in consts]
    operands = list(x_blocks) + [s0] + [a for a, _ in consts]
    if prev is None:
        state_shape = (nseq, HEADS, HEAD_DIM, HEAD_DIM)
        state_spec = pl.BlockSpec(state_block, lambda b, c: (b, 0, 0, 0))
    else:
        n_prev = prev.shape[0]
        in_specs.append(pl.BlockSpec((n_prev,) + state_block, lambda b, c: (0, b, 0, 0, 0)))
        operands.append(prev)
        state_shape = (n_prev + 1, nseq, HEADS, HEAD_DIM, HEAD_DIM)
        state_spec = pl.BlockSpec((n_prev + 1,) + state_block, lambda b, c: (0, b, 0, 0, 0))
    return pl.pallas_call(
        functools.partial(kernel_fn, nbs=nbs, chunk=chunk, n_x=len(x_blocks), n_consts=len(consts),
                          has_prev=prev is not None),
        grid=(nseq // nbs, length // chunk),
        in_specs=in_specs,
        out_specs=[pl.BlockSpec((nbs, chunk, WIDTH), blk), state_spec],
        out_shape=[jax.ShapeDtypeStruct((nseq, length, WIDTH), F32),
                   jax.ShapeDtypeStruct(state_shape, F32)],
        scratch_shapes=[pltpu.VMEM((nbs * HEADS, HEAD_DIM, HEAD_DIM), F32)],
        compiler_params=pltpu.CompilerParams(dimension_semantics=("parallel", "arbitrary")),
        name=name,
    )(*operands)


def _unpack_refs(refs, n_x, n_consts, has_prev):
    x_refs = refs[:n_x]
    s0_ref = refs[n_x]
    consts = refs[n_x + 1:n_x + 1 + n_consts]
    pos = n_x + 1 + n_consts
    prev_ref = refs[pos] if has_prev else None
    o_ref, sout_ref, state_sc = refs[pos + int(has_prev):]
    return x_refs, s0_ref, consts, prev_ref, o_ref, sout_ref, state_sc


def _gdn_kernel(*refs, nbs, chunk, n_x, n_consts, has_prev):
    ((pb_ref, pg_ref), s0_ref, (vec_ref, bd_ref, eb_ref, eg_ref), prev_ref,
     o_ref, sout_ref, s_sc) = _unpack_refs(refs, n_x, n_consts, has_prev)
    c = pl.program_id(1)
    nb = nbs * HEADS
    rows = nbs * chunk

    @pl.when(c == 0)
    def _():
        s_sc[...] = s0_ref[...].reshape(nb, HEAD_DIM, HEAD_DIM)

    pb = pb_ref[...]
    beta_all = g_all = pg_ref[...]
    og = vec_ref[...][:, 2 * GATE_LANES:2 * GATE_LANES + WIDTH]
    bd = bd_ref[...]
    qkv = pb[:, :, 0:B_QKV].reshape(rows, B_QKV)
    gate = pb[:, :, B_QKV:B_QKV + WIDTH].reshape(rows, WIDTH)
    qn = qkv[:, 0:WIDTH]
    kn = qkv[:, WIDTH:2 * WIDTH]
    vf = qkv[:, 2 * WIDTH:]

    row = lax.broadcasted_iota(jnp.int32, (chunk, chunk), 0)
    col = lax.broadcasted_iota(jnp.int32, (chunk, chunk), 1)
    tril = row >= col
    strict = row > col
    trilb = tril.astype(BF16)
    b1, b2, _ = _split3(beta_all.reshape(rows, GATE_LANES))
    beta_f = _dot(jnp.concatenate([b1, b2], axis=1), eb_ref[...])
    gc_small = [_cumsum_rows(trilb, g_all[b]) for b in range(nbs)]
    gc = _dot(jnp.concatenate(_split3(jnp.concatenate(gc_small, axis=0)), axis=1),
              eg_ref[...]).reshape(nbs, chunk, WIDTH)
    glast = gc[:, chunk - 1:chunk, :]
    e_g = jnp.exp(gc).reshape(rows, WIDTH)
    e_gl = jnp.exp(glast - gc).reshape(rows, WIDTH)
    bk = beta_f * kn

    def heads(x):
        return _split_heads(x.reshape(nbs, chunk, WIDTH).astype(BF16), 0)

    qkk = _bmm_nt(jnp.concatenate([heads(qn), heads(bk)], axis=1), heads(kn))
    grow_all = [_transpose_rows(g) for g in gc_small]
    gcol = jnp.stack([gc[b, :, h * HEAD_DIM:h * HEAD_DIM + 1] for b in range(nbs) for h in range(HEADS)])
    grow = jnp.stack([grow_all[b][HEADS + h:HEADS + h + 1, :] for b in range(nbs) for h in range(HEADS)])
    decay = jnp.where(tril, jnp.exp(jnp.minimum(gcol - grow, 0.0)), 0.0)
    qk = qkk[:, :chunk] * decay
    lm = jnp.where(strict, qkk[:, chunk:] * decay, 0.0)
    tinv = _unit_lower_inverse(lm, row, col, chunk)
    rhs = jnp.concatenate([heads(beta_f * vf), heads(bk * e_g)], axis=2)
    uw = _bmm(tinv.astype(BF16), rhs)
    u = uw[:, :, :HEAD_DIM]
    wk = uw[:, :, HEAD_DIM:]
    s = s_sc[...]
    ws = _bmm(jnp.concatenate([wk.astype(BF16), heads(qn * e_g)], axis=1),
              s.astype(BF16))
    wnb = (u - ws[:, :chunk]).astype(BF16)
    o = ws[:, chunk:] + _bmm(qk.astype(BF16), wnb)
    s_sc[...] = _split_heads(jnp.exp(glast), 0) * s + _bmm_tn(heads(kn * e_gl), wnb)
    of = _merge_heads(o, nbs)
    of = of * lax.rsqrt(_group_sum(of * of, bd) * (1.0 / HEAD_DIM) + NORM_EPS) * og * gate
    o_ref[...] = of.reshape(nbs, chunk, WIDTH)

    @pl.when(c == pl.num_programs(1) - 1)
    def _():
        _store_state(sout_ref, prev_ref, s_sc[...].reshape(nbs, HEADS, HEAD_DIM, HEAD_DIM))


def _rwkv_kernel(*refs, nbs, chunk, n_x, n_consts, has_prev):
    ((pc_ref,), s0_ref, (vec_ref, bd_ref), prev_ref, o_ref, sout_ref, st_sc) = _unpack_refs(
        refs, n_x, n_consts, has_prev)
    c = pl.program_id(1)
    rows = nbs * chunk

    @pl.when(c == 0)
    def _():
        for b in range(nbs):
            for h in range(HEADS):
                st_sc[b * HEADS + h] = s0_ref[b, h].T

    vec = vec_ref[...]
    ln_g, ln_b = (vec[:, C_PROJ + i * WIDTH:C_PROJ + (i + 1) * WIDTH] for i in (5, 6))
    bd = bd_ref[...]
    xp = pc_ref[...].reshape(rows, 8 * WIDTH)
    r, k2, v, kkn, bf, logw, gg, bonus = (xp[:, i * WIDTH:(i + 1) * WIDTH] for i in range(8))

    row = lax.broadcasted_iota(jnp.int32, (chunk, chunk), 0)
    col = lax.broadcasted_iota(jnp.int32, (chunk, chunk), 1)
    tril = row >= col
    strict = row > col
    trilb = tril.astype(BF16)

    def heads(x):
        return _split_heads(x.reshape(nbs, chunk, WIDTH).astype(BF16), 0)

    logw3 = logw.reshape(nbs, chunk, WIDTH)
    glog3 = jnp.stack([_cumsum_rows(trilb, logw3[b]) for b in range(nbs)])
    glast = glog3[:, chunk - 1:chunk, :]
    glog = glog3.reshape(rows, WIDTH)
    e_n = jnp.exp(-glog)
    e_l = jnp.exp(glast - glog3).reshape(rows, WIDTH)
    lhs = jnp.concatenate([heads(kkn * jnp.exp(glog - logw)), heads(r * jnp.exp(glog))], axis=1)
    rhs = jnp.concatenate([heads(bf * e_n), heads(k2 * e_n)], axis=1)
    pair = _bmm_nt(lhs, rhs)
    st = st_sc[...]
    ls = _bmm_nt(lhs, st.astype(BF16))
    lb = jnp.where(strict, pair[:, :chunk, :chunk], 0.0)
    lk = jnp.where(strict, pair[:, :chunk, chunk:], 0.0)
    arb = jnp.where(tril, pair[:, chunk:, :chunk], 0.0)
    ark = jnp.where(tril, pair[:, chunk:, chunk:], 0.0)
    tinv = _unit_lower_inverse(lb, row, col, chunk)
    vhb = heads(v)
    ub = _bmm(tinv.astype(BF16), (ls[:, :chunk] + _bmm(lk.astype(BF16), vhb)).astype(BF16)).astype(BF16)
    y = ls[:, chunk:] + _bmm(jnp.concatenate([ark, -arb], axis=2).astype(BF16),
                             jnp.concatenate([vhb, ub], axis=1))
    upd = _bmm_tn(jnp.concatenate([vhb, -ub], axis=1),
                  jnp.concatenate([heads(k2 * e_l), heads(bf * e_l)], axis=1))
    st_sc[...] = _split_heads(jnp.exp(glast), 0) * st + upd
    yf = _merge_heads(y, nbs)
    dev = yf - _group_sum(yf, bd) * (1.0 / HEAD_DIM)
    var = _group_sum(dev * dev, bd) * (1.0 / HEAD_DIM)
    yn = dev * lax.rsqrt(var + GN_EPS) * ln_g + ln_b
    out = (yn + bonus) * gg
    o_ref[...] = out.reshape(nbs, chunk, WIDTH)

    @pl.when(c == pl.num_programs(1) - 1)
    def _():
        _store_state(sout_ref, prev_ref,
                     jnp.stack([jnp.stack([st_sc[b * HEADS + h].T for h in range(HEADS)])
                                for b in range(nbs)]))


def _store_stacked(sout_ref, prev_ref, s_sc):
    if prev_ref is None:
        sout_ref[...] = s_sc[...]
    else:
        n_prev = prev_ref.shape[0]
        sout_ref[0:n_prev] = prev_ref[...]
        sout_ref[n_prev] = s_sc[...]


def _gdn_steps_kernel(*refs, steps, has_prev):
    (xq_ref, xk_ref, xv_ref, xg_ref, hq_ref, hk_ref, hv_ref, wq_ref, wk_ref, wv_ref, pg_ref, hp_ref, og_ref,
     s0_ref) = refs[:14]
    prev_ref = refs[14] if has_prev else None
    o_ref, sout_ref, s_sc = refs[14 + int(has_prev):]
    h = pl.program_id(0)
    n_hist = B_CONV - 1

    def conv_silu(x_ref, h_ref, w_ref, t):
        rows = [h_ref[i] for i in range(n_hist)] + [x_ref[i] for i in range(steps)]
        acc = rows[t] * w_ref[0]
        for i in range(1, B_CONV):
            acc = acc + rows[t + i] * w_ref[i]
        return jax.nn.silu(acc)

    s_sc[...] = s0_ref[...]
    hp = hp_ref[...]
    for t in range(steps):
        q = conv_silu(xq_ref, hq_ref, wq_ref, t)
        k = conv_silu(xk_ref, hk_ref, wk_ref, t)
        v = conv_silu(xv_ref, hv_ref, wv_ref, t)
        q = q * lax.rsqrt(jnp.sum(q * q, axis=0, keepdims=True) + L2_EPS) * (HEAD_DIM ** -0.5)
        k = k * lax.rsqrt(jnp.sum(k * k, axis=0, keepdims=True) + L2_EPS)
        beta = jax.nn.sigmoid(pg_ref[t, pl.ds(h, 1), :])
        a = jnp.exp(-jnp.exp(hp[:, 0:1]) * _softplus(pg_ref[t, pl.ds(HEADS + h, 1), :] + hp[:, 1:2]))
        ks = jnp.zeros_like(v)
        for j in range(HEAD_DIM):
            ks = ks + k[j:j + 1, :] * (a * s_sc[j])
        w = beta * (v - ks)
        o = jnp.zeros_like(v)
        for j in range(HEAD_DIM):
            sj = a * s_sc[j] + k[j:j + 1, :] * w
            s_sc[j] = sj
            o = o + q[j:j + 1, :] * sj
        on = o * lax.rsqrt(jnp.mean(o * o, axis=0, keepdims=True) + NORM_EPS) * og_ref[...]
        o_ref[t] = on * jax.nn.silu(xg_ref[t])
    _store_stacked(sout_ref, prev_ref, s_sc)


def _steps_state_specs(prev, nseq, l):
    state_block = (HEAD_DIM, HEAD_DIM, nseq)
    s0_spec = pl.BlockSpec((None, None) + state_block, lambda h: (l, h, 0, 0, 0))
    if prev is None:
        return (s0_spec, None, (HEADS,) + state_block,
                pl.BlockSpec((None,) + state_block, lambda h: (h, 0, 0, 0)))
    n_prev = prev.shape[0]
    return (s0_spec, pl.BlockSpec((n_prev, None) + state_block, lambda h: (0, h, 0, 0, 0)),
            (n_prev + 1, HEADS) + state_block,
            pl.BlockSpec((n_prev + 1, None) + state_block, lambda h: (0, h, 0, 0, 0)))


def _gdn_steps(pb, pg, conv_state, s0_t, conv_w, a_log, dt_bias, onorm_g, prev, l):
    nseq = conv_state.shape[0]
    steps = pb.shape[0] // nseq
    xt = jnp.transpose(pb.reshape(nseq, steps, 4 * WIDTH), (1, 2, 0))
    pgt = jnp.transpose(pg.reshape(nseq, steps, GATE_LANES)[:, :, :2 * HEADS], (1, 2, 0))
    ht = jnp.transpose(conv_state, (1, 2, 0))
    wt = conv_w[l][:, :, None]
    hp = jnp.stack([a_log[l], dt_bias[l]], axis=1)[:, None, :]
    og = onorm_g[l][:, None]
    hb = lambda group: (lambda h: (0, group * HEADS + h, 0))
    x_spec = lambda group: pl.BlockSpec((steps, HEAD_DIM, nseq), hb(group))
    h_spec = lambda group: pl.BlockSpec((B_CONV - 1, HEAD_DIM, nseq), hb(group))
    w_spec = lambda group: pl.BlockSpec((B_CONV, HEAD_DIM, 1), hb(group))
    s0_spec, prev_spec, state_shape, state_spec = _steps_state_specs(prev, nseq, l)
    in_specs = [x_spec(0), x_spec(1), x_spec(2), x_spec(3), h_spec(0), h_spec(1), h_spec(2),
                w_spec(0), w_spec(1), w_spec(2), _whole(pgt),
                pl.BlockSpec((None, 1, 2), lambda h: (h, 0, 0)), _whole(og), s0_spec]
    operands = [xt, xt, xt, xt, ht, ht, ht, wt, wt, wt, pgt, hp, og, s0_t]
    if prev is not None:
        in_specs.append(prev_spec)
        operands.append(prev)
    out_t, states = pl.pallas_call(
        functools.partial(_gdn_steps_kernel, steps=steps, has_prev=prev is not None),
        grid=(HEADS,),
        in_specs=in_specs,
        out_specs=[pl.BlockSpec((steps, HEAD_DIM, nseq), lambda h: (0, h, 0)), state_spec],
        out_shape=[jax.ShapeDtypeStruct((steps, WIDTH, nseq), F32), jax.ShapeDtypeStruct(state_shape, F32)],
        scratch_shapes=[pltpu.VMEM((HEAD_DIM, HEAD_DIM, nseq), F32)],
        compiler_params=pltpu.CompilerParams(dimension_semantics=("parallel",)),
        name="gdn_steps",
    )(*operands)
    return jnp.transpose(out_t, (2, 0, 1)).reshape(nseq * steps, WIDTH), states


def _rwkv_steps_kernel(*refs, steps, has_prev):
    (xr_ref, xk_ref, xv_ref, xl_ref, pr_ref, pk_ref, pv_ref, pl_ref, mr_ref, mk_ref, mv_ref, ml_ref,
     hp_ref, w2_ref, a2_ref, g2_ref, s0_ref) = refs[:17]
    prev_ref = refs[17] if has_prev else None
    o_ref, sout_ref, s_sc = refs[17 + int(has_prev):]

    def lerp(x_ref, p_ref, m_ref, t):
        x = x_ref[t]
        before = p_ref[...] if t == 0 else x_ref[t - 1]
        return x + (before - x) * m_ref[...]

    w0, a0, k_k, k_a, r_k, ln_g, ln_b = (hp_ref[i] for i in range(7))
    s_sc[...] = s0_ref[...]
    for t in range(steps):
        wl, al, gl = _lora_rows(lerp(xl_ref, pl_ref, ml_ref, t))
        w_log = -_softplus(-(w0 + _dot(w2_ref[...], jnp.tanh(wl).astype(BF16)))) - 0.5
        w = jnp.exp(-jnp.exp(w_log))
        a = jax.nn.sigmoid(a0 + _dot(a2_ref[...], al.astype(BF16)))
        gg = _dot(g2_ref[...], jax.nn.sigmoid(gl).astype(BF16))
        r = lerp(xr_ref, pr_ref, mr_ref, t)
        k = lerp(xk_ref, pk_ref, mk_ref, t)
        v = lerp(xv_ref, pv_ref, mv_ref, t)
        kk = k * k_k
        kk = kk * lax.rsqrt(jnp.sum(kk * kk, axis=0, keepdims=True) + L2_EPS)
        k2 = k * (1.0 + (a - 1.0) * k_a)
        b = kk * a
        sk = jnp.zeros_like(v)
        for j in range(HEAD_DIM):
            sk = sk + kk[j:j + 1, :] * s_sc[j]
        y = jnp.zeros_like(v)
        for j in range(HEAD_DIM):
            sj = w[j:j + 1, :] * s_sc[j] - b[j:j + 1, :] * sk + k2[j:j + 1, :] * v
            s_sc[j] = sj
            y = y + r[j:j + 1, :] * sj
        mean = jnp.mean(y, axis=0, keepdims=True)
        var = jnp.mean(jnp.square(y - mean), axis=0, keepdims=True)
        yn = (y - mean) * lax.rsqrt(var + GN_EPS) * ln_g + ln_b
        bonus = jnp.sum(r * k2 * r_k, axis=0, keepdims=True) * v
        o_ref[t] = (yn + bonus) * gg
    _store_stacked(sout_ref, prev_ref, s_sc)


def _rwkv_steps(pc, shift_state, s0_t, cvec, w2b, a2b, g2b, prev, l):
    nseq = shift_state.shape[0]
    steps = pc.shape[0] // nseq
    xt = jnp.transpose(pc.reshape(nseq, steps, C_PROJ), (1, 2, 0))
    pt = jnp.transpose(shift_state, (1, 0))
    mu = cvec[l, 0, 0:C_PROJ][:, None]
    hp = cvec[l, 0, C_PROJ:].reshape(7, WIDTH)[:, :, None]
    w2t = jnp.transpose(w2b[l], (1, 0))
    a2t = jnp.transpose(a2b[l], (1, 0))
    g2t = jnp.transpose(g2b[l], (1, 0))
    hb3 = lambda group: (lambda h: (0, group * HEADS + h, 0))
    hb2 = lambda group: (lambda h: (group * HEADS + h, 0))
    lora3 = lambda h: (0, 3, 0)
    lora2 = lambda h: (3, 0)
    s0_spec, prev_spec, state_shape, state_spec = _steps_state_specs(prev, nseq, l)
    in_specs = ([pl.BlockSpec((steps, HEAD_DIM, nseq), hb3(g)) for g in range(3)]
                + [pl.BlockSpec((steps, WIDTH, nseq), lora3)]
                + [pl.BlockSpec((HEAD_DIM, nseq), hb2(g)) for g in range(3)] + [pl.BlockSpec((WIDTH, nseq), lora2)]
                + [pl.BlockSpec((HEAD_DIM, 1), hb2(g)) for g in range(3)] + [pl.BlockSpec((WIDTH, 1), lora2)]
                + [pl.BlockSpec((7, HEAD_DIM, 1), lambda h: (0, h, 0)),
                   pl.BlockSpec((HEAD_DIM, C_DECAY_LORA), lambda h: (h, 0)),
                   pl.BlockSpec((HEAD_DIM, C_RATE_LORA), lambda h: (h, 0)),
                   pl.BlockSpec((HEAD_DIM, C_GATE_LORA), lambda h: (h, 0)), s0_spec])
    operands = [xt] * 4 + [pt] * 4 + [mu] * 4 + [hp, w2t, a2t, g2t, s0_t]
    if prev is not None:
        in_specs.append(prev_spec)
        operands.append(prev)
    out_t, states = pl.pallas_call(
        functools.partial(_rwkv_steps_kernel, steps=steps, has_prev=prev is not None),
        grid=(HEADS,),
        in_specs=in_specs,
        out_specs=[pl.BlockSpec((steps, HEAD_DIM, nseq), lambda h: (0, h, 0)), state_spec],
        out_shape=[jax.ShapeDtypeStruct((steps, WIDTH, nseq), F32), jax.ShapeDtypeStruct(state_shape, F32)],
        scratch_shapes=[pltpu.VMEM((HEAD_DIM, HEAD_DIM, nseq), F32)],
        compiler_params=pltpu.CompilerParams(dimension_semantics=("parallel",)),
        name="rwkv_steps",
    )(*operands)
    return jnp.transpose(out_t, (2, 0, 1)).reshape(nseq * steps, WIDTH), states


def _round_up(x, m):
    return (x + m - 1) // m * m


def _pool_offsets(stride):
    offs = []
    a = 0
    for k in range(len(POOL_WINDOWS)):
        a = _round_up(a + (1 << k) * stride, SUBLANES)
        offs.append(a)
    return offs


def _pool_kernel(hist_ref, dp_ref, wbd_ref, scale_ref, out_ref, s1, s2, *, rows, stride, start):
    offs = _pool_offsets(stride)
    d0 = offs[-1]
    n = d0 + rows
    dp = dp_ref[...]
    s1[0:d0, :] = hist_ref[0]
    s1[d0:n, :] = dp
    lane = lax.broadcasted_iota(jnp.int32, (1, WIDTH), 1)
    src, dst = s1, s2
    for k, a in enumerate(offs):
        sh = (1 << k) * stride
        shifted = jnp.where(lane >= k * POOL_GROUP, src[a - sh:n - sh, :], 0.0)
        if k < len(offs) - 1:
            dst[a:n, :] = src[a:n, :] + shifted
            src, dst = dst, src
        else:
            sums = src[a:n, :] + shifted
    assert stride & (stride - 1) == 0
    pos = start + (lax.broadcasted_iota(jnp.int32, (rows, WIDTH), 0) >> (stride.bit_length() - 1))
    group = lax.broadcasted_iota(jnp.int32, (rows, WIDTH), 1) >> POOL_GROUP_SHIFT
    window = jnp.left_shift(POOL_WINDOWS[0], group)
    cnt = jnp.minimum(pos + 1, window).astype(F32)
    diff = sums / cnt - dp
    out_ref[...] = _dot(diff.astype(BF16), wbd_ref[...]) * scale_ref[...]


def _pool(hist, dp, wbd, scale, l, *, nseq, rows, stride, start):
    d0 = _pool_offsets(stride)[-1]
    return pl.pallas_call(
        functools.partial(_pool_kernel, rows=rows, stride=stride, start=start),
        grid=(nseq,),
        in_specs=[pl.BlockSpec((1, d0, WIDTH), lambda b: (b, 0, 0)),
                  pl.BlockSpec((rows, WIDTH), lambda b: (b, 0)),
                  _layer_block(wbd, l), _layer_block(scale, l)],
        out_specs=pl.BlockSpec((rows, WIDTH), lambda b: (b, 0)),
        out_shape=jax.ShapeDtypeStruct((nseq * rows, WIDTH), F32),
        scratch_shapes=[pltpu.VMEM((d0 + rows, WIDTH), F32)] * 2,
        compiler_params=pltpu.CompilerParams(dimension_semantics=("parallel",)),
        name="pool",
    )(hist, dp, wbd, scale)


def _ffn_kernel(x_ref, ma_ref, mb_ref, mc_ref, md_ref, wo_ref, g2_ref, wu_ref, wd_ref, gf_ref,
                o_ref, *, tf, final):
    mixed = None
    for i, m_ref in enumerate((ma_ref, mb_ref, mc_ref, md_ref)):
        part = _dot(m_ref[...].astype(BF16), wo_ref[i * WIDTH:(i + 1) * WIDTH, :])
        mixed = part if mixed is None else mixed + part
    x = x_ref[...] + mixed
    hm = _rms(x, g2_ref[...]).astype(BF16)
    down = None
    for j in range(D_FF // tf):
        up = jnp.maximum(_dot(hm, wu_ref[:, j * tf:(j + 1) * tf]), 0.0)
        part = _dot((up * up).astype(BF16), wd_ref[j * tf:(j + 1) * tf, :])
        down = part if down is None else down + part
    x = x + down
    if final:
        x = _rms(x, gf_ref[...])
    o_ref[...] = x


def _ffn(x, mixed, wo, g2, wu, wd, gf, l):
    t = x.shape[0]
    tm = min(ROW_TILE, t)
    row = lambda i: (i, 0)
    return pl.pallas_call(
        functools.partial(_ffn_kernel, tf=FF_TILE, final=l == DEPTH - 1),
        grid=(t // tm,),
        in_specs=[pl.BlockSpec((tm, D_MODEL), row)] + [pl.BlockSpec((tm, WIDTH), row)] * 4
        + [_layer_block(wo, l, single_buffer=True), _layer_block(g2, l),
           _layer_block(wu, l, single_buffer=True), _layer_block(wd, l, single_buffer=True), _whole(gf)],
        out_specs=pl.BlockSpec((tm, D_MODEL), row),
        out_shape=jax.ShapeDtypeStruct((t, D_MODEL), F32),
        compiler_params=pltpu.CompilerParams(dimension_semantics=("parallel",),
                                             vmem_limit_bytes=VMEM_LIMIT),
        name="ffn",
    )(x, *mixed, wo, g2, wu, wd, gf)


def _rows(v):
    return v[:, None, :]


def kernel(x_prompt, x_sample, state_b_conv, state_b_ssm, state_c_shift, state_c_wkv, state_d_pool, norm1_g, w_in, a_ws, a_bs, a_vnorm_g, b_conv_w, b_a_log, b_dt_bias, b_onorm_g, c_mu, c_w0, c_w2, c_a0, c_a2, c_g2, c_k_k, c_k_a, c_r_k, c_ln_g, c_ln_b, d_w, d_scale, w_out, norm2_g, w_up, w_down, final_g):
    nb, seq, _ = x_prompt.shape
    ns, dseq, _ = x_sample.shape
    a_chunk = a_ws.shape[-1]

    a_end = 2 * WIDTH
    b_end = a_end + 4 * WIDTH
    c_off = b_end + 2 * HEADS
    d_off = c_off + C_PROJ
    w_in_b = w_in.astype(BF16)
    w_cols = [w_in_b[:, :, :a_end], w_in_b[:, :, a_end:b_end], w_in_b[:, :, c_off:d_off], w_in_b[:, :, d_off:],
              jnp.pad(w_in_b[:, :, b_end:c_off], ((0, 0), (0, 0), (0, GATE_LANES - 2 * HEADS)))]
    wo = w_out.astype(BF16)
    wu = w_up.astype(BF16)
    wd = w_down.astype(BF16)
    g1 = _rows(norm1_g)
    g2 = _rows(norm2_g)
    gf = final_g[None, :]
    vg = _rows(a_vnorm_g)
    causal = jnp.tril(jnp.ones((a_chunk, a_chunk), dtype=bool))
    wm = jnp.where(causal, a_ws, 0.0)
    bias_t = jnp.repeat(jnp.transpose(a_bs, (0, 2, 1)), HEAD_DIM, axis=2)
    mm_p = jnp.transpose(wm, (0, 2, 1, 3)).reshape(DEPTH, a_chunk, HEADS * a_chunk).astype(BF16)
    srow = jnp.arange(ns * dseq)
    step_onehot = (srow[:, None] % dseq == jnp.arange(dseq)[None, :]).astype(F32)
    same_seq = (srow[:, None] // dseq) == (srow[None, :] // dseq)
    mm_s = jnp.einsum('it,lhts,js->lihj', step_onehot, wm[:, :, :dseq, :dseq], step_onehot,
                      precision=lax.Precision.HIGHEST)
    mm_s = jnp.where(same_seq[None, :, None, :], mm_s, 0.0).reshape(
        DEPTH, ns * dseq, HEADS * ns * dseq).astype(BF16)
    bias_s = jnp.tile(bias_t[:, :dseq], (1, ns, 1))
    lead = jnp.zeros((DEPTH, HEADS), F32)
    tail = jnp.zeros((DEPTH, GATE_LANES - 2 * HEADS), F32)
    gvec = jnp.concatenate([lead, b_a_log, tail, lead, b_dt_bias, tail, jnp.tile(b_onorm_g, (1, HEADS))],
                           axis=1)[:, None, :]
    cvec = jnp.concatenate([c_mu, c_w0, c_a0, c_k_k, c_k_a, c_r_k.reshape(DEPTH, WIDTH), c_ln_g, c_ln_b],
                           axis=1)[:, None, :]
    c_w2b, c_a2b, c_g2b = c_w2.astype(BF16), c_a2.astype(BF16), c_g2.astype(BF16)
    n_groups = len(POOL_WINDOWS)
    wbd = jnp.einsum('lgcd,gh->lgchd', d_w, jnp.eye(n_groups, dtype=F32)).reshape(
        DEPTH, WIDTH, WIDTH).astype(BF16)
    dscale = _rows(d_scale)
    lane_head = jnp.arange(WIDTH) // HEAD_DIM
    gate_lane = jnp.arange(GATE_LANES)
    bd = (lane_head[:, None] == lane_head[None, :]).astype(BF16)
    eb = jnp.tile((gate_lane[:, None] == lane_head[None, :]).astype(BF16), (2, 1))
    eg = jnp.tile((gate_lane[:, None] == lane_head[None, :] + HEADS).astype(BF16), (3, 1))

    def run_group(x, l, grp, prev_b, prev_c):
        nseq, length = grp["nseq"], grp["length"]
        conv_state = grp["b_conv"](l)
        if grp["prep"]:
            a_out, a_v, pb, pc, pd, pg, tails, tails_c = _inproj(
                x, g1, w_cols, grp["mm"], grp["bias"], vg, l,
                token_prep=(b_conv_w, gvec, bd, conv_state, length, cvec, c_w2b, c_a2b, c_g2b,
                            grp["c_shift"](l)[:, None, :]))
            b_rows = tails.reshape(nseq, -1, SUBLANES, B_QKV)[:, -1]
            c_last = tails_c.reshape(nseq, -1, SUBLANES, C_PROJ)[:, -1, SUBLANES - 1]
        else:
            a_out, a_v, pb, pc, pd, pg = _inproj(x, g1, w_cols, grp["mm"], grp["bias"], vg, l)
            b_rows = pb.reshape(nseq, length, 4 * WIDTH)[:, :, :B_QKV]
            c_last = pc.reshape(nseq, length, C_PROJ)[:, length - 1]
        b_tail = jnp.concatenate([conv_state, b_rows], axis=1)[:, -(B_CONV - 1):]
        pb3 = pb.reshape(nseq, length, 4 * WIDTH)
        pc3 = pc.reshape(nseq, length, pc.shape[1])
        if grp["prep"]:
            zero_state = jnp.zeros((nseq, HEADS, HEAD_DIM, HEAD_DIM), F32)
            b_out, b_state = _recurrent_call(
                _gdn_kernel, "gdn", [pb3, pg.reshape(nseq, length, GATE_LANES)], zero_state,
                [(gvec, l), (bd, None), (eb, None), (eg, None)], prev_b,
                nbs=PROMPT_SEQS_PER_STEP, chunk=MIX_CHUNK)
            c_out, c_state = _recurrent_call(
                _rwkv_kernel, "rwkv", [pc3], zero_state, [(cvec, l), (bd, None)], prev_c,
                nbs=PROMPT_SEQS_PER_STEP, chunk=MIX_CHUNK)
            b_out = b_out.reshape(nseq * length, WIDTH)
            c_out = c_out.reshape(nseq * length, WIDTH)
        else:
            b_out, b_state = _gdn_steps(pb, pg, conv_state, grp["b_ssm_t"], b_conv_w, b_a_log, b_dt_bias,
                                        b_onorm_g, prev_b, l)
            c_out, c_state = _rwkv_steps(pc, grp["c_shift"](l), grp["c_wkv_t"], cvec, c_w2b, c_a2b, c_g2b,
                                         prev_c, l)
        d_out = grp["pool"](pd, l)
        x = _ffn(x, (a_out, b_out, c_out, d_out), wo, g2, wu, wd, gf, l)
        return x, a_v, b_tail, c_last, pd.reshape(nseq, length, WIDTH), b_state, c_state

    d0 = _pool_offsets(1)[-1]
    zero_hist = jnp.zeros((nb, d0, WIDTH), F32)
    prompt = dict(
        nseq=nb, length=seq, mm=mm_p, bias=bias_t, prep=True,
        b_conv=lambda l: jnp.zeros((nb, B_CONV - 1, B_QKV), F32), c_shift=lambda l: jnp.zeros((nb, C_PROJ), F32),
        pool=lambda pd, l: _pool(zero_hist, pd, wbd, dscale, l, nseq=nb, rows=seq, stride=1, start=0))

    def sample_pool(pd, l):
        pd_t = jnp.transpose(pd.reshape(ns, dseq, WIDTH), (1, 0, 2)).reshape(dseq * ns, WIDTH)
        hist = jnp.transpose(state_d_pool[l], (1, 0, 2)).reshape(1, POOL_BUF * ns, WIDTH)
        out_t = _pool(hist, pd_t, wbd, dscale, l, nseq=1, rows=dseq * ns, stride=ns, start=PAST_LEN)
        return jnp.transpose(out_t.reshape(dseq, ns, WIDTH), (1, 0, 2)).reshape(ns * dseq, WIDTH)

    sample = dict(
        nseq=ns, length=dseq, mm=mm_s, bias=bias_s, prep=False,
        b_ssm_t=jnp.transpose(state_b_ssm, (0, 2, 3, 4, 1)), c_wkv_t=jnp.transpose(state_c_wkv, (0, 2, 3, 4, 1)),
        b_conv=lambda l: state_b_conv[l], c_shift=lambda l: state_c_shift[l], pool=sample_pool)

    xp = x_prompt.reshape(nb * seq, D_MODEL)
    xs = x_sample.reshape(ns * dseq, D_MODEL)
    p_bc, p_cs, p_dp, s_av, s_bc, s_cs, s_dp = [], [], [], [], [], [], []
    p_bs = p_cw = s_bs = s_cw = None
    for l in range(DEPTH):
        xp, _, b_tail, c_last, pd3, p_bs, p_cw = run_group(xp, l, prompt, p_bs, p_cw)
        p_bs = p_bs if l else p_bs[None]
        p_cw = p_cw if l else p_cw[None]
        p_bc.append(b_tail)
        p_cs.append(c_last)
        p_dp.append(pd3[:, seq - POOL_BUF:])

        xs, a_v, b_tail, c_last, pd3, s_bs, s_cw = run_group(xs, l, sample, s_bs, s_cw)
        s_bs = s_bs if l else s_bs[None]
        s_cw = s_cw if l else s_cw[None]
        s_av.append(a_v.reshape(ns, dseq, WIDTH))
        s_bc.append(b_tail)
        s_cs.append(c_last)
        s_dp.append(jnp.concatenate([state_d_pool[l], pd3], axis=1)[:, -POOL_BUF:])

    return (xp.reshape(nb, seq, D_MODEL), xs.reshape(ns, dseq, D_MODEL),
            jnp.stack(p_bc), p_bs, jnp.stack(p_cs), p_cw, jnp.stack(p_dp),
            jnp.stack(s_av), jnp.stack(s_bc), jnp.transpose(s_bs, (0, 4, 1, 2, 3)), jnp.stack(s_cs),
            jnp.transpose(s_cw, (0, 4, 1, 2, 3)), jnp.stack(s_dp))
```

```python
import functools

import jax
import jax.numpy as jnp
from jax import lax
from jax.experimental import pallas as pl
from jax.experimental.pallas import tpu as pltpu

F32 = jnp.float32
BF16 = jnp.bfloat16

D_MODEL = 1024
DEPTH = 2
HEADS = 4
HEAD_DIM = 64
HEAD_SHIFT = HEAD_DIM.bit_length() - 1
WIDTH = HEADS * HEAD_DIM
B_QKV = 3 * WIDTH
B_CONV = 4
C_PROJ = 4 * WIDTH
C_DECAY_LORA = 64
C_RATE_LORA = 64
C_GATE_LORA = 128
POOL_WINDOWS = (2, 4, 8, 16)
POOL_BUF = max(POOL_WINDOWS) - 1
POOL_GROUP = WIDTH // len(POOL_WINDOWS)
POOL_GROUP_SHIFT = POOL_GROUP.bit_length() - 1
D_FF = 4 * D_MODEL
PAST_LEN = 16384
NORM_EPS = 1e-6
L2_EPS = 1e-6
GN_EPS = 64e-5
V7X_LANES = 128
V7X_SUBLANES = 8
V7X_VMEM_BYTES = 64 * 1024 * 1024
GATE_LANES = V7X_LANES
SUBLANES = V7X_SUBLANES
VMEM_LIMIT = V7X_VMEM_BYTES * 3 // 4
MIX_CHUNK = 64
PROMPT_SEQS_PER_STEP = 8
ROW_TILE = 512
FF_TILE = 1024


def _rms(x, g):
    return x * lax.rsqrt(jnp.mean(x * x, axis=-1, keepdims=True) + NORM_EPS) * g


def _softplus(x):
    return jnp.maximum(x, 0.0) + jnp.log1p(jnp.exp(-jnp.abs(x)))


def _dot(a, b):
    return jnp.dot(a, b, preferred_element_type=F32)


def _bmm(a, b):
    return jnp.einsum('bij,bjk->bik', a, b, preferred_element_type=F32)


def _bmm_nt(a, b):
    return jnp.einsum('bik,bjk->bij', a, b, preferred_element_type=F32)


def _bmm_tn(a, b):
    return jnp.einsum('bki,bkj->bij', a, b, preferred_element_type=F32)


def _layer_block(arr, l, single_buffer=False):
    shape = arr.shape[1:]
    index = lambda *_: (l,) + (0,) * len(shape)
    if single_buffer:
        return pl.BlockSpec((None,) + shape, index, pipeline_mode=pl.Buffered(1))
    return pl.BlockSpec((None,) + shape, index)


def _whole(arr):
    return pl.BlockSpec(arr.shape, lambda *_: (0,) * arr.ndim)


def _split_heads(x, lane0):
    return jnp.stack([x[b, :, lane0 + h * HEAD_DIM:lane0 + (h + 1) * HEAD_DIM]
                      for b in range(x.shape[0]) for h in range(HEADS)])


def _merge_heads(x, nbs):
    return jnp.concatenate([jnp.concatenate([x[b * HEADS + h] for h in range(HEADS)], axis=1)
                            for b in range(nbs)], axis=0)


def _group_sum(x, bd):
    hi = x.astype(BF16)
    lo = (x - hi.astype(F32)).astype(BF16)
    return _dot(hi, bd) + _dot(lo, bd)


def _split3(x):
    p1 = x.astype(BF16)
    r1 = x - p1.astype(F32)
    p2 = r1.astype(BF16)
    return p1, p2, (r1 - p2.astype(F32)).astype(BF16)


def _cumsum_rows(tril_b, x):
    w = x.shape[1]
    y = _dot(tril_b, jnp.concatenate(_split3(x), axis=1))
    return y[:, :w] + (y[:, w:2 * w] + y[:, 2 * w:])


def _unit_lower_inverse(lm, row, col, n_valid):
    def sub_diag_block(shift):
        return (((row >> (shift + 1)) == (col >> (shift + 1)))
                & (((row >> shift) & 1) == 1) & (((col >> shift) & 1) == 0))

    m = (row == col).astype(F32) - jnp.where(sub_diag_block(0), lm, 0.0)
    shift = 1
    while (1 << shift) < n_valid:
        cs = jnp.where(sub_diag_block(shift), lm, 0.0).astype(BF16)
        mb = m.astype(BF16)
        m = m - _bmm(_bmm(mb, cs).astype(BF16), mb)
        shift += 1
    return m


def _lora_inputs(x, lane0):
    a = lane0 + C_DECAY_LORA
    b = a + C_RATE_LORA
    return x[:, lane0:a], x[:, a:b], x[:, b:b + C_GATE_LORA]


def _lora_rows(x):
    a = C_DECAY_LORA
    b = a + C_RATE_LORA
    return x[0:a], x[a:b], x[b:b + C_GATE_LORA]


def _transpose_rows(a):
    n = a.shape[0]
    if n < GATE_LANES:
        a = jnp.concatenate([a, jnp.zeros((GATE_LANES - n, a.shape[1]), a.dtype)], axis=0)
    return a.T[:, :n]


def _conv_silu(raw, hist, tail_ref, lo, cw):
    tm = raw.shape[0]
    cols = slice(lo, lo + WIDTH)
    xfull = jnp.concatenate([hist[:, cols], raw], axis=0)
    conv = pltpu.roll(xfull, 3, axis=0)[SUBLANES:, :] * cw[0:1, cols]
    conv = conv + pltpu.roll(xfull, 2, axis=0)[SUBLANES:, :] * cw[1:2, cols]
    conv = conv + pltpu.roll(xfull, 1, axis=0)[SUBLANES:, :] * cw[2:3, cols]
    conv = conv + raw * cw[3:4, cols]
    tail = raw[tm - SUBLANES:tm, :]
    hist[:, cols] = tail
    tail_ref[:, cols] = tail
    return jax.nn.silu(conv)


def _inproj_kernel(*refs, r, prep, tiles_per_seq):
    (x_ref, g_ref, wa_ref, wb_ref, wc_ref, wd_ref, wg_ref, mm_ref, bias_ref, vg_ref) = refs[:10]
    if prep:
        cw_ref, gvec_ref, bd_ref, cbuf_ref, cvec_ref, w2_ref, a2_ref, g2_ref, sh_ref = refs[10:19]
        a_ref, v_ref, pb_ref, pc_ref, pd_ref, pg_ref, tail_ref, tailc_ref, hist, hist_c = refs[19:]
    else:
        a_ref, v_ref, pb_ref, pc_ref, pd_ref, pg_ref = refs[10:]
    h = _rms(x_ref[...], g_ref[...]).astype(BF16)

    def gmlp_gates(pa):
        v = _rms(jax.nn.gelu(pa[:, WIDTH:]), vg_ref[...])
        v_ref[...] = v
        return jax.nn.gelu(pa[:, :WIDTH]), v

    def gmlp_mix(u, v):
        lane_head = lax.broadcasted_iota(jnp.int32, (r, WIDTH), 1) >> HEAD_SHIFT
        mm = mm_ref[...]
        bias = bias_ref[...]
        for i in range(u.shape[0] // r):
            vc = v[i * r:(i + 1) * r]
            per_head = jnp.concatenate([jnp.where(lane_head == h, vc, 0.0).astype(BF16)
                                        for h in range(HEADS)], axis=0)
            a_ref[i * r:(i + 1) * r, :] = u[i * r:(i + 1) * r] * (_dot(mm, per_head) + bias)

    if prep:
        n_hist = B_CONV - 1

        @pl.when(pl.program_id(0) % tiles_per_seq == 0)
        def _():
            hist[...] = jnp.zeros(hist.shape, F32)
            hist[SUBLANES - n_hist:SUBLANES, :] = cbuf_ref[0]
            hist_c[...] = jnp.zeros(hist_c.shape, F32)
            hist_c[SUBLANES - 1:SUBLANES, :] = sh_ref[0]

        cvec = cvec_ref[...]
        mu = cvec[:, 0:C_PROJ]
        w0, a0, k_k, k_a, r_k = (cvec[:, C_PROJ + i * WIDTH:C_PROJ + (i + 1) * WIDTH] for i in range(5))

        def shifted_lerp(cp, lo):
            cols = slice(lo, lo + WIDTH)
            prev = pltpu.roll(jnp.concatenate([hist_c[:, cols], cp], axis=0), 1, axis=0)[SUBLANES:, :]
            tail = cp[cp.shape[0] - SUBLANES:, :]
            hist_c[:, cols] = tail
            tailc_ref[:, cols] = tail
            return cp + (prev - cp) * mu[:, cols]

        cw = cw_ref[...]
        bd = bd_ref[...]
        gvec = gvec_ref[...]
        raw_q = _dot(h, wb_ref[:, 0:WIDTH])
        raw_k = _dot(h, wb_ref[:, WIDTH:2 * WIDTH])
        raw_v = _dot(h, wb_ref[:, 2 * WIDTH:3 * WIDTH])
        qf = _conv_silu(raw_q, hist, tail_ref, 0, cw)
        pb_ref[:, 0:WIDTH] = qf * lax.rsqrt(_dot((qf * qf).astype(BF16), bd) + L2_EPS) * (HEAD_DIM ** -0.5)
        raw_gate = _dot(h, wb_ref[:, 3 * WIDTH:4 * WIDTH])
        pg = _dot(h, wg_ref[...])
        kf = _conv_silu(raw_k, hist, tail_ref, WIDTH, cw)
        pb_ref[:, WIDTH:2 * WIDTH] = kf * lax.rsqrt(_dot((kf * kf).astype(BF16), bd) + L2_EPS)
        cp_l = _dot(h, wc_ref[:, 3 * WIDTH:4 * WIDTH])
        cp_k = _dot(h, wc_ref[:, WIDTH:2 * WIDTH])
        pb_ref[:, 2 * WIDTH:3 * WIDTH] = _conv_silu(raw_v, hist, tail_ref, 2 * WIDTH, cw)
        pa = _dot(h, wa_ref[...])
        pb_ref[:, 3 * WIDTH:4 * WIDTH] = jax.nn.silu(raw_gate)
        alog = gvec[:, 0:GATE_LANES]
        dtb = gvec[:, GATE_LANES:2 * GATE_LANES]
        lane = lax.broadcasted_iota(jnp.int32, pg.shape, 1)
        pg_ref[...] = jnp.where(lane < HEADS, jax.nn.sigmoid(pg), -jnp.exp(alog) * _softplus(pg + dtb))
        cp_r = _dot(h, wc_ref[:, 0:WIDTH])
        cp_v = _dot(h, wc_ref[:, 2 * WIDTH:3 * WIDTH])
        xm_l = shifted_lerp(cp_l, 3 * WIDTH)
        wl, al, gl = _lora_inputs(xm_l, 0)
        w_log = -_softplus(-(w0 + _dot(jnp.tanh(wl).astype(BF16), w2_ref[...]))) - 0.5
        pc_ref[:, 5 * WIDTH:6 * WIDTH] = -jnp.exp(w_log)
        a = jax.nn.sigmoid(a0 + _dot(al.astype(BF16), a2_ref[...]))
        pc_ref[:, 6 * WIDTH:7 * WIDTH] = _dot(jax.nn.sigmoid(gl).astype(BF16), g2_ref[...])
        pd_ref[...] = _dot(h, wd_ref[...])
        k = shifted_lerp(cp_k, WIDTH)
        kkraw = k * k_k
        kkn = kkraw * lax.rsqrt(_dot((kkraw * kkraw).astype(BF16), bd) + L2_EPS)
        k2 = k * (1.0 + (a - 1.0) * k_a)
        pc_ref[:, WIDTH:2 * WIDTH] = k2
        pc_ref[:, 3 * WIDTH:4 * WIDTH] = kkn
        pc_ref[:, 4 * WIDTH:5 * WIDTH] = kkn * a
        rr = shifted_lerp(cp_r, 0)
        vv = shifted_lerp(cp_v, 2 * WIDTH)
        pc_ref[:, 0:WIDTH] = rr
        pc_ref[:, 2 * WIDTH:3 * WIDTH] = vv
        pc_ref[:, 7 * WIDTH:8 * WIDTH] = _group_sum(rr * k2 * r_k, bd) * vv
        gmlp_mix(*gmlp_gates(pa))
    else:
        pa = _dot(h, wa_ref[...])
        pb_ref[...] = _dot(h, wb_ref[...])
        u, v = gmlp_gates(pa)
        pc_ref[...] = _dot(h, wc_ref[...])
        pd_ref[...] = _dot(h, wd_ref[...])
        pg_ref[...] = _dot(h, wg_ref[...])
        gmlp_mix(u, v)


def _inproj(x, g, ws, mm, bias, vg, l, token_prep=None):
    t = x.shape[0]
    r = mm.shape[1]
    tm = max(r, min(ROW_TILE, t))
    widths = [WIDTH, WIDTH] + [w.shape[2] for w in ws[1:]]
    in_specs = ([pl.BlockSpec((tm, D_MODEL), lambda i: (i, 0)), _layer_block(g, l)]
                + [_layer_block(w, l, single_buffer=True) for w in ws]
                + [_layer_block(mm, l), _layer_block(bias, l), _layer_block(vg, l)])
    operands = [x, g, *ws, mm, bias, vg]
    out_specs = [pl.BlockSpec((tm, n), lambda i: (i, 0)) for n in widths]
    out_shape = [jax.ShapeDtypeStruct((t, n), F32) for n in widths]
    scratch = []
    tiles_per_seq = 1
    if token_prep is not None:
        conv_w, gvec, bd, conv_state, seq_len, cvec, w2, a2, g2, shift_state = token_prep
        tiles_per_seq = seq_len // tm
        per_seq = lambda i: (i // tiles_per_seq, 0, 0)
        in_specs += [_layer_block(conv_w, l), _layer_block(gvec, l), _whole(bd),
                     pl.BlockSpec((1,) + conv_state.shape[1:], per_seq),
                     _layer_block(cvec, l), _layer_block(w2, l), _layer_block(a2, l), _layer_block(g2, l),
                     pl.BlockSpec((1,) + shift_state.shape[1:], per_seq)]
        operands += [conv_w, gvec, bd, conv_state, cvec, w2, a2, g2, shift_state]
        widths[3] = 8 * WIDTH
        out_specs[3] = pl.BlockSpec((tm, 8 * WIDTH), lambda i: (i, 0))
        out_shape[3] = jax.ShapeDtypeStruct((t, 8 * WIDTH), F32)
        for w in (B_QKV, C_PROJ):
            out_specs.append(pl.BlockSpec((SUBLANES, w), lambda i: (i, 0)))
            out_shape.append(jax.ShapeDtypeStruct((t // tm * SUBLANES, w), F32))
            scratch.append(pltpu.VMEM((SUBLANES, w), F32))
    return pl.pallas_call(
        functools.partial(_inproj_kernel, r=r, prep=token_prep is not None, tiles_per_seq=tiles_per_seq),
        grid=(t // tm,),
        in_specs=in_specs,
        out_specs=out_specs,
        out_shape=out_shape,
        scratch_shapes=scratch,
        compiler_params=pltpu.CompilerParams(
            dimension_semantics=("arbitrary" if token_prep is not None else "parallel",),
            vmem_limit_bytes=VMEM_LIMIT),
        name="inproj",
    )(*operands)


def _store_state(sout_ref, prev_ref, new_state):
    if prev_ref is None:
        sout_ref[...] = new_state
    else:
        n_prev = prev_ref.shape[0]
        sout_ref[0:n_prev] = prev_ref[...]
        sout_ref[n_prev] = new_state


def _recurrent_call(kernel_fn, name, x_blocks, s0, consts, prev, *, nbs, chunk):
    nseq, length, _ = x_blocks[0].shape
    blk = lambda b, c: (b, c, 0)
    state_block = (nbs, HEADS, HEAD_DIM, HEAD_DIM)
    in_specs = [pl.BlockSpec((nbs, chunk, x.shape[2]), blk) for x in x_blocks]
    in_specs.append(pl.BlockSpec(state_block, lambda b, c: (b, 0, 0, 0)))
    in_specs += [_whole(a) if l is None else _layer_block(a, l) for a, l in consts]
    operands = list(x_blocks) + [s0] + [a for a, _ in consts]
    if prev is None:
        state_shape = (nseq, HEADS, HEAD_DIM, HEAD_DIM)
        state_spec = pl.BlockSpec(state_block, lambda b, c: (b, 0, 0, 0))
    else:
        n_prev = prev.shape[0]
        in_specs.append(pl.BlockSpec((n_prev,) + state_block, lambda b, c: (0, b, 0, 0, 0)))
        operands.append(prev)
        state_shape = (n_prev + 1, nseq, HEADS, HEAD_DIM, HEAD_DIM)
        state_spec = pl.BlockSpec((n_prev + 1,) + state_block, lambda b, c: (0, b, 0, 0, 0))
    return pl.pallas_call(
        functools.partial(kernel_fn, nbs=nbs, chunk=chunk, n_x=len(x_blocks), n_consts=len(consts),
                          has_prev=prev is not None),
        grid=(nseq // nbs, length // chunk),
        in_specs=in_specs,
        out_specs=[pl.BlockSpec((nbs, chunk, WIDTH), blk), state_spec],
        out_shape=[jax.ShapeDtypeStruct((nseq, length, WIDTH), F32),
                   jax.ShapeDtypeStruct(state_shape, F32)],
        scratch_shapes=[pltpu.VMEM((nbs * HEADS, HEAD_DIM, HEAD_DIM), F32)],
        compiler_params=pltpu.CompilerParams(dimension_semantics=("parallel", "arbitrary")),
        name=name,
    )(*operands)


def _unpack_refs(refs, n_x, n_consts, has_prev):
    x_refs = refs[:n_x]
    s0_ref = refs[n_x]
    consts = refs[n_x + 1:n_x + 1 + n_consts]
    pos = n_x + 1 + n_consts
    prev_ref = refs[pos] if has_prev else None
    o_ref, sout_ref, state_sc = refs[pos + int(has_prev):]
    return x_refs, s0_ref, consts, prev_ref, o_ref, sout_ref, state_sc


def _gdn_kernel(*refs, nbs, chunk, n_x, n_consts, has_prev):
    ((pb_ref, pg_ref), s0_ref, (vec_ref, bd_ref, eb_ref, eg_ref), prev_ref,
     o_ref, sout_ref, s_sc) = _unpack_refs(refs, n_x, n_consts, has_prev)
    c = pl.program_id(1)
    nb = nbs * HEADS
    rows = nbs * chunk

    @pl.when(c == 0)
    def _():
        s_sc[...] = s0_ref[...].reshape(nb, HEAD_DIM, HEAD_DIM)

    pb = pb_ref[...]
    beta_all = g_all = pg_ref[...]
    og = vec_ref[...][:, 2 * GATE_LANES:2 * GATE_LANES + WIDTH]
    bd = bd_ref[...]
    qkv = pb[:, :, 0:B_QKV].reshape(rows, B_QKV)
    gate = pb[:, :, B_QKV:B_QKV + WIDTH].reshape(rows, WIDTH)
    qn = qkv[:, 0:WIDTH]
    kn = qkv[:, WIDTH:2 * WIDTH]
    vf = qkv[:, 2 * WIDTH:]

    row = lax.broadcasted_iota(jnp.int32, (chunk, chunk), 0)
    col = lax.broadcasted_iota(jnp.int32, (chunk, chunk), 1)
    tril = row >= col
    strict = row > col
    trilb = tril.astype(BF16)
    b1, b2, _ = _split3(beta_all.reshape(rows, GATE_LANES))
    beta_f = _dot(jnp.concatenate([b1, b2], axis=1), eb_ref[...])
    gc_small = [_cumsum_rows(trilb, g_all[b]) for b in range(nbs)]
    gc = _dot(jnp.concatenate(_split3(jnp.concatenate(gc_small, axis=0)), axis=1),
              eg_ref[...]).reshape(nbs, chunk, WIDTH)
    glast = gc[:, chunk - 1:chunk, :]
    e_g = jnp.exp(gc).reshape(rows, WIDTH)
    e_gl = jnp.exp(glast - gc).reshape(rows, WIDTH)
    bk = beta_f * kn

    def heads(x):
        return _split_heads(x.reshape(nbs, chunk, WIDTH).astype(BF16), 0)

    qkk = _bmm_nt(jnp.concatenate([heads(qn), heads(bk)], axis=1), heads(kn))
    grow_all = [_transpose_rows(g) for g in gc_small]
    gcol = jnp.stack([gc[b, :, h * HEAD_DIM:h * HEAD_DIM + 1] for b in range(nbs) for h in range(HEADS)])
    grow = jnp.stack([grow_all[b][HEADS + h:HEADS + h + 1, :] for b in range(nbs) for h in range(HEADS)])
    decay = jnp.where(tril, jnp.exp(jnp.minimum(gcol - grow, 0.0)), 0.0)
    qk = qkk[:, :chunk] * decay
    lm = jnp.where(strict, qkk[:, chunk:] * decay, 0.0)
    tinv = _unit_lower_inverse(lm, row, col, chunk)
    rhs = jnp.concatenate([heads(beta_f * vf), heads(bk * e_g)], axis=2)
    uw = _bmm(tinv.astype(BF16), rhs)
    u = uw[:, :, :HEAD_DIM]
    wk = uw[:, :, HEAD_DIM:]
    s = s_sc[...]
    ws = _bmm(jnp.concatenate([wk.astype(BF16), heads(qn * e_g)], axis=1),
              s.astype(BF16))
    wnb = (u - ws[:, :chunk]).astype(BF16)
    o = ws[:, chunk:] + _bmm(qk.astype(BF16), wnb)
    s_sc[...] = _split_heads(jnp.exp(glast), 0) * s + _bmm_tn(heads(kn * e_gl), wnb)
    of = _merge_heads(o, nbs)
    of = of * lax.rsqrt(_group_sum(of * of, bd) * (1.0 / HEAD_DIM) + NORM_EPS) * og * gate
    o_ref[...] = of.reshape(nbs, chunk, WIDTH)

    @pl.when(c == pl.num_programs(1) - 1)
    def _():
        _store_state(sout_ref, prev_ref, s_sc[...].reshape(nbs, HEADS, HEAD_DIM, HEAD_DIM))


def _rwkv_kernel(*refs, nbs, chunk, n_x, n_consts, has_prev):
    ((pc_ref,), s0_ref, (vec_ref, bd_ref), prev_ref, o_ref, sout_ref, st_sc) = _unpack_refs(
        refs, n_x, n_consts, has_prev)
    c = pl.program_id(1)
    rows = nbs * chunk

    @pl.when(c == 0)
    def _():
        for b in range(nbs):
            for h in range(HEADS):
                st_sc[b * HEADS + h] = s0_ref[b, h].T

    vec = vec_ref[...]
    ln_g, ln_b = (vec[:, C_PROJ + i * WIDTH:C_PROJ + (i + 1) * WIDTH] for i in (5, 6))
    bd = bd_ref[...]
    xp = pc_ref[...].reshape(rows, 8 * WIDTH)
    r, k2, v, kkn, bf, logw, gg, bonus = (xp[:, i * WIDTH:(i + 1) * WIDTH] for i in range(8))

    row = lax.broadcasted_iota(jnp.int32, (chunk, chunk), 0)
    col = lax.broadcasted_iota(jnp.int32, (chunk, chunk), 1)
    tril = row >= col
    strict = row > col
    trilb = tril.astype(BF16)

    def heads(x):
        return _split_heads(x.reshape(nbs, chunk, WIDTH).astype(BF16), 0)

    logw3 = logw.reshape(nbs, chunk, WIDTH)
    glog3 = jnp.stack([_cumsum_rows(trilb, logw3[b]) for b in range(nbs)])
    glast = glog3[:, chunk - 1:chunk, :]
    glog = glog3.reshape(rows, WIDTH)
    e_n = jnp.exp(-glog)
    e_l = jnp.exp(glast - glog3).reshape(rows, WIDTH)
    lhs = jnp.concatenate([heads(kkn * jnp.exp(glog - logw)), heads(r * jnp.exp(glog))], axis=1)
    rhs = jnp.concatenate([heads(bf * e_n), heads(k2 * e_n)], axis=1)
    pair = _bmm_nt(lhs, rhs)
    st = st_sc[...]
    ls = _bmm_nt(lhs, st.astype(BF16))
    lb = jnp.where(strict, pair[:, :chunk, :chunk], 0.0)
    lk = jnp.where(strict, pair[:, :chunk, chunk:], 0.0)
    arb = jnp.where(tril, pair[:, chunk:, :chunk], 0.0)
    ark = jnp.where(tril, pair[:, chunk:, chunk:], 0.0)
    tinv = _unit_lower_inverse(lb, row, col, chunk)
    vhb = heads(v)
    ub = _bmm(tinv.astype(BF16), (ls[:, :chunk] + _bmm(lk.astype(BF16), vhb)).astype(BF16)).astype(BF16)
    y = ls[:, chunk:] + _bmm(jnp.concatenate([ark, -arb], axis=2).astype(BF16),
                             jnp.concatenate([vhb, ub], axis=1))
    upd = _bmm_tn(jnp.concatenate([vhb, -ub], axis=1),
                  jnp.concatenate([heads(k2 * e_l), heads(bf * e_l)], axis=1))
    st_sc[...] = _split_heads(jnp.exp(glast), 0) * st + upd
    yf = _merge_heads(y, nbs)
    dev = yf - _group_sum(yf, bd) * (1.0 / HEAD_DIM)
    var = _group_sum(dev * dev, bd) * (1.0 / HEAD_DIM)
    yn = dev * lax.rsqrt(var + GN_EPS) * ln_g + ln_b
    out = (yn + bonus) * gg
    o_ref[...] = out.reshape(nbs, chunk, WIDTH)

    @pl.when(c == pl.num_programs(1) - 1)
    def _():
        _store_state(sout_ref, prev_ref,
                     jnp.stack([jnp.stack([st_sc[b * HEADS + h].T for h in range(HEADS)])
                                for b in range(nbs)]))


def _mixers_kernel(*refs, nbs, chunk, has_prev):
    pb_ref, pg_ref, pc_ref, s0_ref, gvec_ref, cvec_ref, bd_ref, eb_ref, eg_ref = refs[:9]
    prev = refs[9:11] if has_prev else ()
    ob_ref, oc_ref, sb_ref, sc_ref, sb_sc, sc_sc = refs[9 + len(prev):]
    _gdn_kernel(pb_ref, pg_ref, s0_ref, gvec_ref, bd_ref, eb_ref, eg_ref, *prev[:1], ob_ref, sb_ref, sb_sc,
                nbs=nbs, chunk=chunk, n_x=2, n_consts=4, has_prev=has_prev)
    _rwkv_kernel(pc_ref, s0_ref, cvec_ref, bd_ref, *prev[1:], oc_ref, sc_ref, sc_sc,
                 nbs=nbs, chunk=chunk, n_x=1, n_consts=2, has_prev=has_prev)


def _mixers_call(pb3, pg3, pc3, s0, gvec, cvec, bd, eb, eg, prev_b, prev_c, l, *, nbs, chunk):
    nseq, length, _ = pb3.shape
    blk = lambda b, c: (b, c, 0)
    state_block = (nbs, HEADS, HEAD_DIM, HEAD_DIM)
    in_specs = [pl.BlockSpec((nbs, chunk, x.shape[2]), blk) for x in (pb3, pg3, pc3)]
    in_specs.append(pl.BlockSpec(state_block, lambda b, c: (b, 0, 0, 0)))
    in_specs += [_layer_block(gvec, l), _layer_block(cvec, l), _whole(bd), _whole(eb), _whole(eg)]
    operands = [pb3, pg3, pc3, s0, gvec, cvec, bd, eb, eg]
    if prev_b is None:
        state_shape = (nseq, HEADS, HEAD_DIM, HEAD_DIM)
        state_spec = pl.BlockSpec(state_block, lambda b, c: (b, 0, 0, 0))
    else:
        n_prev = prev_b.shape[0]
        in_specs += [pl.BlockSpec((n_prev,) + state_block, lambda b, c: (0, b, 0, 0, 0))] * 2
        operands += [prev_b, prev_c]
        state_shape = (n_prev + 1, nseq, HEADS, HEAD_DIM, HEAD_DIM)
        state_spec = pl.BlockSpec((n_prev + 1,) + state_block, lambda b, c: (0, b, 0, 0, 0))
    out_spec = pl.BlockSpec((nbs, chunk, WIDTH), blk)
    return pl.pallas_call(
        functools.partial(_mixers_kernel, nbs=nbs, chunk=chunk, has_prev=prev_b is not None),
        grid=(nseq // nbs, length // chunk),
        in_specs=in_specs,
        out_specs=[out_spec, out_spec, state_spec, state_spec],
        out_shape=[jax.ShapeDtypeStruct((nseq, length, WIDTH), F32)] * 2
        + [jax.ShapeDtypeStruct(state_shape, F32)] * 2,
        scratch_shapes=[pltpu.VMEM((nbs * HEADS, HEAD_DIM, HEAD_DIM), F32)] * 2,
        compiler_params=pltpu.CompilerParams(dimension_semantics=("parallel", "arbitrary"),
                                             vmem_limit_bytes=VMEM_LIMIT),
        name="mixers",
    )(*operands)


def _store_stacked(sout_ref, prev_ref, s_sc):
    if prev_ref is None:
        sout_ref[...] = s_sc[...]
    else:
        n_prev = prev_ref.shape[0]
        sout_ref[0:n_prev] = prev_ref[...]
        sout_ref[n_prev] = s_sc[...]


def _gdn_steps_kernel(*refs, steps, has_prev):
    (xq_ref, xk_ref, xv_ref, xg_ref, hq_ref, hk_ref, hv_ref, wq_ref, wk_ref, wv_ref, pg_ref, hp_ref, og_ref,
     s0_ref) = refs[:14]
    prev_ref = refs[14] if has_prev else None
    o_ref, sout_ref, s_sc = refs[14 + int(has_prev):]
    h = pl.program_id(0)
    n_hist = B_CONV - 1

    def conv_silu(x_ref, h_ref, w_ref, t):
        rows = [h_ref[i] for i in range(n_hist)] + [x_ref[i] for i in range(steps)]
        acc = rows[t] * w_ref[0]
        for i in range(1, B_CONV):
            acc = acc + rows[t + i] * w_ref[i]
        return jax.nn.silu(acc)

    s_sc[...] = s0_ref[...]
    hp = hp_ref[...]
    for t in range(steps):
        q = conv_silu(xq_ref, hq_ref, wq_ref, t)
        k = conv_silu(xk_ref, hk_ref, wk_ref, t)
        v = conv_silu(xv_ref, hv_ref, wv_ref, t)
        q = q * lax.rsqrt(jnp.sum(q * q, axis=0, keepdims=True) + L2_EPS) * (HEAD_DIM ** -0.5)
        k = k * lax.rsqrt(jnp.sum(k * k, axis=0, keepdims=True) + L2_EPS)
        beta = jax.nn.sigmoid(pg_ref[t, pl.ds(h, 1), :])
        a = jnp.exp(-jnp.exp(hp[:, 0:1]) * _softplus(pg_ref[t, pl.ds(HEADS + h, 1), :] + hp[:, 1:2]))
        ks = jnp.zeros_like(v)
        for j in range(HEAD_DIM):
            ks = ks + k[j:j + 1, :] * (a * s_sc[j])
        w = beta * (v - ks)
        o = jnp.zeros_like(v)
        for j in range(HEAD_DIM):
            sj = a * s_sc[j] + k[j:j + 1, :] * w
            s_sc[j] = sj
            o = o + q[j:j + 1, :] * sj
        on = o * lax.rsqrt(jnp.mean(o * o, axis=0, keepdims=True) + NORM_EPS) * og_ref[...]
        o_ref[t] = on * jax.nn.silu(xg_ref[t])
    _store_stacked(sout_ref, prev_ref, s_sc)


def _steps_state_specs(prev, nseq, l):
    state_block = (HEAD_DIM, HEAD_DIM, nseq)
    s0_spec = pl.BlockSpec((None, None) + state_block, lambda h: (l, h, 0, 0, 0))
    if prev is None:
        return (s0_spec, None, (HEADS,) + state_block,
                pl.BlockSpec((None,) + state_block, lambda h: (h, 0, 0, 0)))
    n_prev = prev.shape[0]
    return (s0_spec, pl.BlockSpec((n_prev, None) + state_block, lambda h: (0, h, 0, 0, 0)),
            (n_prev + 1, HEADS) + state_block,
            pl.BlockSpec((n_prev + 1, None) + state_block, lambda h: (0, h, 0, 0, 0)))


def _gdn_steps(pb, pg, conv_state, s0_t, conv_w, a_log, dt_bias, onorm_g, prev, l):
    nseq = conv_state.shape[0]
    steps = pb.shape[0] // nseq
    xt = jnp.transpose(pb.reshape(nseq, steps, 4 * WIDTH), (1, 2, 0))
    pgt = jnp.transpose(pg.reshape(nseq, steps, GATE_LANES)[:, :, :2 * HEADS], (1, 2, 0))
    ht = jnp.transpose(conv_state, (1, 2, 0))
    wt = conv_w[l][:, :, None]
    hp = jnp.stack([a_log[l], dt_bias[l]], axis=1)[:, None, :]
    og = onorm_g[l][:, None]
    hb = lambda group: (lambda h: (0, group * HEADS + h, 0))
    x_spec = lambda group: pl.BlockSpec((steps, HEAD_DIM, nseq), hb(group))
    h_spec = lambda group: pl.BlockSpec((B_CONV - 1, HEAD_DIM, nseq), hb(group))
    w_spec = lambda group: pl.BlockSpec((B_CONV, HEAD_DIM, 1), hb(group))
    s0_spec, prev_spec, state_shape, state_spec = _steps_state_specs(prev, nseq, l)
    in_specs = [x_spec(0), x_spec(1), x_spec(2), x_spec(3), h_spec(0), h_spec(1), h_spec(2),
                w_spec(0), w_spec(1), w_spec(2), _whole(pgt),
                pl.BlockSpec((None, 1, 2), lambda h: (h, 0, 0)), _whole(og), s0_spec]
    operands = [xt, xt, xt, xt, ht, ht, ht, wt, wt, wt, pgt, hp, og, s0_t]
    if prev is not None:
        in_specs.append(prev_spec)
        operands.append(prev)
    out_t, states = pl.pallas_call(
        functools.partial(_gdn_steps_kernel, steps=steps, has_prev=prev is not None),
        grid=(HEADS,),
        in_specs=in_specs,
        out_specs=[pl.BlockSpec((steps, HEAD_DIM, nseq), lambda h: (0, h, 0)), state_spec],
        out_shape=[jax.ShapeDtypeStruct((steps, WIDTH, nseq), F32), jax.ShapeDtypeStruct(state_shape, F32)],
        scratch_shapes=[pltpu.VMEM((HEAD_DIM, HEAD_DIM, nseq), F32)],
        compiler_params=pltpu.CompilerParams(dimension_semantics=("parallel",)),
        name="gdn_steps",
    )(*operands)
    return jnp.transpose(out_t, (2, 0, 1)).reshape(nseq * steps, WIDTH), states


def _rwkv_steps_kernel(*refs, steps, has_prev):
    (xr_ref, xk_ref, xv_ref, xl_ref, pr_ref, pk_ref, pv_ref, pl_ref, mr_ref, mk_ref, mv_ref, ml_ref,
     hp_ref, w2_ref, a2_ref, g2_ref, s0_ref) = refs[:17]
    prev_ref = refs[17] if has_prev else None
    o_ref, sout_ref, s_sc = refs[17 + int(has_prev):]

    def lerp(x_ref, p_ref, m_ref, t):
        x = x_ref[t]
        before = p_ref[...] if t == 0 else x_ref[t - 1]
        return x + (before - x) * m_ref[...]

    w0, a0, k_k, k_a, r_k, ln_g, ln_b = (hp_ref[i] for i in range(7))
    s_sc[...] = s0_ref[...]
    for t in range(steps):
        wl, al, gl = _lora_rows(lerp(xl_ref, pl_ref, ml_ref, t))
        w_log = -_softplus(-(w0 + _dot(w2_ref[...], jnp.tanh(wl).astype(BF16)))) - 0.5
        w = jnp.exp(-jnp.exp(w_log))
        a = jax.nn.sigmoid(a0 + _dot(a2_ref[...], al.astype(BF16)))
        gg = _dot(g2_ref[...], jax.nn.sigmoid(gl).astype(BF16))
        r = lerp(xr_ref, pr_ref, mr_ref, t)
        k = lerp(xk_ref, pk_ref, mk_ref, t)
        v = lerp(xv_ref, pv_ref, mv_ref, t)
        kk = k * k_k
        kk = kk * lax.rsqrt(jnp.sum(kk * kk, axis=0, keepdims=True) + L2_EPS)
        k2 = k * (1.0 + (a - 1.0) * k_a)
        b = kk * a
        sk = jnp.zeros_like(v)
        for j in range(HEAD_DIM):
            sk = sk + kk[j:j + 1, :] * s_sc[j]
        y = jnp.zeros_like(v)
        for j in range(HEAD_DIM):
            sj = w[j:j + 1, :] * s_sc[j] - b[j:j + 1, :] * sk + k2[j:j + 1, :] * v
            s_sc[j] = sj
            y = y + r[j:j + 1, :] * sj
        mean = jnp.mean(y, axis=0, keepdims=True)
        var = jnp.mean(jnp.square(y - mean), axis=0, keepdims=True)
        yn = (y - mean) * lax.rsqrt(var + GN_EPS) * ln_g + ln_b
        bonus = jnp.sum(r * k2 * r_k, axis=0, keepdims=True) * v
        o_ref[t] = (yn + bonus) * gg
    _store_stacked(sout_ref, prev_ref, s_sc)


def _rwkv_steps(pc, shift_state, s0_t, cvec, w2b, a2b, g2b, prev, l):
    nseq = shift_state.shape[0]
    steps = pc.shape[0] // nseq
    xt = jnp.transpose(pc.reshape(nseq, steps, C_PROJ), (1, 2, 0))
    pt = jnp.transpose(shift_state, (1, 0))
    mu = cvec[l, 0, 0:C_PROJ][:, None]
    hp = cvec[l, 0, C_PROJ:].reshape(7, WIDTH)[:, :, None]
    w2t = jnp.transpose(w2b[l], (1, 0))
    a2t = jnp.transpose(a2b[l], (1, 0))
    g2t = jnp.transpose(g2b[l], (1, 0))
    hb3 = lambda group: (lambda h: (0, group * HEADS + h, 0))
    hb2 = lambda group: (lambda h: (group * HEADS + h, 0))
    lora3 = lambda h: (0, 3, 0)
    lora2 = lambda h: (3, 0)
    s0_spec, prev_spec, state_shape, state_spec = _steps_state_specs(prev, nseq, l)
    in_specs = ([pl.BlockSpec((steps, HEAD_DIM, nseq), hb3(g)) for g in range(3)]
                + [pl.BlockSpec((steps, WIDTH, nseq), lora3)]
                + [pl.BlockSpec((HEAD_DIM, nseq), hb2(g)) for g in range(3)] + [pl.BlockSpec((WIDTH, nseq), lora2)]
                + [pl.BlockSpec((HEAD_DIM, 1), hb2(g)) for g in range(3)] + [pl.BlockSpec((WIDTH, 1), lora2)]
                + [pl.BlockSpec((7, HEAD_DIM, 1), lambda h: (0, h, 0)),
                   pl.BlockSpec((HEAD_DIM, C_DECAY_LORA), lambda h: (h, 0)),
                   pl.BlockSpec((HEAD_DIM, C_RATE_LORA), lambda h: (h, 0)),
                   pl.BlockSpec((HEAD_DIM, C_GATE_LORA), lambda h: (h, 0)), s0_spec])
    operands = [xt] * 4 + [pt] * 4 + [mu] * 4 + [hp, w2t, a2t, g2t, s0_t]
    if prev is not None:
        in_specs.append(prev_spec)
        operands.append(prev)
    out_t, states = pl.pallas_call(
        functools.partial(_rwkv_steps_kernel, steps=steps, has_prev=prev is not None),
        grid=(HEADS,),
        in_specs=in_specs,
        out_specs=[pl.BlockSpec((steps, HEAD_DIM, nseq), lambda h: (0, h, 0)), state_spec],
        out_shape=[jax.ShapeDtypeStruct((steps, WIDTH, nseq), F32), jax.ShapeDtypeStruct(state_shape, F32)],
        scratch_shapes=[pltpu.VMEM((HEAD_DIM, HEAD_DIM, nseq), F32)],
        compiler_params=pltpu.CompilerParams(dimension_semantics=("parallel",)),
        name="rwkv_steps",
    )(*operands)
    return jnp.transpose(out_t, (2, 0, 1)).reshape(nseq * steps, WIDTH), states


def _round_up(x, m):
    return (x + m - 1) // m * m


def _pool_offsets(stride):
    offs = []
    a = 0
    for k in range(len(POOL_WINDOWS)):
        a = _round_up(a + (1 << k) * stride, SUBLANES)
        offs.append(a)
    return offs


def _pool_kernel(hist_ref, dp_ref, wbd_ref, scale_ref, out_ref, s1, s2, *, rows, stride, start):
    offs = _pool_offsets(stride)
    d0 = offs[-1]
    n = d0 + rows
    dp = dp_ref[...]
    s1[0:d0, :] = hist_ref[0]
    s1[d0:n, :] = dp
    lane = lax.broadcasted_iota(jnp.int32, (1, WIDTH), 1)
    src, dst = s1, s2
    for k, a in enumerate(offs):
        sh = (1 << k) * stride
        shifted = jnp.where(lane >= k * POOL_GROUP, src[a - sh:n - sh, :], 0.0)
        if k < len(offs) - 1:
            dst[a:n, :] = src[a:n, :] + shifted
            src, dst = dst, src
        else:
            sums = src[a:n, :] + shifted
    assert stride & (stride - 1) == 0
    pos = start + (lax.broadcasted_iota(jnp.int32, (rows, WIDTH), 0) >> (stride.bit_length() - 1))
    group = lax.broadcasted_iota(jnp.int32, (rows, WIDTH), 1) >> POOL_GROUP_SHIFT
    window = jnp.left_shift(POOL_WINDOWS[0], group)
    cnt = jnp.minimum(pos + 1, window).astype(F32)
    diff = sums / cnt - dp
    out_ref[...] = _dot(diff.astype(BF16), wbd_ref[...]) * scale_ref[...]


def _pool(hist, dp, wbd, scale, l, *, nseq, rows, stride, start):
    d0 = _pool_offsets(stride)[-1]
    return pl.pallas_call(
        functools.partial(_pool_kernel, rows=rows, stride=stride, start=start),
        grid=(nseq,),
        in_specs=[pl.BlockSpec((1, d0, WIDTH), lambda b: (b, 0, 0)),
                  pl.BlockSpec((rows, WIDTH), lambda b: (b, 0)),
                  _layer_block(wbd, l), _layer_block(scale, l)],
        out_specs=pl.BlockSpec((rows, WIDTH), lambda b: (b, 0)),
        out_shape=jax.ShapeDtypeStruct((nseq * rows, WIDTH), F32),
        scratch_shapes=[pltpu.VMEM((d0 + rows, WIDTH), F32)] * 2,
        compiler_params=pltpu.CompilerParams(dimension_semantics=("parallel",)),
        name="pool",
    )(hist, dp, wbd, scale)


def _ffn_kernel(x_ref, ma_ref, mb_ref, mc_ref, md_ref, wo_ref, g2_ref, wu_ref, wd_ref, gf_ref,
                o_ref, *, tf, final):
    mixed = None
    for i, m_ref in enumerate((ma_ref, mb_ref, mc_ref, md_ref)):
        part = _dot(m_ref[...].astype(BF16), wo_ref[i * WIDTH:(i + 1) * WIDTH, :])
        mixed = part if mixed is None else mixed + part
    x = x_ref[...] + mixed
    hm = _rms(x, g2_ref[...]).astype(BF16)
    down = None
    for j in range(D_FF // tf):
        up = jnp.maximum(_dot(hm, wu_ref[:, j * tf:(j + 1) * tf]), 0.0)
        part = _dot((up * up).astype(BF16), wd_ref[j * tf:(j + 1) * tf, :])
        down = part if down is None else down + part
    x = x + down
    if final:
        x = _rms(x, gf_ref[...])
    o_ref[...] = x


def _ffn(x, mixed, wo, g2, wu, wd, gf, l):
    t = x.shape[0]
    tm = min(ROW_TILE, t)
    row = lambda i: (i, 0)
    return pl.pallas_call(
        functools.partial(_ffn_kernel, tf=FF_TILE, final=l == DEPTH - 1),
        grid=(t // tm,),
        in_specs=[pl.BlockSpec((tm, D_MODEL), row)] + [pl.BlockSpec((tm, WIDTH), row)] * 4
        + [_layer_block(wo, l, single_buffer=True), _layer_block(g2, l),
           _layer_block(wu, l, single_buffer=True), _layer_block(wd, l, single_buffer=True), _whole(gf)],
        out_specs=pl.BlockSpec((tm, D_MODEL), row),
        out_shape=jax.ShapeDtypeStruct((t, D_MODEL), F32),
        compiler_params=pltpu.CompilerParams(dimension_semantics=("parallel",),
                                             vmem_limit_bytes=VMEM_LIMIT),
        name="ffn",
    )(x, *mixed, wo, g2, wu, wd, gf)


def _rows(v):
    return v[:, None, :]


def kernel(x_prompt, x_sample, state_b_conv, state_b_ssm, state_c_shift, state_c_wkv, state_d_pool, norm1_g, w_in, a_ws, a_bs, a_vnorm_g, b_conv_w, b_a_log, b_dt_bias, b_onorm_g, c_mu, c_w0, c_w2, c_a0, c_a2, c_g2, c_k_k, c_k_a, c_r_k, c_ln_g, c_ln_b, d_w, d_scale, w_out, norm2_g, w_up, w_down, final_g):
    nb, seq, _ = x_prompt.shape
    ns, dseq, _ = x_sample.shape
    a_chunk = a_ws.shape[-1]

    a_end = 2 * WIDTH
    b_end = a_end + 4 * WIDTH
    c_off = b_end + 2 * HEADS
    d_off = c_off + C_PROJ
    w_in_b = w_in.astype(BF16)
    w_cols = [w_in_b[:, :, :a_end], w_in_b[:, :, a_end:b_end], w_in_b[:, :, c_off:d_off], w_in_b[:, :, d_off:],
              jnp.pad(w_in_b[:, :, b_end:c_off], ((0, 0), (0, 0), (0, GATE_LANES - 2 * HEADS)))]
    wo = w_out.astype(BF16)
    wu = w_up.astype(BF16)
    wd = w_down.astype(BF16)
    g1 = _rows(norm1_g)
    g2 = _rows(norm2_g)
    gf = final_g[None, :]
    vg = _rows(a_vnorm_g)
    causal = jnp.tril(jnp.ones((a_chunk, a_chunk), dtype=bool))
    wm = jnp.where(causal, a_ws, 0.0)
    bias_t = jnp.repeat(jnp.transpose(a_bs, (0, 2, 1)), HEAD_DIM, axis=2)
    mm_p = jnp.transpose(wm, (0, 2, 1, 3)).reshape(DEPTH, a_chunk, HEADS * a_chunk).astype(BF16)
    srow = jnp.arange(ns * dseq)
    step_onehot = (srow[:, None] % dseq == jnp.arange(dseq)[None, :]).astype(F32)
    same_seq = (srow[:, None] // dseq) == (srow[None, :] // dseq)
    mm_s = jnp.einsum('it,lhts,js->lihj', step_onehot, wm[:, :, :dseq, :dseq], step_onehot,
                      precision=lax.Precision.HIGHEST)
    mm_s = jnp.where(same_seq[None, :, None, :], mm_s, 0.0).reshape(
        DEPTH, ns * dseq, HEADS * ns * dseq).astype(BF16)
    bias_s = jnp.tile(bias_t[:, :dseq], (1, ns, 1))
    lead = jnp.zeros((DEPTH, HEADS), F32)
    tail = jnp.zeros((DEPTH, GATE_LANES - 2 * HEADS), F32)
    gvec = jnp.concatenate([lead, b_a_log, tail, lead, b_dt_bias, tail, jnp.tile(b_onorm_g, (1, HEADS))],
                           axis=1)[:, None, :]
    cvec = jnp.concatenate([c_mu, c_w0, c_a0, c_k_k, c_k_a, c_r_k.reshape(DEPTH, WIDTH), c_ln_g, c_ln_b],
                           axis=1)[:, None, :]
    c_w2b, c_a2b, c_g2b = c_w2.astype(BF16), c_a2.astype(BF16), c_g2.astype(BF16)
    n_groups = len(POOL_WINDOWS)
    wbd = jnp.einsum('lgcd,gh->lgchd', d_w, jnp.eye(n_groups, dtype=F32)).reshape(
        DEPTH, WIDTH, WIDTH).astype(BF16)
    dscale = _rows(d_scale)
    lane_head = jnp.arange(WIDTH) // HEAD_DIM
    gate_lane = jnp.arange(GATE_LANES)
    bd = (lane_head[:, None] == lane_head[None, :]).astype(BF16)
    eb = jnp.tile((gate_lane[:, None] == lane_head[None, :]).astype(BF16), (2, 1))
    eg = jnp.tile((gate_lane[:, None] == lane_head[None, :] + HEADS).astype(BF16), (3, 1))

    def run_group(x, l, grp, prev_b, prev_c):
        nseq, length = grp["nseq"], grp["length"]
        conv_state = grp["b_conv"](l)
        if grp["prep"]:
            a_out, a_v, pb, pc, pd, pg, tails, tails_c = _inproj(
                x, g1, w_cols, grp["mm"], grp["bias"], vg, l,
                token_prep=(b_conv_w, gvec, bd, conv_state, length, cvec, c_w2b, c_a2b, c_g2b,
                            grp["c_shift"](l)[:, None, :]))
            b_rows = tails.reshape(nseq, -1, SUBLANES, B_QKV)[:, -1]
            c_last = tails_c.reshape(nseq, -1, SUBLANES, C_PROJ)[:, -1, SUBLANES - 1]
        else:
            a_out, a_v, pb, pc, pd, pg = _inproj(x, g1, w_cols, grp["mm"], grp["bias"], vg, l)
            b_rows = pb.reshape(nseq, length, 4 * WIDTH)[:, :, :B_QKV]
            c_last = pc.reshape(nseq, length, C_PROJ)[:, length - 1]
        b_tail = jnp.concatenate([conv_state, b_rows], axis=1)[:, -(B_CONV - 1):]
        pb3 = pb.reshape(nseq, length, 4 * WIDTH)
        pc3 = pc.reshape(nseq, length, pc.shape[1])
        if grp["prep"]:
            zero_state = jnp.zeros((nseq, HEADS, HEAD_DIM, HEAD_DIM), F32)
            b_out, c_out, b_state, c_state = _mixers_call(
                pb3, pg.reshape(nseq, length, GATE_LANES), pc3, zero_state, gvec, cvec, bd, eb, eg, prev_b, prev_c, l,
                nbs=PROMPT_SEQS_PER_STEP, chunk=MIX_CHUNK)
            b_out = b_out.reshape(nseq * length, WIDTH)
            c_out = c_out.reshape(nseq * length, WIDTH)
        else:
            b_out, b_state = _gdn_steps(pb, pg, conv_state, grp["b_ssm_t"], b_conv_w, b_a_log, b_dt_bias,
                                        b_onorm_g, prev_b, l)
            c_out, c_state = _rwkv_steps(pc, grp["c_shift"](l), grp["c_wkv_t"], cvec, c_w2b, c_a2b, c_g2b,
                                         prev_c, l)
        d_out = grp["pool"](pd, l)
        x = _ffn(x, (a_out, b_out, c_out, d_out), wo, g2, wu, wd, gf, l)
        return x, a_v, b_tail, c_last, pd.reshape(nseq, length, WIDTH), b_state, c_state

    d0 = _pool_offsets(1)[-1]
    zero_hist = jnp.zeros((nb, d0, WIDTH), F32)
    prompt = dict(
        nseq=nb, length=seq, mm=mm_p, bias=bias_t, prep=True,
        b_conv=lambda l: jnp.zeros((nb, B_CONV - 1, B_QKV), F32), c_shift=lambda l: jnp.zeros((nb, C_PROJ), F32),
        pool=lambda pd, l: _pool(zero_hist, pd, wbd, dscale, l, nseq=nb, rows=seq, stride=1, start=0))

    def sample_pool(pd, l):
        pd_t = jnp.transpose(pd.reshape(ns, dseq, WIDTH), (1, 0, 2)).reshape(dseq * ns, WIDTH)
        hist = jnp.transpose(state_d_pool[l], (1, 0, 2)).reshape(1, POOL_BUF * ns, WIDTH)
        out_t = _pool(hist, pd_t, wbd, dscale, l, nseq=1, rows=dseq * ns, stride=ns, start=PAST_LEN)
        return jnp.transpose(out_t.reshape(dseq, ns, WIDTH), (1, 0, 2)).reshape(ns * dseq, WIDTH)

    sample = dict(
        nseq=ns, length=dseq, mm=mm_s, bias=bias_s, prep=False,
        b_ssm_t=jnp.transpose(state_b_ssm, (0, 2, 3, 4, 1)), c_wkv_t=jnp.transpose(state_c_wkv, (0, 2, 3, 4, 1)),
        b_conv=lambda l: state_b_conv[l], c_shift=lambda l: state_c_shift[l], pool=sample_pool)

    xp = x_prompt.reshape(nb * seq, D_MODEL)
    xs = x_sample.reshape(ns * dseq, D_MODEL)
    p_bc, p_cs, p_dp, s_av, s_bc, s_cs, s_dp = [], [], [], [], [], [], []
    p_bs = p_cw = s_bs = s_cw = None
    for l in range(DEPTH):
        xp, _, b_tail, c_last, pd3, p_bs, p_cw = run_group(xp, l, prompt, p_bs, p_cw)
        p_bs = p_bs if l else p_bs[None]
        p_cw = p_cw if l else p_cw[None]
        p_bc.append(b_tail)
        p_cs.append(c_last)
        p_dp.append(pd3[:, seq - POOL_BUF:])

        xs, a_v, b_tail, c_last, pd3, s_bs, s_cw = run_group(xs, l, sample, s_bs, s_cw)
        s_bs = s_bs if l else s_bs[None]
        s_cw = s_cw if l else s_cw[None]
        s_av.append(a_v.reshape(ns, dseq, WIDTH))
        s_bc.append(b_tail)
        s_cs.append(c_last)
        s_dp.append(jnp.concatenate([state_d_pool[l], pd3], axis=1)[:, -POOL_BUF:])

    return (xp.reshape(nb, seq, D_MODEL), xs.reshape(ns, dseq, D_MODEL),
            jnp.stack(p_bc), p_bs, jnp.stack(p_cs), p_cw, jnp.stack(p_dp),
            jnp.stack(s_av), jnp.stack(s_bc), jnp.transpose(s_bs, (0, 4, 1, 2, 3)), jnp.stack(s_cs),
            jnp.transpose(s_cw, (0, 4, 1, 2, 3)), jnp.stack(s_dp))
```
